```python
import math
import jax, jax.numpy as jnp
from jax import lax
import numpy as np

D_MODEL = 2048
BATCH = 8
SEQ = 2048
DEPTH = 1

MEM_LEN = 256
CONV_CH = 1024
CONV_WIDTH = 31
N_Q_HEADS = 16
N_KV_HEADS = 2
HEAD_DIM = 64
ATTN_WIDTH = N_Q_HEADS * HEAD_DIM
KV_WIDTH = N_KV_HEADS * HEAD_DIM
WINDOW = 128
Q_BLOCK = WINDOW
NUM_BUCKETS = 32
MAX_DISTANCE = 128
X_HEADS = 4
X_HEAD_DIM = 128
X_WIDTH = X_HEADS * X_HEAD_DIM
N_GROUPS = 4
EXPERTS_PER_GROUP = 8
N_EXPERTS = N_GROUPS * EXPERTS_PER_GROUP
TOP_K = 2
D_EXPERT = 512
DISPATCH_BLOCK = 256
N_BRANCHES = 2
IN_WIDTH = 2 * CONV_CH + ATTN_WIDTH + 2 * KV_WIDTH + N_BRANCHES * D_MODEL
EPS = 1e-6
NEG = -1e30

kernel_name = "hybrid_conv_swa_memxattn_hmoe"


def rms_norm(x, g):
    xf = x.astype(jnp.float32)
    y = xf * lax.rsqrt(jnp.mean(xf * xf, axis=-1, keepdims=True) + EPS)
    return (y * g.astype(jnp.float32)).astype(x.dtype)


def layer_norm(x, g, b):
    xf = x.astype(jnp.float32)
    mu = jnp.mean(xf, axis=-1, keepdims=True)
    var = jnp.mean(jnp.square(xf - mu), axis=-1, keepdims=True)
    y = (xf - mu) * lax.rsqrt(var + EPS)
    return (y * g.astype(jnp.float32) + b.astype(jnp.float32)).astype(x.dtype)


def t5_bucket(dist):
    n = jnp.maximum(dist, 0)
    max_exact = NUM_BUCKETS // 2
    large = max_exact + (jnp.log(jnp.maximum(n, 1).astype(jnp.float32) / max_exact)
                         / math.log(MAX_DISTANCE / max_exact)
                         * (NUM_BUCKETS - max_exact)).astype(jnp.int32)
    large = jnp.minimum(large, NUM_BUCKETS - 1)
    return jnp.where(n < max_exact, n, large)


def conformer_conv(a, b, dw_w, dw_b, ln_g, ln_b, w_out):
    u = a * jax.nn.sigmoid(b)
    u = lax.conv_general_dilated(
        u, dw_w.astype(u.dtype), window_strides=(1,), padding=[(CONV_WIDTH - 1, 0)],
        dimension_numbers=('NWC', 'WIO', 'NWC'), feature_group_count=CONV_CH) + dw_b
    u = jax.nn.silu(layer_norm(u, ln_g, ln_b))
    return u @ w_out


def sliding_window_gqa(q, k, v, q_g, k_g, sinks, rel_bias):
    B, T = q.shape[:2]
    nb = T // Q_BLOCK
    G = N_Q_HEADS // N_KV_HEADS
    q = rms_norm(q.reshape(B, T, N_Q_HEADS, HEAD_DIM), q_g)
    q = q.reshape(B, nb, Q_BLOCK, N_KV_HEADS, G, HEAD_DIM)
    k = rms_norm(k.reshape(B, T, N_KV_HEADS, HEAD_DIM), k_g)
    v = v.reshape(B, T, N_KV_HEADS, HEAD_DIM)

    def band(t):
        tp = jnp.pad(t, ((0, 0), (Q_BLOCK, 0), (0, 0), (0, 0)))
        prev = tp[:, :T].reshape(B, nb, Q_BLOCK, N_KV_HEADS, HEAD_DIM)
        cur = t.reshape(B, nb, Q_BLOCK, N_KV_HEADS, HEAD_DIM)
        return jnp.concatenate([prev, cur], axis=2)

    kb, vb = band(k), band(v)
    qi = jnp.arange(Q_BLOCK)[:, None]
    kj = jnp.arange(2 * Q_BLOCK)[None, :]
    dist = qi + Q_BLOCK - kj
    bias = rel_bias[t5_bucket(dist)].astype(jnp.float32)
    bias = bias.transpose(2, 0, 1).reshape(N_KV_HEADS, G, Q_BLOCK, 2 * Q_BLOCK)
    key_pos = jnp.arange(nb)[:, None, None] * Q_BLOCK + kj[None] - Q_BLOCK
    valid = (dist >= 0) & (dist < WINDOW) & (key_pos >= 0)

    s = jnp.einsum('bnqkgd,bnskd->bnkgqs', q.astype(jnp.float32), kb.astype(jnp.float32))
    s = s * (HEAD_DIM ** -0.5) + bias
    s = jnp.where(valid[None, :, None, None], s, NEG)
    sink = sinks.astype(jnp.float32).reshape(N_KV_HEADS, G, 1, 1)
    m = jnp.maximum(s.max(axis=-1, keepdims=True), sink)
    p = jnp.exp(s - m)
    p = p / (p.sum(axis=-1, keepdims=True) + jnp.exp(sink - m))
    o = jnp.einsum('bnkgqs,bnskd->bnqkgd', p.astype(vb.dtype), vb)
    return o.reshape(B, T, ATTN_WIDTH)


def memory_cross_attention(xn, memn, w_xq, w_xkv, qg, kg, w_xo):
    B, T, _ = xn.shape
    M = memn.shape[1]
    q = rms_norm((xn @ w_xq).reshape(B, T, X_HEADS, X_HEAD_DIM), qg)
    kv = (memn @ w_xkv).reshape(B, M, 2, X_HEADS, X_HEAD_DIM)
    k = rms_norm(kv[:, :, 0], kg)
    v = kv[:, :, 1]
    s = jnp.einsum('bthd,bmhd->bhtm', q.astype(jnp.float32), k.astype(jnp.float32))
    p = jax.nn.softmax(s * (X_HEAD_DIM ** -0.5), axis=-1).astype(v.dtype)
    o = jnp.einsum('bhtm,bmhd->bthd', p, v).reshape(B, T, X_WIDTH)
    return o @ w_xo


def hierarchical_moe(xn, w_rg, b_rg, w_re, b_re, w_gu, w_dn):
    B, T, D = xn.shape
    xt = xn.reshape(-1, D)
    N = xt.shape[0]
    lg = (xt @ w_rg).astype(jnp.float32) + b_rg.astype(jnp.float32)
    pg_top, grp = lax.top_k(jax.nn.softmax(lg, axis=-1), 1)
    le = ((xt @ w_re).astype(jnp.float32) + b_re.astype(jnp.float32)).reshape(N, N_GROUPS, EXPERTS_PER_GROUP)
    le_sel = jnp.take_along_axis(le, grp[:, :, None], axis=1)[:, 0]
    pe_top, idx = lax.top_k(jax.nn.softmax(le_sel, axis=-1), TOP_K)
    pe_top = pe_top / pe_top.sum(axis=-1, keepdims=True)
    gate = pg_top * pe_top
    eid = grp * EXPERTS_PER_GROUP + idx

    A = N * TOP_K
    flat_e = eid.reshape(A)
    flat_tok = jnp.repeat(jnp.arange(N, dtype=jnp.int32), TOP_K)
    flat_w = gate.reshape(A)
    order = jnp.argsort(flat_e)
    se, stok, sw = flat_e[order], flat_tok[order], flat_w[order]
    counts = jnp.bincount(flat_e, length=N_EXPERTS)
    padded = (counts + DISPATCH_BLOCK - 1) // DISPATCH_BLOCK * DISPATCH_BLOCK
    pad_end = jnp.cumsum(padded)
    pad_start = pad_end - padded
    start = jnp.cumsum(counts) - counts
    dest = pad_start[se] + jnp.arange(A, dtype=jnp.int32) - start[se]
    n_blocks = -(-A // DISPATCH_BLOCK) + N_EXPERTS
    rows = n_blocks * DISPATCH_BLOCK
    row_tok = jnp.full((rows,), N, dtype=jnp.int32).at[dest].set(stok)
    x_pad = jnp.concatenate([xt, jnp.zeros((1, D), xt.dtype)], axis=0)
    xs = x_pad[row_tok].reshape(n_blocks, DISPATCH_BLOCK, D)
    blk_e = jnp.minimum(jnp.searchsorted(pad_end, jnp.arange(n_blocks) * DISPATCH_BLOCK, side='right'),
                        N_EXPERTS - 1)

    def expert_block(args):
        xb, e = args
        gu = xb @ w_gu[e]
        g, u = gu[:, :D_EXPERT], gu[:, D_EXPERT:]
        return (jax.nn.silu(g) * u) @ w_dn[e]

    ys = lax.map(expert_block, (xs, blk_e)).reshape(rows, D)
    contrib = ys[dest] * sw[:, None].astype(ys.dtype)
    out = jax.ops.segment_sum(contrib, stok, num_segments=N)
    return out.reshape(B, T, D)


def setup_inputs(seed: int = 0) -> dict:
    key = jax.random.key(seed)
    ks = jax.random.split(key, 32)
    nrm = jax.random.normal
    f32 = jnp.float32
    L, D = DEPTH, D_MODEL

    def gain(k, shape):
        return 1.0 + 0.02 * nrm(k, shape, f32)

    return {
        'x': nrm(ks[0], (BATCH, SEQ, D), f32),
        'mem': nrm(ks[1], (BATCH, MEM_LEN, D), f32),
        'rel_bias': 0.1 * nrm(ks[2], (NUM_BUCKETS, N_Q_HEADS), f32),
        'norm_mix_g': gain(ks[3], (L, D)),
        'w_in': nrm(ks[4], (L, D, IN_WIDTH), f32) * D ** -0.5,
        'conv_dw_w': nrm(ks[5], (L, CONV_WIDTH, 1, CONV_CH), f32) * CONV_WIDTH ** -0.5,
        'conv_dw_b': 0.02 * nrm(ks[6], (L, CONV_CH), f32),
        'conv_ln_g': gain(ks[7], (L, CONV_CH)),
        'conv_ln_b': 0.02 * nrm(ks[8], (L, CONV_CH), f32),
        'w_conv_out': nrm(ks[9], (L, CONV_CH, D), f32) * CONV_CH ** -0.5,
        'q_norm_g': gain(ks[10], (L, HEAD_DIM)),
        'k_norm_g': gain(ks[11], (L, HEAD_DIM)),
        'attn_sinks': 0.5 * nrm(ks[12], (L, N_Q_HEADS), f32),
        'w_attn_out': nrm(ks[13], (L, ATTN_WIDTH, D), f32) * ATTN_WIDTH ** -0.5,
        'w_mix_out': nrm(ks[14], (L, D, D), f32) * D ** -0.5,
        'norm_x_g': gain(ks[15], (L, D)),
        'norm_mem_g': gain(ks[16], (L, D)),
        'w_xq': nrm(ks[17], (L, D, X_WIDTH), f32) * D ** -0.5,
        'w_xkv': nrm(ks[18], (L, D, 2 * X_WIDTH), f32) * D ** -0.5,
        'xq_norm_g': gain(ks[19], (L, X_HEAD_DIM)),
        'xk_norm_g': gain(ks[20], (L, X_HEAD_DIM)),
        'w_xo': nrm(ks[21], (L, X_WIDTH, D), f32) * X_WIDTH ** -0.5,
        'norm_moe_g': gain(ks[22], (L, D)),
        'w_router_group': nrm(ks[23], (L, D, N_GROUPS), f32) * D ** -0.5,
        'b_router_group': 0.01 * nrm(ks[24], (L, N_GROUPS), f32),
        'w_router_expert': nrm(ks[25], (L, D, N_EXPERTS), f32) * D ** -0.5,
        'b_router_expert': 0.01 * nrm(ks[26], (L, N_EXPERTS), f32),
        'w_expert_gu': nrm(ks[27], (L, N_EXPERTS, D, 2 * D_EXPERT), f32) * D ** -0.5,
        'w_expert_down': nrm(ks[28], (L, N_EXPERTS, D_EXPERT, D), f32) * D_EXPERT ** -0.5,
    }


def reference(x, mem, rel_bias, norm_mix_g, w_in, conv_dw_w, conv_dw_b, conv_ln_g, conv_ln_b,
              w_conv_out, q_norm_g, k_norm_g, attn_sinks, w_attn_out, w_mix_out, norm_x_g,
              norm_mem_g, w_xq, w_xkv, xq_norm_g, xk_norm_g, w_xo, norm_moe_g, w_router_group,
              b_router_group, w_router_expert, b_router_expert, w_expert_gu, w_expert_down):
    B, T, D = x.shape
    splits = [CONV_CH, 2 * CONV_CH, 2 * CONV_CH + ATTN_WIDTH,
              2 * CONV_CH + ATTN_WIDTH + KV_WIDTH, 2 * CONV_CH + ATTN_WIDTH + 2 * KV_WIDTH]
    h = x
    for l in range(DEPTH):
        xn = rms_norm(h, norm_mix_g[l])
        proj = xn @ w_in[l]
        ca, cb, q, k, v, gl = jnp.split(proj, splits, axis=-1)
        y_conv = conformer_conv(ca, cb, conv_dw_w[l], conv_dw_b[l], conv_ln_g[l], conv_ln_b[l], w_conv_out[l])
        y_attn = sliding_window_gqa(q, k, v, q_norm_g[l], k_norm_g[l], attn_sinks[l], rel_bias) @ w_attn_out[l]
        g = jax.nn.sigmoid(gl.reshape(B, T, N_BRANCHES, D))
        merged = g[:, :, 0] * y_conv + g[:, :, 1] * y_attn
        h = h + merged @ w_mix_out[l]
        h = h + memory_cross_attention(rms_norm(h, norm_x_g[l]), rms_norm(mem, norm_mem_g[l]),
                                       w_xq[l], w_xkv[l], xq_norm_g[l], xk_norm_g[l], w_xo[l])
        h = h + hierarchical_moe(rms_norm(h, norm_moe_g[l]), w_router_group[l], b_router_group[l],
                                 w_router_expert[l], b_router_expert[l], w_expert_gu[l], w_expert_down[l])
    return h
```

```python
import functools
import math

import jax
import jax.numpy as jnp
import numpy as np
from jax import lax
from jax.experimental import pallas as pl
from jax.experimental.pallas import tpu as pltpu

EPS = 1e-6
NEG = -1e30

CONV_CH = 1024
CONV_WIDTH = 31
N_Q_HEADS = 16
N_KV_HEADS = 2
HEAD_DIM = 64
ATTN_WIDTH = N_Q_HEADS * HEAD_DIM
KV_WIDTH = N_KV_HEADS * HEAD_DIM
WINDOW = 128
NUM_BUCKETS = 32
MAX_DISTANCE = 128
X_HEADS = 4
X_HEAD_DIM = 128
X_WIDTH = X_HEADS * X_HEAD_DIM
N_GROUPS = 4
EXPERTS_PER_GROUP = 8
N_EXPERTS = N_GROUPS * EXPERTS_PER_GROUP
D_EXPERT = 512
DISPATCH_BLOCK = 256

LANES = 128
CONV_HALO = 32
VMEM_LIMIT = 56 * 1024 * 1024

BF16 = jnp.bfloat16
F32 = jnp.float32


def _cparams(sem):
    return pltpu.CompilerParams(dimension_semantics=sem, vmem_limit_bytes=VMEM_LIMIT)


def _const_spec(shape):
    nd = len(shape)
    return pl.BlockSpec(shape, lambda *_: (0,) * nd, pipeline_mode=pl.Buffered(1))


def _rms_scale(x):
    return lax.rsqrt(jnp.mean(x * x, axis=-1, keepdims=True) + EPS)


def _group_sumsq(y, bd):
    sq = y * y
    hi = sq.astype(BF16)
    lo = (sq - hi.astype(F32)).astype(BF16)
    return (jnp.dot(hi, bd, preferred_element_type=F32) + jnp.dot(lo, bd, preferred_element_type=F32))


def _norm_proj_kernel(x_ref, g_ref, *refs, mode, head_dim):
    if mode == "glu":
        wa_ref, wb_ref, o_ref, xn_ref = refs
    elif mode == "sigmoid":
        w_ref, o_ref, xn_ref = refs
    else:
        w_ref, gain_ref, flag_ref, bd_ref, o_ref, xn_ref = refs

    @pl.when(pl.program_id(1) == 0)
    def _():
        x = x_ref[...]
        xn_ref[...] = (x * _rms_scale(x) * g_ref[...]).astype(BF16)

    xn = xn_ref[...]
    if mode == "glu":
        a = jnp.dot(xn, wa_ref[...], preferred_element_type=F32)
        b = jnp.dot(xn, wb_ref[...], preferred_element_type=F32)
        o_ref[...] = (a * jax.nn.sigmoid(b)).astype(o_ref.dtype)
    elif mode == "sigmoid":
        y = jnp.dot(xn, w_ref[...], preferred_element_type=F32)
        o_ref[...] = jax.nn.sigmoid(y).astype(o_ref.dtype)
    else:
        y = jnp.dot(xn, w_ref[...], preferred_element_type=F32)
        bd = bd_ref[...]
        for c in range(y.shape[1] // LANES):
            sl = slice(c * LANES, (c + 1) * LANES)
            yc = y[:, sl]
            ss = _group_sumsq(yc, bd)
            normed = yc * lax.rsqrt(ss * (1.0 / head_dim) + EPS) * gain_ref[:, sl]
            o_ref[:, sl] = jnp.where(flag_ref[:, sl] > 0.0, normed, yc).astype(o_ref.dtype)


def _norm_proj(x, g, weights, *, mode, tm, tn, extras=(), head_dim=1):
    m, d = x.shape
    n = weights[0].shape[1]
    grid = (m // tm, n // tn)
    in_specs = [pl.BlockSpec((tm, d), lambda i, j: (i, 0)), pl.BlockSpec((1, d), lambda i, j: (0, 0))]
    in_specs += [pl.BlockSpec((d, tn), lambda i, j: (0, j)) for _ in weights]
    for e in extras:
        if e.shape[0] == 1:
            in_specs.append(pl.BlockSpec((1, tn), lambda i, j: (0, j)))
        else:
            in_specs.append(pl.BlockSpec(e.shape, lambda i, j: (0, 0)))
    return pl.pallas_call(
        functools.partial(_norm_proj_kernel, mode=mode, head_dim=head_dim),
        grid=grid,
        in_specs=in_specs,
        out_specs=pl.BlockSpec((tm, tn), lambda i, j: (i, j)),
        out_shape=jax.ShapeDtypeStruct((m, n), BF16),
        scratch_shapes=[pltpu.VMEM((tm, d), BF16)],
        compiler_params=_cparams(("parallel", "arbitrary")),
        name=f"norm_proj_{mode}",
    )(x, g.reshape(1, d), *weights, *extras)


def _block_diag_ones(group):
    r = np.arange(LANES)
    return jnp.asarray((r[:, None] // group) == (r[None, :] // group), dtype=BF16)


def _swa_kernel(sink_ref, q_ref, kp_ref, kc_ref, vp_ref, vc_ref, bias_ref, o_ref, *, n_chunks):
    first = pl.program_id(1) == 0
    qb = q_ref.shape[0]
    kcat = jnp.concatenate([kp_ref[...], kc_ref[...]], axis=0)
    vcat = jnp.concatenate([vp_ref[...], vc_ref[...]], axis=0)
    lane = lax.broadcasted_iota(jnp.int32, (qb, LANES), 1)
    row = lax.broadcasted_iota(jnp.int32, (2 * qb, 1), 0)
    col = lax.broadcasted_iota(jnp.int32, (2 * qb, 2 * qb), 1)
    no_prev = jnp.where(jnp.logical_and(first, col < qb), NEG, 0.0)
    for c in range(n_chunks):
        q2 = q_ref[:, c * LANES:(c + 1) * LANES]
        zero = jnp.zeros_like(q2)
        qs = jnp.concatenate([jnp.where(lane < HEAD_DIM, q2, zero), jnp.where(lane >= HEAD_DIM, q2, zero)], axis=0)
        s = lax.dot_general(qs, kcat, (((1,), (1,)), ((), ())), preferred_element_type=F32)
        s = s + bias_ref[c] + no_prev
        sink = jnp.where(row < qb, sink_ref[c], sink_ref[n_chunks + c])
        m = jnp.maximum(jnp.max(s, axis=-1, keepdims=True), sink)
        p = jnp.exp(s - m)
        den = jnp.sum(p, axis=-1, keepdims=True) + jnp.exp(sink - m)
        o2 = jnp.dot(p.astype(BF16), vcat, preferred_element_type=F32) / den
        o = jnp.where(lane < HEAD_DIM, o2[:qb], o2[qb:])
        o_ref[:, c * LANES:(c + 1) * LANES] = o.astype(o_ref.dtype)


def _t5_bucket_np(dist):
    n = np.maximum(dist, 0)
    max_exact = NUM_BUCKETS // 2
    large = max_exact + (np.log(np.maximum(n, 1).astype(np.float32) / max_exact)
                         / math.log(MAX_DISTANCE / max_exact) * (NUM_BUCKETS - max_exact)).astype(np.int32)
    large = np.minimum(large, NUM_BUCKETS - 1)
    return np.where(n < max_exact, n, large)


def _swa(qkv, rel_bias, sinks, batch, seq):
    qb = WINDOW
    nb = seq // qb
    n_chunks = N_Q_HEADS // 2
    qi = np.arange(qb)[:, None]
    kj = np.arange(2 * qb)[None, :]
    dist = qi + qb - kj
    valid = (dist >= 0) & (dist < WINDOW)
    bias = rel_bias[_t5_bucket_np(dist)].astype(F32)
    bias = jnp.where(valid[:, :, None], bias, NEG).transpose(2, 0, 1)
    bias = jnp.concatenate([bias[:n_chunks], bias[n_chunks:]], axis=1)
    kcol = ATTN_WIDTH // LANES
    vcol = kcol + 1

    def prev(b, n):
        return (b * nb + jnp.maximum(n - 1, 0))

    return pl.pallas_call(
        functools.partial(_swa_kernel, n_chunks=n_chunks),
        grid=(batch, nb),
        in_specs=[
            pl.BlockSpec(memory_space=pltpu.SMEM),
            pl.BlockSpec((qb, ATTN_WIDTH), lambda b, n: (b * nb + n, 0)),
            pl.BlockSpec((qb, LANES), lambda b, n: (prev(b, n), kcol)),
            pl.BlockSpec((qb, LANES), lambda b, n: (b * nb + n, kcol)),
            pl.BlockSpec((qb, LANES), lambda b, n: (prev(b, n), vcol)),
            pl.BlockSpec((qb, LANES), lambda b, n: (b * nb + n, vcol)),
            _const_spec((n_chunks, 2 * qb, 2 * qb)),
        ],
        out_specs=pl.BlockSpec((qb, ATTN_WIDTH), lambda b, n: (b * nb + n, 0)),
        out_shape=jax.ShapeDtypeStruct((batch * seq, ATTN_WIDTH), BF16),
        compiler_params=_cparams(("parallel", "parallel")),
        name="swa",
    )(sinks.astype(F32), qkv, qkv, qkv, qkv, qkv, bias)


def _mix_kernel(u_ref, halo_ref, a_ref, g0_ref, g1_ref, x_ref, dww_ref, dwb_ref, lng_ref, lnb_ref,
                wc_ref, wa_ref, wm_ref, o_ref, ext_ref, cv_ref, *, tiles_per_seq, row_chunk):
    tm = u_ref.shape[0]
    seq_start = pl.program_id(0) % tiles_per_seq == 0
    halo = halo_ref[...].astype(F32)
    ext_ref[0:CONV_HALO, :] = jnp.where(seq_start, 0.0, halo)
    ext_ref[CONV_HALO:, :] = u_ref[...].astype(F32)
    off = CONV_HALO - (CONV_WIDTH - 1)
    for r in range(tm // row_chunk):
        for c in range(CONV_CH // LANES):
            sl = slice(c * LANES, (c + 1) * LANES)
            acc = jnp.broadcast_to(dwb_ref[:, sl], (row_chunk, LANES))
            for j in range(CONV_WIDTH):
                lo = r * row_chunk + off + j
                acc = acc + dww_ref[j:j + 1, sl] * ext_ref[lo:lo + row_chunk, sl]
            cv_ref[r * row_chunk:(r + 1) * row_chunk, sl] = acc
    cv = cv_ref[...]
    mu = jnp.mean(cv, axis=-1, keepdims=True)
    cen = cv - mu
    var = jnp.mean(cen * cen, axis=-1, keepdims=True)
    ln = cen * lax.rsqrt(var + EPS) * lng_ref[...] + lnb_ref[...]
    act = (ln * jax.nn.sigmoid(ln)).astype(BF16)
    y_conv = jnp.dot(act, wc_ref[...], preferred_element_type=F32)
    y_attn = jnp.dot(a_ref[...], wa_ref[...], preferred_element_type=F32)
    merged = g0_ref[...].astype(F32) * y_conv + g1_ref[...].astype(F32) * y_attn
    o_ref[...] = x_ref[...] + jnp.dot(merged.astype(BF16), wm_ref[...], preferred_element_type=F32)


def _mix(u, attn_o, gates, x, dw_w, dw_b, ln_g, ln_b, w_conv_out, w_attn_out, w_mix_out, seq, tm=256):
    n, d = x.shape
    assert tm % CONV_HALO == 0 and seq % tm == 0
    hb = tm // CONV_HALO
    return pl.pallas_call(
        functools.partial(_mix_kernel, tiles_per_seq=seq // tm, row_chunk=128),
        grid=(n // tm,),
        in_specs=[
            pl.BlockSpec((tm, CONV_CH), lambda i: (i, 0)),
            pl.BlockSpec((CONV_HALO, CONV_CH), lambda i: (jnp.maximum(i * hb - 1, 0), 0)),
            pl.BlockSpec((tm, ATTN_WIDTH), lambda i: (i, 0)),
            pl.BlockSpec((tm, d), lambda i: (i, 0)),
            pl.BlockSpec((tm, d), lambda i: (i, 1)),
            pl.BlockSpec((tm, d), lambda i: (i, 0)),
            _const_spec((CONV_WIDTH, CONV_CH)),
            _const_spec((1, CONV_CH)),
            _const_spec((1, CONV_CH)),
            _const_spec((1, CONV_CH)),
            _const_spec((CONV_CH, d)),
            _const_spec((ATTN_WIDTH, d)),
            _const_spec((d, d)),
        ],
        out_specs=pl.BlockSpec((tm, d), lambda i: (i, 0)),
        out_shape=jax.ShapeDtypeStruct((n, d), F32),
        scratch_shapes=[pltpu.VMEM((tm + CONV_HALO, CONV_CH), F32), pltpu.VMEM((tm, CONV_CH), F32)],
        compiler_params=_cparams(("parallel",)),
        name="mix",
    )(u, u, attn_o, gates, gates, x, dw_w, dw_b.reshape(1, -1), ln_g.reshape(1, -1), ln_b.reshape(1, -1),
      w_conv_out, w_attn_out, w_mix_out)


def _xattn_route_kernel(h_ref, gx_ref, wq_ref, qgain_ref, bd_ref, k_ref, v_ref, wo_ref, gm_ref,
                        wrh_ref, wrl_ref, br_ref, tri_ref,
                        h2_ref, xp_ref, ri_ref, rf_ref, cnt_ref, run_ref):
    tm, d = h_ref.shape
    half = d // 2

    @pl.when(pl.program_id(0) == 0)
    def _():
        run_ref[...] = jnp.zeros_like(run_ref)

    h = h_ref[...]
    xn = (h * _rms_scale(h) * gx_ref[...]).astype(BF16)
    q = jnp.dot(xn, wq_ref[...], preferred_element_type=F32)
    heads = []
    for hd in range(X_HEADS):
        sl = slice(hd * X_HEAD_DIM, (hd + 1) * X_HEAD_DIM)
        qh = q[:, sl]
        qh = (qh * lax.rsqrt(_group_sumsq(qh, bd_ref[...]) * (1.0 / X_HEAD_DIM) + EPS) * qgain_ref[:, sl]).astype(BF16)
        s = lax.dot_general(qh, k_ref[:, sl], (((1,), (1,)), ((), ())), preferred_element_type=F32)
        p = jnp.exp(s - jnp.max(s, axis=-1, keepdims=True))
        den = jnp.sum(p, axis=-1, keepdims=True)
        heads.append((jnp.dot(p.astype(BF16), v_ref[:, sl], preferred_element_type=F32) / den).astype(BF16))
    o = jnp.concatenate(heads, axis=-1)
    h2 = h + jnp.dot(o, wo_ref[...], preferred_element_type=F32)
    h2_ref[...] = h2

    xn2 = h2 * _rms_scale(h2) * gm_ref[...]
    lo_bits = lax.bitcast_convert_type(xn2[:, :half].astype(BF16).astype(F32), jnp.uint32)
    hi_bits = lax.bitcast_convert_type(xn2[:, half:].astype(BF16).astype(F32), jnp.uint32)
    xp_ref[...] = (lo_bits >> 16) | hi_bits

    x_hi = xn2.astype(BF16)
    x_lo = (xn2 - x_hi.astype(F32)).astype(BF16)
    lg = (jnp.dot(x_hi, wrh_ref[...], preferred_element_type=F32)
          + jnp.dot(x_lo, wrh_ref[...], preferred_element_type=F32)
          + jnp.dot(x_hi, wrl_ref[...], preferred_element_type=F32) + br_ref[...])

    lane = lax.broadcasted_iota(jnp.int32, (tm, LANES), 1)
    lane_f = lane.astype(F32)
    big = jnp.float32(LANES)

    def first_argmax(vals, vmax):
        idx = jnp.min(jnp.where(vals == vmax, lane_f, big), axis=-1, keepdims=True)
        return idx.astype(jnp.int32)

    glog = jnp.where(lane < N_GROUPS, lg, -jnp.inf)
    gmax = jnp.max(glog, axis=-1, keepdims=True)
    grp = first_argmax(glog, gmax)
    pg_top = 1.0 / jnp.sum(jnp.exp(glog - gmax), axis=-1, keepdims=True)
    elane = lane - N_GROUPS
    emask = (elane >= 0) & (elane < N_EXPERTS) & ((elane >> 3) == grp)
    elog = jnp.where(emask, lg, -jnp.inf)
    m1 = jnp.max(elog, axis=-1, keepdims=True)
    i1 = first_argmax(elog, m1)
    elog2 = jnp.where(lane == i1, -jnp.inf, elog)
    m2 = jnp.max(elog2, axis=-1, keepdims=True)
    i2 = first_argmax(elog2, m2)
    den = jnp.sum(jnp.exp(elog - m1), axis=-1, keepdims=True)
    p1 = 1.0 / den
    p2 = jnp.exp(m2 - m1) / den
    psum = p1 + p2
    g1 = pg_top * (p1 / psum)
    g2 = pg_top * (p2 / psum)
    e1 = i1 - N_GROUPS
    e2 = i2 - N_GROUPS

    onehot = jnp.where((elane == e1) | (elane == e2), 1.0, 0.0)
    before = jnp.dot(tri_ref[...], onehot.astype(BF16), preferred_element_type=F32) + run_ref[0:1, :]
    r1 = jnp.sum(jnp.where(elane == e1, before, 0.0), axis=-1, keepdims=True).astype(jnp.int32)
    r2 = jnp.sum(jnp.where(elane == e2, before, 0.0), axis=-1, keepdims=True).astype(jnp.int32)
    run = run_ref[...] + jnp.sum(onehot, axis=0, keepdims=True)
    run_ref[...] = run
    cnt_ref[0] = run.astype(jnp.int32)

    ri_ref[...] = jnp.where(lane == 0, e1, jnp.where(lane == 1, e2, jnp.where(lane == 2, r1, jnp.where(lane == 3, r2, 0))))
    rf_ref[...] = jnp.where(lane == 0, g1, jnp.where(lane == 1, g2, 0.0))


def _xattn_route(h1, memkv, gx, w_xq, q_gain, w_xo, gm, wr_hi, wr_lo, b_r, seq, mem_len, tm=512):
    n, d = h1.shape
    tiles_per_seq = seq // tm
    nt = n // tm
    tri = jnp.asarray(np.tril(np.ones((tm, tm), np.float32), -1), dtype=BF16)
    bd = _block_diag_ones(X_HEAD_DIM)
    return pl.pallas_call(
        _xattn_route_kernel,
        grid=(nt,),
        in_specs=[
            pl.BlockSpec((tm, d), lambda i: (i, 0)),
            _const_spec((1, d)),
            _const_spec((d, X_WIDTH)),
            _const_spec((1, X_WIDTH)),
            _const_spec((LANES, LANES)),
            pl.BlockSpec((mem_len, X_WIDTH), lambda i: (i // tiles_per_seq, 0)),
            pl.BlockSpec((mem_len, X_WIDTH), lambda i: (i // tiles_per_seq, 1)),
            _const_spec((X_WIDTH, d)),
            _const_spec((1, d)),
            _const_spec((d, LANES)),
            _const_spec((d, LANES)),
            _const_spec((1, LANES)),
            _const_spec((tm, tm)),
        ],
        out_specs=[
            pl.BlockSpec((tm, d), lambda i: (i, 0)),
            pl.BlockSpec((tm, d // 2), lambda i: (i, 0)),
            pl.BlockSpec((tm, LANES), lambda i: (i, 0)),
            pl.BlockSpec((tm, LANES), lambda i: (i, 0)),
            pl.BlockSpec((1, 8, LANES), lambda i: (i, 0, 0)),
        ],
        out_shape=[
            jax.ShapeDtypeStruct((n, d), F32),
            jax.ShapeDtypeStruct((n, d // 2), jnp.uint32),
            jax.ShapeDtypeStruct((n, LANES), jnp.int32),
            jax.ShapeDtypeStruct((n, LANES), F32),
            jax.ShapeDtypeStruct((nt, 8, LANES), jnp.int32),
        ],
        scratch_shapes=[pltpu.VMEM((8, LANES), F32)],
        compiler_params=_cparams(("arbitrary",)),
        name="xattn_route",
    )(h1, gx.reshape(1, d), w_xq, q_gain, bd, memkv, memkv, w_xo, gm.reshape(1, d), wr_hi, wr_lo, b_r, tri)


def _dispatch_kernel(dest_ref, x_ref, xs_in_ref, xs_ref, sem):
    del xs_in_ref
    tm = x_ref.shape[0]

    def copy(t, k):
        return pltpu.make_async_copy(x_ref.at[pl.ds(t, 1)], xs_ref.at[pl.ds(dest_ref[0, 0, 2 * t + k], 1)], sem)

    def issue(t, carry):
        copy(t, 0).start()
        copy(t, 1).start()
        return carry

    lax.fori_loop(0, tm, issue, 0)

    def drain(t, carry):
        copy(t, 0).wait()
        copy(t, 1).wait()
        return carry

    lax.fori_loop(0, tm, drain, 0)


def _dispatch(xp, dest, rows, tm=512):
    n, w = xp.shape
    nt = n // tm
    xs0 = jnp.zeros((rows, w), xp.dtype)
    return pl.pallas_call(
        _dispatch_kernel,
        grid=(nt,),
        in_specs=[
            pl.BlockSpec((1, 1, 2 * tm), lambda i: (i, 0, 0), memory_space=pltpu.SMEM),
            pl.BlockSpec((tm, w), lambda i: (i, 0)),
            pl.BlockSpec(memory_space=pl.ANY),
        ],
        out_specs=pl.BlockSpec(memory_space=pl.ANY),
        out_shape=jax.ShapeDtypeStruct((rows, w), xp.dtype),
        scratch_shapes=[pltpu.SemaphoreType.DMA(())],
        input_output_aliases={2: 0},
        compiler_params=_cparams(("arbitrary",)),
        name="dispatch",
    )(dest.reshape(nt, 1, 2 * tm), xp, xs0)


def _experts_kernel(blk_e_ref, blk_x_ref, nused_ref, xs_ref, wgu_ref, wdn_ref, y_ref, wgu_bf, wdn_bf):
    b = pl.program_id(0)
    half = xs_ref.shape[1]
    prev_e = blk_e_ref[jnp.maximum(b - 1, 0)]
    new_expert = jnp.logical_or(b == 0, blk_e_ref[b] != prev_e)

    @pl.when(new_expert)
    def _():
        wgu_bf[...] = wgu_ref[0].astype(BF16)
        wdn_bf[...] = wdn_ref[0].astype(BF16)

    @pl.when(b < nused_ref[0])
    def _():
        w = xs_ref[...]
        x_lo = lax.bitcast_convert_type(w << 16, F32).astype(BF16)
        x_hi = lax.bitcast_convert_type(w & jnp.uint32(0xFFFF0000), F32).astype(BF16)
        gu = (jnp.dot(x_lo, wgu_bf[0:half, :], preferred_element_type=F32)
              + jnp.dot(x_hi, wgu_bf[half:, :], preferred_element_type=F32))
        g = gu[:, :D_EXPERT]
        u = gu[:, D_EXPERT:]
        act = (g * jax.nn.sigmoid(g) * u).astype(BF16)
        y_ref[...] = jnp.dot(act, wdn_bf[...], preferred_element_type=F32)

    @pl.when(b >= nused_ref[0])
    def _():
        y_ref[...] = jnp.zeros_like(y_ref)


def _experts(xs, w_gu, w_dn, blk_e, blk_x, nused):
    rows, half = xs.shape
    d = 2 * half
    n_blocks = rows // DISPATCH_BLOCK
    grid_spec = pltpu.PrefetchScalarGridSpec(
        num_scalar_prefetch=3,
        grid=(n_blocks,),
        in_specs=[
            pl.BlockSpec((DISPATCH_BLOCK, half), lambda b, be, bx, nu: (bx[b], 0)),
            pl.BlockSpec((1, d, 2 * D_EXPERT), lambda b, be, bx, nu: (be[b], 0, 0)),
            pl.BlockSpec((1, D_EXPERT, d), lambda b, be, bx, nu: (be[b], 0, 0)),
        ],
        out_specs=pl.BlockSpec((DISPATCH_BLOCK, d), lambda b, be, bx, nu: (b, 0)),
        scratch_shapes=[pltpu.VMEM((d, 2 * D_EXPERT), BF16), pltpu.VMEM((D_EXPERT, d), BF16)],
    )
    return pl.pallas_call(
        _experts_kernel,
        grid_spec=grid_spec,
        out_shape=jax.ShapeDtypeStruct((rows, d), F32),
        compiler_params=_cparams(("arbitrary",)),
        name="experts",
    )(blk_e, blk_x, nused, xs, w_gu, w_dn)


def _combine_kernel(dest_ref, h_ref, rf_ref, y_ref, o_ref, ya_ref, yb_ref, sem):
    tm = h_ref.shape[0]

    def copy(t, k, buf):
        return pltpu.make_async_copy(y_ref.at[pl.ds(dest_ref[0, 0, 2 * t + k], 1)], buf.at[pl.ds(t, 1)], sem)

    def issue(t, carry):
        copy(t, 0, ya_ref).start()
        copy(t, 1, yb_ref).start()
        return carry

    lax.fori_loop(0, tm, issue, 0)

    def drain(t, carry):
        copy(t, 0, ya_ref).wait()
        copy(t, 1, yb_ref).wait()
        return carry

    lax.fori_loop(0, tm, drain, 0)
    g1 = rf_ref[:, 0:1]
    g2 = rf_ref[:, 1:2]
    o_ref[...] = h_ref[...] + (ya_ref[...] * g1 + yb_ref[...] * g2)


def _combine(h2, rf, dest, ys, tm=256):
    n, d = h2.shape
    nt = n // tm
    return pl.pallas_call(
        _combine_kernel,
        grid=(nt,),
        in_specs=[
            pl.BlockSpec((1, 1, 2 * tm), lambda i: (i, 0, 0), memory_space=pltpu.SMEM),
            pl.BlockSpec((tm, d), lambda i: (i, 0)),
            pl.BlockSpec((tm, LANES), lambda i: (i, 0)),
            pl.BlockSpec(memory_space=pl.ANY),
        ],
        out_specs=pl.BlockSpec((tm, d), lambda i: (i, 0)),
        out_shape=jax.ShapeDtypeStruct((n, d), F32),
        scratch_shapes=[pltpu.VMEM((tm, d), F32), pltpu.VMEM((tm, d), F32), pltpu.SemaphoreType.DMA(())],
        compiler_params=_cparams(("arbitrary",)),
        name="combine",
    )(dest.reshape(nt, 1, 2 * tm), h2, rf, ys)


def kernel(x, mem, rel_bias, norm_mix_g, w_in, conv_dw_w, conv_dw_b, conv_ln_g, conv_ln_b, w_conv_out, q_norm_g, k_norm_g, attn_sinks, w_attn_out, w_mix_out, norm_x_g, norm_mem_g, w_xq, w_xkv, xq_norm_g, xk_norm_g, w_xo, norm_moe_g, w_router_group, b_router_group, w_router_expert, b_router_expert, w_expert_gu, w_expert_down):
    batch, seq, d = x.shape
    mem_len = mem.shape[1]
    n = batch * seq
    n_chunks = N_Q_HEADS // 2
    h = x.reshape(n, d)
    for l in range(norm_mix_g.shape[0]):
        w = w_in[l].astype(BF16)
        c0, c1, c2 = 2 * CONV_CH, 2 * CONV_CH + ATTN_WIDTH, 2 * CONV_CH + ATTN_WIDTH + 2 * KV_WIDTH
        u = _norm_proj(h, norm_mix_g[l], (w[:, :CONV_CH], w[:, CONV_CH:c0]), mode="glu", tm=512, tn=512)
        gates = _norm_proj(h, norm_mix_g[l], (w[:, c2:],), mode="sigmoid", tm=512, tn=512)
        head_order = np.stack([np.arange(n_chunks), np.arange(n_chunks) + n_chunks], axis=1).reshape(-1)
        q_cols = (head_order[:, None] * HEAD_DIM + np.arange(HEAD_DIM)[None, :]).reshape(-1)
        w_qkv = jnp.concatenate([w[:, c0:c1][:, q_cols], w[:, c1:c2]], axis=1)
        q_gain = jnp.tile(q_norm_g[l].astype(F32) * HEAD_DIM ** -0.5, N_Q_HEADS)
        gain = jnp.concatenate([q_gain, jnp.tile(k_norm_g[l].astype(F32), N_KV_HEADS), jnp.ones((KV_WIDTH,), F32)])
        flag = jnp.concatenate([jnp.ones((ATTN_WIDTH + KV_WIDTH,), F32), jnp.zeros((KV_WIDTH,), F32)])
        qkv = _norm_proj(h, norm_mix_g[l], (w_qkv,), mode="headnorm", tm=512, tn=w_qkv.shape[1],
                         extras=(gain.reshape(1, -1), flag.reshape(1, -1), _block_diag_ones(HEAD_DIM)),
                         head_dim=HEAD_DIM)
        attn_o = _swa(qkv, rel_bias, attn_sinks[l], batch, seq)
        w_ao = w_attn_out[l][q_cols].astype(BF16)
        h = _mix(u, attn_o, gates, h, conv_dw_w[l].reshape(CONV_WIDTH, CONV_CH), conv_dw_b[l], conv_ln_g[l],
                 conv_ln_b[l], w_conv_out[l].astype(BF16), w_ao, w_mix_out[l].astype(BF16), seq)

        kgain = jnp.concatenate([jnp.tile(xk_norm_g[l].astype(F32), X_HEADS), jnp.ones((X_WIDTH,), F32)])
        kflag = jnp.concatenate([jnp.ones((X_WIDTH,), F32), jnp.zeros((X_WIDTH,), F32)])
        memkv = _norm_proj(mem.reshape(batch * mem_len, d), norm_mem_g[l], (w_xkv[l].astype(BF16),), mode="headnorm",
                           tm=512, tn=2 * X_WIDTH,
                           extras=(kgain.reshape(1, -1), kflag.reshape(1, -1), _block_diag_ones(X_HEAD_DIM)),
                           head_dim=X_HEAD_DIM)
        xq_gain = jnp.tile(xq_norm_g[l].astype(F32) * X_HEAD_DIM ** -0.5, X_HEADS).reshape(1, -1)
        w_r = jnp.concatenate([w_router_group[l], w_router_expert[l]], axis=1).astype(F32)
        w_r = jnp.pad(w_r, ((0, 0), (0, LANES - w_r.shape[1])))
        wr_hi = w_r.astype(BF16)
        wr_lo = (w_r - wr_hi.astype(F32)).astype(BF16)
        b_r = jnp.concatenate([b_router_group[l], b_router_expert[l]]).astype(F32)
        b_r = jnp.pad(b_r, (0, LANES - b_r.shape[0])).reshape(1, LANES)
        h2, xp, ri, rf, cnt = _xattn_route(h, memkv, norm_x_g[l], w_xq[l].astype(BF16), xq_gain, w_xo[l].astype(BF16),
                                           norm_moe_g[l], wr_hi, wr_lo, b_r, seq, mem_len)

        counts = cnt[-1, 0, N_GROUPS:N_GROUPS + N_EXPERTS]
        padded = (counts + DISPATCH_BLOCK - 1) // DISPATCH_BLOCK * DISPATCH_BLOCK
        pad_end = jnp.cumsum(padded)
        pad_start = pad_end - padded
        n_blocks = -(-(2 * n) // DISPATCH_BLOCK) + N_EXPERTS
        rows = n_blocks * DISPATCH_BLOCK
        nused = (pad_end[-1] // DISPATCH_BLOCK).astype(jnp.int32)
        blk = jnp.minimum(jnp.arange(n_blocks, dtype=jnp.int32), nused - 1)
        blk_e = jnp.minimum(jnp.searchsorted(pad_end, blk * DISPATCH_BLOCK, side='right'), N_EXPERTS - 1).astype(jnp.int32)
        dest = (pad_start[ri[:, 0:2]] + ri[:, 2:4]).astype(jnp.int32)

        xs = _dispatch(xp, dest, rows)
        ys = _experts(xs, w_expert_gu[l], w_expert_down[l], blk_e, blk, nused.reshape(1))
        h = _combine(h2, rf, dest, ys)
    return h.reshape(batch, seq, d)
```

```python
import functools
import math

import jax
import jax.numpy as jnp
import numpy as np
from jax import lax
from jax.experimental import pallas as pl
from jax.experimental.pallas import tpu as pltpu

EPS = 1e-6
NEG = -1e30

CONV_CH = 1024
CONV_WIDTH = 31
N_Q_HEADS = 16
N_KV_HEADS = 2
HEAD_DIM = 64
ATTN_WIDTH = N_Q_HEADS * HEAD_DIM
KV_WIDTH = N_KV_HEADS * HEAD_DIM
WINDOW = 128
NUM_BUCKETS = 32
MAX_DISTANCE = 128
X_HEADS = 4
X_HEAD_DIM = 128
X_WIDTH = X_HEADS * X_HEAD_DIM
N_GROUPS = 4
EXPERTS_PER_GROUP = 8
N_EXPERTS = N_GROUPS * EXPERTS_PER_GROUP
D_EXPERT = 512
DISPATCH_BLOCK = 256

LANES = 128
CONV_HALO = 32
VMEM_LIMIT = 56 * 1024 * 1024

BF16 = jnp.bfloat16
F32 = jnp.float32


def _cparams(sem):
    return pltpu.CompilerParams(dimension_semantics=sem, vmem_limit_bytes=VMEM_LIMIT)


def _const_spec(shape):
    nd = len(shape)
    return pl.BlockSpec(shape, lambda *_: (0,) * nd, pipeline_mode=pl.Buffered(1))


def _rms_scale(x):
    return lax.rsqrt(jnp.mean(x * x, axis=-1, keepdims=True) + EPS)


def _group_sumsq(y, bd):
    sq = y * y
    hi = sq.astype(BF16)
    lo = (sq - hi.astype(F32)).astype(BF16)
    return (jnp.dot(hi, bd, preferred_element_type=F32) + jnp.dot(lo, bd, preferred_element_type=F32))


def _head_norm_store(y, gain_ref, flag_ref, bd, o_ref, head_dim):
    for c in range(y.shape[1] // LANES):
        sl = slice(c * LANES, (c + 1) * LANES)
        yc = y[:, sl]
        ss = _group_sumsq(yc, bd)
        normed = yc * lax.rsqrt(ss * (1.0 / head_dim) + EPS) * gain_ref[:, sl]
        o_ref[:, sl] = jnp.where(flag_ref[:, sl] > 0.0, normed, yc).astype(o_ref.dtype)


def _in_proj_kernel(x_ref, g_ref, w_ref, gain_ref, flag_ref, bd_ref, u_ref, qkv_ref, gate_ref, xn_ref, *, tn):
    x = x_ref[...]
    xn_ref[...] = (x * _rms_scale(x) * g_ref[...]).astype(BF16)

    def proj(lo, width):
        return jnp.dot(xn_ref[...], w_ref[:, lo:lo + width], preferred_element_type=F32)

    n_u = u_ref.shape[1]
    for c in range(n_u // tn):
        a = proj(c * tn, tn)
        b = proj(n_u + c * tn, tn)
        u_ref[:, c * tn:(c + 1) * tn] = (a * jax.nn.sigmoid(b)).astype(u_ref.dtype)
    n_qkv = qkv_ref.shape[1]
    _head_norm_store(proj(2 * n_u, n_qkv), gain_ref, flag_ref, bd_ref[...], qkv_ref, HEAD_DIM)
    g0 = 2 * n_u + n_qkv
    for c in range(gate_ref.shape[1] // tn):
        gate_ref[:, c * tn:(c + 1) * tn] = jax.nn.sigmoid(proj(g0 + c * tn, tn)).astype(gate_ref.dtype)


def _in_proj(x, g, w, gain, flag, tm=256, tn=512):
    m, d = x.shape
    n_qkv = ATTN_WIDTH + 2 * KV_WIDTH
    n_gate = w.shape[1] - 2 * CONV_CH - n_qkv
    row = lambda i: (i, 0)
    return pl.pallas_call(
        functools.partial(_in_proj_kernel, tn=tn),
        grid=(m // tm,),
        in_specs=[pl.BlockSpec((tm, d), row), _const_spec((1, d)), _const_spec(w.shape), _const_spec((1, n_qkv)),
                  _const_spec((1, n_qkv)), _const_spec((LANES, LANES))],
        out_specs=[pl.BlockSpec((tm, CONV_CH), row), pl.BlockSpec((tm, n_qkv), row), pl.BlockSpec((tm, n_gate), row)],
        out_shape=[jax.ShapeDtypeStruct((m, CONV_CH), BF16), jax.ShapeDtypeStruct((m, n_qkv), BF16),
                   jax.ShapeDtypeStruct((m, n_gate), BF16)],
        scratch_shapes=[pltpu.VMEM((tm, d), BF16)],
        compiler_params=_cparams(("parallel",)),
        name="in_proj",
    )(x, g.reshape(1, d), w, gain.reshape(1, -1), flag.reshape(1, -1), _block_diag_ones(HEAD_DIM))


def _mem_proj_kernel(x_ref, g_ref, w_ref, gain_ref, flag_ref, bd_ref, o_ref):
    x = x_ref[...]
    xn = (x * _rms_scale(x) * g_ref[...]).astype(BF16)
    y = jnp.dot(xn, w_ref[...], preferred_element_type=F32)
    _head_norm_store(y, gain_ref, flag_ref, bd_ref[...], o_ref, X_HEAD_DIM)


def _mem_proj(x, g, w, gain, flag, tm=256):
    m, d = x.shape
    n = w.shape[1]
    return pl.pallas_call(
        _mem_proj_kernel,
        grid=(m // tm,),
        in_specs=[pl.BlockSpec((tm, d), lambda i: (i, 0)), _const_spec((1, d)), _const_spec(w.shape), _const_spec((1, n)),
                  _const_spec((1, n)), _const_spec((LANES, LANES))],
        out_specs=pl.BlockSpec((tm, n), lambda i: (i, 0)),
        out_shape=jax.ShapeDtypeStruct((m, n), BF16),
        compiler_params=_cparams(("parallel",)),
        name="mem_proj",
    )(x, g.reshape(1, d), w, gain.reshape(1, -1), flag.reshape(1, -1), _block_diag_ones(X_HEAD_DIM))


def _block_diag_ones(group):
    r = np.arange(LANES)
    return jnp.asarray((r[:, None] // group) == (r[None, :] // group), dtype=BF16)


def _swa_kernel(sink_ref, q_ref, kp_ref, kc_ref, vp_ref, vc_ref, bias_ref, o_ref, *, n_chunks):
    first = pl.program_id(1) == 0
    qb = q_ref.shape[0]
    kcat = jnp.concatenate([kp_ref[...], kc_ref[...]], axis=0)
    vcat = jnp.concatenate([vp_ref[...], vc_ref[...]], axis=0)
    lane = lax.broadcasted_iota(jnp.int32, (qb, LANES), 1)
    row = lax.broadcasted_iota(jnp.int32, (2 * qb, 1), 0)
    col = lax.broadcasted_iota(jnp.int32, (2 * qb, 2 * qb), 1)
    no_prev = jnp.where(jnp.logical_and(first, col < qb), NEG, 0.0)
    for c in range(n_chunks):
        q2 = q_ref[:, c * LANES:(c + 1) * LANES]
        zero = jnp.zeros_like(q2)
        qs = jnp.concatenate([jnp.where(lane < HEAD_DIM, q2, zero), jnp.where(lane >= HEAD_DIM, q2, zero)], axis=0)
        s = lax.dot_general(qs, kcat, (((1,), (1,)), ((), ())), preferred_element_type=F32)
        s = s + bias_ref[c] + no_prev
        sink = jnp.where(row < qb, sink_ref[c], sink_ref[n_chunks + c])
        m = jnp.maximum(jnp.max(s, axis=-1, keepdims=True), sink)
        p = jnp.exp(s - m)
        den = jnp.sum(p, axis=-1, keepdims=True) + jnp.exp(sink - m)
        o2 = jnp.dot(p.astype(BF16), vcat, preferred_element_type=F32) / den
        o = jnp.where(lane < HEAD_DIM, o2[:qb], o2[qb:])
        o_ref[:, c * LANES:(c + 1) * LANES] = o.astype(o_ref.dtype)


def _t5_bucket_np(dist):
    n = np.maximum(dist, 0)
    max_exact = NUM_BUCKETS // 2
    large = max_exact + (np.log(np.maximum(n, 1).astype(np.float32) / max_exact)
                         / math.log(MAX_DISTANCE / max_exact) * (NUM_BUCKETS - max_exact)).astype(np.int32)
    large = np.minimum(large, NUM_BUCKETS - 1)
    return np.where(n < max_exact, n, large)


def _swa(qkv, rel_bias, sinks, batch, seq):
    qb = WINDOW
    nb = seq // qb
    n_chunks = N_Q_HEADS // 2
    qi = np.arange(qb)[:, None]
    kj = np.arange(2 * qb)[None, :]
    dist = qi + qb - kj
    valid = (dist >= 0) & (dist < WINDOW)
    bias = rel_bias[_t5_bucket_np(dist)].astype(F32)
    bias = jnp.where(valid[:, :, None], bias, NEG).transpose(2, 0, 1)
    bias = jnp.concatenate([bias[:n_chunks], bias[n_chunks:]], axis=1)
    kcol = ATTN_WIDTH // LANES
    vcol = kcol + 1

    def prev(b, n):
        return (b * nb + jnp.maximum(n - 1, 0))

    return pl.pallas_call(
        functools.partial(_swa_kernel, n_chunks=n_chunks),
        grid=(batch, nb),
        in_specs=[
            pl.BlockSpec(memory_space=pltpu.SMEM),
            pl.BlockSpec((qb, ATTN_WIDTH), lambda b, n: (b * nb + n, 0)),
            pl.BlockSpec((qb, LANES), lambda b, n: (prev(b, n), kcol)),
            pl.BlockSpec((qb, LANES), lambda b, n: (b * nb + n, kcol)),
            pl.BlockSpec((qb, LANES), lambda b, n: (prev(b, n), vcol)),
            pl.BlockSpec((qb, LANES), lambda b, n: (b * nb + n, vcol)),
            _const_spec((n_chunks, 2 * qb, 2 * qb)),
        ],
        out_specs=pl.BlockSpec((qb, ATTN_WIDTH), lambda b, n: (b * nb + n, 0)),
        out_shape=jax.ShapeDtypeStruct((batch * seq, ATTN_WIDTH), BF16),
        compiler_params=_cparams(("parallel", "parallel")),
        name="swa",
    )(sinks.astype(F32), qkv, qkv, qkv, qkv, qkv, bias)


def _mix_kernel(u_ref, halo_ref, a_ref, g0_ref, g1_ref, x_ref, dww_ref, dwb_ref, lng_ref, lnb_ref,
                wc_ref, wa_ref, wm_ref, o_ref, ext_ref, cv_ref, *, tiles_per_seq, row_chunk):
    tm = u_ref.shape[0]
    seq_start = pl.program_id(0) % tiles_per_seq == 0
    halo = halo_ref[...].astype(F32)
    ext_ref[0:CONV_HALO, :] = jnp.where(seq_start, 0.0, halo)
    ext_ref[CONV_HALO:, :] = u_ref[...].astype(F32)
    off = CONV_HALO - (CONV_WIDTH - 1)
    for r in range(tm // row_chunk):
        for c in range(CONV_CH // LANES):
            sl = slice(c * LANES, (c + 1) * LANES)
            acc = jnp.broadcast_to(dwb_ref[:, sl], (row_chunk, LANES))
            for j in range(CONV_WIDTH):
                lo = r * row_chunk + off + j
                acc = acc + dww_ref[j:j + 1, sl] * ext_ref[lo:lo + row_chunk, sl]
            cv_ref[r * row_chunk:(r + 1) * row_chunk, sl] = acc
    cv = cv_ref[...]
    mu = jnp.mean(cv, axis=-1, keepdims=True)
    cen = cv - mu
    var = jnp.mean(cen * cen, axis=-1, keepdims=True)
    ln = cen * lax.rsqrt(var + EPS) * lng_ref[...] + lnb_ref[...]
    act = (ln * jax.nn.sigmoid(ln)).astype(BF16)
    y_conv = jnp.dot(act, wc_ref[...], preferred_element_type=F32)
    y_attn = jnp.dot(a_ref[...], wa_ref[...], preferred_element_type=F32)
    merged = g0_ref[...].astype(F32) * y_conv + g1_ref[...].astype(F32) * y_attn
    o_ref[...] = x_ref[...] + jnp.dot(merged.astype(BF16), wm_ref[...], preferred_element_type=F32)


def _mix(u, attn_o, gates, x, dw_w, dw_b, ln_g, ln_b, w_conv_out, w_attn_out, w_mix_out, seq, tm=256):
    n, d = x.shape
    assert tm % CONV_HALO == 0 and seq % tm == 0
    hb = tm // CONV_HALO
    return pl.pallas_call(
        functools.partial(_mix_kernel, tiles_per_seq=seq // tm, row_chunk=128),
        grid=(n // tm,),
        in_specs=[
            pl.BlockSpec((tm, CONV_CH), lambda i: (i, 0)),
            pl.BlockSpec((CONV_HALO, CONV_CH), lambda i: (jnp.maximum(i * hb - 1, 0), 0)),
            pl.BlockSpec((tm, ATTN_WIDTH), lambda i: (i, 0)),
            pl.BlockSpec((tm, d), lambda i: (i, 0)),
            pl.BlockSpec((tm, d), lambda i: (i, 1)),
            pl.BlockSpec((tm, d), lambda i: (i, 0)),
            _const_spec((CONV_WIDTH, CONV_CH)),
            _const_spec((1, CONV_CH)),
            _const_spec((1, CONV_CH)),
            _const_spec((1, CONV_CH)),
            _const_spec((CONV_CH, d)),
            _const_spec((ATTN_WIDTH, d)),
            _const_spec((d, d)),
        ],
        out_specs=pl.BlockSpec((tm, d), lambda i: (i, 0)),
        out_shape=jax.ShapeDtypeStruct((n, d), F32),
        scratch_shapes=[pltpu.VMEM((tm + CONV_HALO, CONV_CH), F32), pltpu.VMEM((tm, CONV_CH), F32)],
        compiler_params=_cparams(("parallel",)),
        name="mix",
    )(u, u, attn_o, gates, gates, x, dw_w, dw_b.reshape(1, -1), ln_g.reshape(1, -1), ln_b.reshape(1, -1),
      w_conv_out, w_attn_out, w_mix_out)


def _xattn_route_kernel(h_ref, gx_ref, wq_ref, qgain_ref, bd_ref, k_ref, v_ref, wo_ref, gm_ref,
                        wrh_ref, wrl_ref, br_ref, tri_ref,
                        h2_ref, xp_ref, ri_ref, rf_ref, cnt_ref, run_ref):
    tm, d = h_ref.shape
    half = d // 2

    @pl.when(pl.program_id(0) == 0)
    def _():
        run_ref[...] = jnp.zeros_like(run_ref)

    h = h_ref[...]
    xn = (h * _rms_scale(h) * gx_ref[...]).astype(BF16)
    q = jnp.dot(xn, wq_ref[...], preferred_element_type=F32)
    heads = []
    for hd in range(X_HEADS):
        sl = slice(hd * X_HEAD_DIM, (hd + 1) * X_HEAD_DIM)
        qh = q[:, sl]
        qh = (qh * lax.rsqrt(_group_sumsq(qh, bd_ref[...]) * (1.0 / X_HEAD_DIM) + EPS) * qgain_ref[:, sl]).astype(BF16)
        s = lax.dot_general(qh, k_ref[:, sl], (((1,), (1,)), ((), ())), preferred_element_type=F32)
        p = jnp.exp(s - jnp.max(s, axis=-1, keepdims=True))
        den = jnp.sum(p, axis=-1, keepdims=True)
        heads.append((jnp.dot(p.astype(BF16), v_ref[:, sl], preferred_element_type=F32) / den).astype(BF16))
    o = jnp.concatenate(heads, axis=-1)
    h2 = h + jnp.dot(o, wo_ref[...], preferred_element_type=F32)
    h2_ref[...] = h2

    xn2 = h2 * _rms_scale(h2) * gm_ref[...]
    lo_bits = lax.bitcast_convert_type(xn2[:, :half].astype(BF16).astype(F32), jnp.uint32)
    hi_bits = lax.bitcast_convert_type(xn2[:, half:].astype(BF16).astype(F32), jnp.uint32)
    xp_ref[...] = (lo_bits >> 16) | hi_bits

    x_hi = xn2.astype(BF16)
    x_lo = (xn2 - x_hi.astype(F32)).astype(BF16)
    lg = (jnp.dot(x_hi, wrh_ref[...], preferred_element_type=F32)
          + jnp.dot(x_lo, wrh_ref[...], preferred_element_type=F32)
          + jnp.dot(x_hi, wrl_ref[...], preferred_element_type=F32) + br_ref[...])

    lane = lax.broadcasted_iota(jnp.int32, (tm, LANES), 1)
    lane_f = lane.astype(F32)
    big = jnp.float32(LANES)

    def first_argmax(vals, vmax):
        idx = jnp.min(jnp.where(vals == vmax, lane_f, big), axis=-1, keepdims=True)
        return idx.astype(jnp.int32)

    glog = jnp.where(lane < N_GROUPS, lg, -jnp.inf)
    gmax = jnp.max(glog, axis=-1, keepdims=True)
    grp = first_argmax(glog, gmax)
    pg_top = 1.0 / jnp.sum(jnp.exp(glog - gmax), axis=-1, keepdims=True)
    elane = lane - N_GROUPS
    emask = (elane >= 0) & (elane < N_EXPERTS) & ((elane >> 3) == grp)
    elog = jnp.where(emask, lg, -jnp.inf)
    m1 = jnp.max(elog, axis=-1, keepdims=True)
    i1 = first_argmax(elog, m1)
    elog2 = jnp.where(lane == i1, -jnp.inf, elog)
    m2 = jnp.max(elog2, axis=-1, keepdims=True)
    i2 = first_argmax(elog2, m2)
    den = jnp.sum(jnp.exp(elog - m1), axis=-1, keepdims=True)
    p1 = 1.0 / den
    p2 = jnp.exp(m2 - m1) / den
    psum = p1 + p2
    g1 = pg_top * (p1 / psum)
    g2 = pg_top * (p2 / psum)
    e1 = i1 - N_GROUPS
    e2 = i2 - N_GROUPS

    onehot = jnp.where((elane == e1) | (elane == e2), 1.0, 0.0)
    before = jnp.dot(tri_ref[...], onehot.astype(BF16), preferred_element_type=F32) + run_ref[0:1, :]
    r1 = jnp.sum(jnp.where(elane == e1, before, 0.0), axis=-1, keepdims=True).astype(jnp.int32)
    r2 = jnp.sum(jnp.where(elane == e2, before, 0.0), axis=-1, keepdims=True).astype(jnp.int32)
    run = run_ref[...] + jnp.sum(onehot, axis=0, keepdims=True)
    run_ref[...] = run
    cnt_ref[0] = run.astype(jnp.int32)

    ri_ref[...] = jnp.where(lane == 0, e1, jnp.where(lane == 1, e2, jnp.where(lane == 2, r1, jnp.where(lane == 3, r2, 0))))
    rf_ref[...] = jnp.where(lane == 0, g1, jnp.where(lane == 1, g2, 0.0))


def _xattn_route(h1, memkv, gx, w_xq, q_gain, w_xo, gm, wr_hi, wr_lo, b_r, seq, mem_len, tm=512):
    n, d = h1.shape
    tiles_per_seq = seq // tm
    nt = n // tm
    tri = jnp.asarray(np.tril(np.ones((tm, tm), np.float32), -1), dtype=BF16)
    bd = _block_diag_ones(X_HEAD_DIM)
    return pl.pallas_call(
        _xattn_route_kernel,
        grid=(nt,),
        in_specs=[
            pl.BlockSpec((tm, d), lambda i: (i, 0)),
            _const_spec((1, d)),
            _const_spec((d, X_WIDTH)),
            _const_spec((1, X_WIDTH)),
            _const_spec((LANES, LANES)),
            pl.BlockSpec((mem_len, X_WIDTH), lambda i: (i // tiles_per_seq, 0)),
            pl.BlockSpec((mem_len, X_WIDTH), lambda i: (i // tiles_per_seq, 1)),
            _const_spec((X_WIDTH, d)),
            _const_spec((1, d)),
            _const_spec((d, LANES)),
            _const_spec((d, LANES)),
            _const_spec((1, LANES)),
            _const_spec((tm, tm)),
        ],
        out_specs=[
            pl.BlockSpec((tm, d), lambda i: (i, 0)),
            pl.BlockSpec((tm, d // 2), lambda i: (i, 0)),
            pl.BlockSpec((tm, LANES), lambda i: (i, 0)),
            pl.BlockSpec((tm, LANES), lambda i: (i, 0)),
            pl.BlockSpec((1, 8, LANES), lambda i: (i, 0, 0)),
        ],
        out_shape=[
            jax.ShapeDtypeStruct((n, d), F32),
            jax.ShapeDtypeStruct((n, d // 2), jnp.uint32),
            jax.ShapeDtypeStruct((n, LANES), jnp.int32),
            jax.ShapeDtypeStruct((n, LANES), F32),
            jax.ShapeDtypeStruct((nt, 8, LANES), jnp.int32),
        ],
        scratch_shapes=[pltpu.VMEM((8, LANES), F32)],
        compiler_params=_cparams(("arbitrary",)),
        name="xattn_route",
    )(h1, gx.reshape(1, d), w_xq, q_gain, bd, memkv, memkv, w_xo, gm.reshape(1, d), wr_hi, wr_lo, b_r, tri)


def _dispatch_kernel(dest_ref, x_ref, xs_in_ref, xs_ref, sem):
    del xs_in_ref
    tm = x_ref.shape[0]

    def copy(t, k):
        return pltpu.make_async_copy(x_ref.at[pl.ds(t, 1)], xs_ref.at[pl.ds(dest_ref[0, 0, 2 * t + k], 1)], sem)

    def issue(t, carry):
        copy(t, 0).start()
        copy(t, 1).start()
        return carry

    lax.fori_loop(0, tm, issue, 0)

    def drain(t, carry):
        copy(t, 0).wait()
        copy(t, 1).wait()
        return carry

    lax.fori_loop(0, tm, drain, 0)


def _dispatch(xp, dest, rows, tm=512):
    n, w = xp.shape
    nt = n // tm
    xs0 = jnp.zeros((rows, w), xp.dtype)
    return pl.pallas_call(
        _dispatch_kernel,
        grid=(nt,),
        in_specs=[
            pl.BlockSpec((1, 1, 2 * tm), lambda i: (i, 0, 0), memory_space=pltpu.SMEM),
            pl.BlockSpec((tm, w), lambda i: (i, 0)),
            pl.BlockSpec(memory_space=pl.ANY),
        ],
        out_specs=pl.BlockSpec(memory_space=pl.ANY),
        out_shape=jax.ShapeDtypeStruct((rows, w), xp.dtype),
        scratch_shapes=[pltpu.SemaphoreType.DMA(())],
        input_output_aliases={2: 0},
        compiler_params=_cparams(("arbitrary",)),
        name="dispatch",
    )(dest.reshape(nt, 1, 2 * tm), xp, xs0)


def _experts_kernel(blk_e_ref, blk_x_ref, nused_ref, xs_ref, wgu_ref, wdn_ref, y_ref, wgu_bf, wdn_bf):
    b = pl.program_id(0)
    half = xs_ref.shape[1]
    prev_e = blk_e_ref[jnp.maximum(b - 1, 0)]
    new_expert = jnp.logical_or(b == 0, blk_e_ref[b] != prev_e)

    @pl.when(new_expert)
    def _():
        wgu_bf[...] = wgu_ref[0].astype(BF16)
        wdn_bf[...] = wdn_ref[0].astype(BF16)

    @pl.when(b < nused_ref[0])
    def _():
        w = xs_ref[...]
        x_lo = lax.bitcast_convert_type(w << 16, F32).astype(BF16)
        x_hi = lax.bitcast_convert_type(w & jnp.uint32(0xFFFF0000), F32).astype(BF16)
        gu = (jnp.dot(x_lo, wgu_bf[0:half, :], preferred_element_type=F32)
              + jnp.dot(x_hi, wgu_bf[half:, :], preferred_element_type=F32))
        g = gu[:, :D_EXPERT]
        u = gu[:, D_EXPERT:]
        act = (g * jax.nn.sigmoid(g) * u).astype(BF16)
        y_ref[...] = jnp.dot(act, wdn_bf[...], preferred_element_type=F32)

    @pl.when(b >= nused_ref[0])
    def _():
        y_ref[...] = jnp.zeros_like(y_ref)


def _experts(xs, w_gu, w_dn, blk_e, blk_x, nused):
    rows, half = xs.shape
    d = 2 * half
    n_blocks = rows // DISPATCH_BLOCK
    grid_spec = pltpu.PrefetchScalarGridSpec(
        num_scalar_prefetch=3,
        grid=(n_blocks,),
        in_specs=[
            pl.BlockSpec((DISPATCH_BLOCK, half), lambda b, be, bx, nu: (bx[b], 0)),
            pl.BlockSpec((1, d, 2 * D_EXPERT), lambda b, be, bx, nu: (be[b], 0, 0)),
            pl.BlockSpec((1, D_EXPERT, d), lambda b, be, bx, nu: (be[b], 0, 0)),
        ],
        out_specs=pl.BlockSpec((DISPATCH_BLOCK, d), lambda b, be, bx, nu: (b, 0)),
        scratch_shapes=[pltpu.VMEM((d, 2 * D_EXPERT), BF16), pltpu.VMEM((D_EXPERT, d), BF16)],
    )
    return pl.pallas_call(
        _experts_kernel,
        grid_spec=grid_spec,
        out_shape=jax.ShapeDtypeStruct((rows, d), F32),
        compiler_params=_cparams(("arbitrary",)),
        name="experts",
    )(blk_e, blk_x, nused, xs, w_gu, w_dn)


def _combine_kernel(dest_ref, h_ref, rf_ref, y_ref, o_ref, ya_ref, yb_ref, sem):
    tm = h_ref.shape[0]

    def copy(t, k, buf):
        return pltpu.make_async_copy(y_ref.at[pl.ds(dest_ref[0, 0, 2 * t + k], 1)], buf.at[pl.ds(t, 1)], sem)

    def issue(t, carry):
        copy(t, 0, ya_ref).start()
        copy(t, 1, yb_ref).start()
        return carry

    lax.fori_loop(0, tm, issue, 0)

    def drain(t, carry):
        copy(t, 0, ya_ref).wait()
        copy(t, 1, yb_ref).wait()
        return carry

    lax.fori_loop(0, tm, drain, 0)
    g1 = rf_ref[:, 0:1]
    g2 = rf_ref[:, 1:2]
    o_ref[...] = h_ref[...] + (ya_ref[...] * g1 + yb_ref[...] * g2)


def _combine(h2, rf, dest, ys, tm=256):
    n, d = h2.shape
    nt = n // tm
    return pl.pallas_call(
        _combine_kernel,
        grid=(nt,),
        in_specs=[
            pl.BlockSpec((1, 1, 2 * tm), lambda i: (i, 0, 0), memory_space=pltpu.SMEM),
            pl.BlockSpec((tm, d), lambda i: (i, 0)),
            pl.BlockSpec((tm, LANES), lambda i: (i, 0)),
            pl.BlockSpec(memory_space=pl.ANY),
        ],
        out_specs=pl.BlockSpec((tm, d), lambda i: (i, 0)),
        out_shape=jax.ShapeDtypeStruct((n, d), F32),
        scratch_shapes=[pltpu.VMEM((tm, d), F32), pltpu.VMEM((tm, d), F32), pltpu.SemaphoreType.DMA(())],
        compiler_params=_cparams(("arbitrary",)),
        name="combine",
    )(dest.reshape(nt, 1, 2 * tm), h2, rf, ys)


def kernel(x, mem, rel_bias, norm_mix_g, w_in, conv_dw_w, conv_dw_b, conv_ln_g, conv_ln_b, w_conv_out, q_norm_g, k_norm_g, attn_sinks, w_attn_out, w_mix_out, norm_x_g, norm_mem_g, w_xq, w_xkv, xq_norm_g, xk_norm_g, w_xo, norm_moe_g, w_router_group, b_router_group, w_router_expert, b_router_expert, w_expert_gu, w_expert_down):
    batch, seq, d = x.shape
    mem_len = mem.shape[1]
    n = batch * seq
    n_chunks = N_Q_HEADS // 2
    h = x.reshape(n, d)
    for l in range(norm_mix_g.shape[0]):
        c0, c1 = 2 * CONV_CH, 2 * CONV_CH + ATTN_WIDTH
        wq = w_in[l][:, c0:c1].reshape(d, N_KV_HEADS, n_chunks, HEAD_DIM).transpose(0, 2, 1, 3).reshape(d, ATTN_WIDTH)
        w = jnp.concatenate([w_in[l][:, :c0], wq, w_in[l][:, c1:]], axis=1).astype(BF16)
        q_gain = jnp.tile(q_norm_g[l].astype(F32) * HEAD_DIM ** -0.5, N_Q_HEADS)
        gain = jnp.concatenate([q_gain, jnp.tile(k_norm_g[l].astype(F32), N_KV_HEADS), jnp.ones((KV_WIDTH,), F32)])
        flag = jnp.concatenate([jnp.ones((ATTN_WIDTH + KV_WIDTH,), F32), jnp.zeros((KV_WIDTH,), F32)])
        u, qkv, gates = _in_proj(h, norm_mix_g[l], w, gain, flag)
        attn_o = _swa(qkv, rel_bias, attn_sinks[l], batch, seq)
        w_ao = w_attn_out[l].reshape(N_KV_HEADS, n_chunks, HEAD_DIM, d).transpose(1, 0, 2, 3).reshape(ATTN_WIDTH, d)
        h = _mix(u, attn_o, gates, h, conv_dw_w[l].reshape(CONV_WIDTH, CONV_CH), conv_dw_b[l], conv_ln_g[l],
                 conv_ln_b[l], w_conv_out[l].astype(BF16), w_ao.astype(BF16), w_mix_out[l].astype(BF16), seq)

        kgain = jnp.concatenate([jnp.tile(xk_norm_g[l].astype(F32), X_HEADS), jnp.ones((X_WIDTH,), F32)])
        kflag = jnp.concatenate([jnp.ones((X_WIDTH,), F32), jnp.zeros((X_WIDTH,), F32)])
        memkv = _mem_proj(mem.reshape(batch * mem_len, d), norm_mem_g[l], w_xkv[l].astype(BF16), kgain, kflag)
        xq_gain = jnp.tile(xq_norm_g[l].astype(F32) * X_HEAD_DIM ** -0.5, X_HEADS).reshape(1, -1)
        w_r = jnp.concatenate([w_router_group[l], w_router_expert[l]], axis=1).astype(F32)
        w_r = jnp.pad(w_r, ((0, 0), (0, LANES - w_r.shape[1])))
        wr_hi = w_r.astype(BF16)
        wr_lo = (w_r - wr_hi.astype(F32)).astype(BF16)
        b_r = jnp.concatenate([b_router_group[l], b_router_expert[l]]).astype(F32)
        b_r = jnp.pad(b_r, (0, LANES - b_r.shape[0])).reshape(1, LANES)
        h2, xp, ri, rf, cnt = _xattn_route(h, memkv, norm_x_g[l], w_xq[l].astype(BF16), xq_gain, w_xo[l].astype(BF16),
                                           norm_moe_g[l], wr_hi, wr_lo, b_r, seq, mem_len)

        counts = cnt[-1, 0, N_GROUPS:N_GROUPS + N_EXPERTS]
        padded = (counts + DISPATCH_BLOCK - 1) // DISPATCH_BLOCK * DISPATCH_BLOCK
        pad_end = jnp.cumsum(padded)
        pad_start = pad_end - padded
        n_blocks = -(-(2 * n) // DISPATCH_BLOCK) + N_EXPERTS
        rows = n_blocks * DISPATCH_BLOCK
        nused = (pad_end[-1] // DISPATCH_BLOCK).astype(jnp.int32)
        blk = jnp.minimum(jnp.arange(n_blocks, dtype=jnp.int32), nused - 1)
        blk_e = jnp.minimum(jnp.sum(pad_end[None, :] <= (blk * DISPATCH_BLOCK)[:, None], axis=1), N_EXPERTS - 1).astype(jnp.int32)
        dest = (pad_start[ri[:, 0:2]] + ri[:, 2:4]).astype(jnp.int32)

        xs = _dispatch(xp, dest, rows)
        ys = _experts(xs, w_expert_gu[l], w_expert_down[l], blk_e, blk, nused.reshape(1))
        h = _combine(h2, rf, dest, ys)
    return h.reshape(batch, seq, d)
```

```python
import functools
import math

import jax
import jax.numpy as jnp
import numpy as np
from jax import lax
from jax.experimental import pallas as pl
from jax.experimental.pallas import tpu as pltpu

EPS = 1e-6
NEG = -1e30

CONV_CH = 1024
CONV_WIDTH = 31
N_Q_HEADS = 16
N_KV_HEADS = 2
HEAD_DIM = 64
ATTN_WIDTH = N_Q_HEADS * HEAD_DIM
KV_WIDTH = N_KV_HEADS * HEAD_DIM
WINDOW = 128
NUM_BUCKETS = 32
MAX_DISTANCE = 128
X_HEADS = 4
X_HEAD_DIM = 128
X_WIDTH = X_HEADS * X_HEAD_DIM
N_GROUPS = 4
EXPERTS_PER_GROUP = 8
N_EXPERTS = N_GROUPS * EXPERTS_PER_GROUP
D_EXPERT = 512
DISPATCH_BLOCK = 256

LANES = 128
SUBLANES = 8
DMA_UNROLL = 8
CONV_HALO = 32
VMEM_LIMIT = 56 * 1024 * 1024

BF16 = jnp.bfloat16
F32 = jnp.float32


def _cparams(sem):
    return pltpu.CompilerParams(dimension_semantics=sem, vmem_limit_bytes=VMEM_LIMIT)


def _const_spec(shape):
    nd = len(shape)
    return pl.BlockSpec(shape, lambda *_: (0,) * nd, pipeline_mode=pl.Buffered(1))


def _rms_scale(x):
    return lax.rsqrt(jnp.mean(x * x, axis=-1, keepdims=True) + EPS)


def _group_sumsq(y, bd):
    sq = y * y
    hi = sq.astype(BF16)
    lo = (sq - hi.astype(F32)).astype(BF16)
    return (jnp.dot(hi, bd, preferred_element_type=F32) + jnp.dot(lo, bd, preferred_element_type=F32))


def _head_norm_store(y, gain_ref, flag_ref, bd, o_ref, head_dim):
    for c in range(y.shape[1] // LANES):
        sl = slice(c * LANES, (c + 1) * LANES)
        yc = y[:, sl]
        ss = _group_sumsq(yc, bd)
        normed = yc * lax.rsqrt(ss * (1.0 / head_dim) + EPS) * gain_ref[:, sl]
        o_ref[:, sl] = jnp.where(flag_ref[:, sl] > 0.0, normed, yc).astype(o_ref.dtype)


def _conv_chunk(c, ext_ref, sh_ref, cv_ref, dww_ref, dwb_ref, row_chunk=128):
    tm = cv_ref.shape[0]
    off = CONV_HALO - (CONV_WIDTH - 1)
    span = tm + CONV_HALO - SUBLANES
    sl = slice(c * LANES, (c + 1) * LANES)
    for r in range(1, SUBLANES):
        sh_ref[c % 2, r - 1] = ext_ref[r:r + span, sl]
    for rc in range(tm // row_chunk):
        acc = jnp.broadcast_to(dwb_ref[:, sl], (row_chunk, LANES))
        for j in range(CONV_WIDTH):
            q, r = divmod(off + j, SUBLANES)
            lo = SUBLANES * q + rc * row_chunk
            win = ext_ref[lo:lo + row_chunk, sl] if r == 0 else sh_ref[c % 2, r - 1, lo:lo + row_chunk, :]
            acc = acc + dww_ref[j:j + 1, sl] * win
        cv_ref[rc * row_chunk:(rc + 1) * row_chunk, sl] = acc


def _ln_silu(cv_ref, lng_ref, lnb_ref, act_ref):
    cv = cv_ref[...]
    mu = jnp.mean(cv, axis=-1, keepdims=True)
    cen = cv - mu
    var = jnp.mean(cen * cen, axis=-1, keepdims=True)
    ln = cen * lax.rsqrt(var + EPS) * lng_ref[...] + lnb_ref[...]
    act_ref[...] = (ln * jax.nn.sigmoid(ln)).astype(act_ref.dtype)


def _in_proj_kernel(x_ref, g_ref, w_ref, gain_ref, flag_ref, bd_ref, dww_ref, dwb_ref, lng_ref, lnb_ref,
                    act_ref, qkv_ref, gate_ref, xn_ref, ext_ref, sh_ref, cv_ref, *, tn, tiles_per_seq):
    tm = x_ref.shape[0]
    x = x_ref[...]
    xn_ref[...] = (x * _rms_scale(x) * g_ref[...]).astype(BF16)

    def proj(lo, width):
        return jnp.dot(xn_ref[...], w_ref[:, lo:lo + width], preferred_element_type=F32)

    @pl.when(pl.program_id(0) % tiles_per_seq == 0)
    def _():
        ext_ref[0:CONV_HALO, :] = jnp.zeros((CONV_HALO, CONV_CH), F32)

    n_u = act_ref.shape[1]
    for c in range(n_u // tn):
        a = proj(c * tn, tn)
        b = proj(n_u + c * tn, tn)
        ext_ref[CONV_HALO:, c * tn:(c + 1) * tn] = a * jax.nn.sigmoid(b)
    n_qkv = qkv_ref.shape[1]
    g0 = 2 * n_u + n_qkv
    n_conv = CONV_CH // LANES
    n_gate = gate_ref.shape[1] // tn
    for c in range(max(n_conv, n_gate)):
        if c < n_conv:
            _conv_chunk(c, ext_ref, sh_ref, cv_ref, dww_ref, dwb_ref)
        if c < n_gate:
            gate_ref[:, c * tn:(c + 1) * tn] = jax.nn.sigmoid(proj(g0 + c * tn, tn)).astype(gate_ref.dtype)
    ext_ref[0:CONV_HALO, :] = ext_ref[tm:tm + CONV_HALO, :]
    _ln_silu(cv_ref, lng_ref, lnb_ref, act_ref)
    _head_norm_store(proj(2 * n_u, n_qkv), gain_ref, flag_ref, bd_ref[...], qkv_ref, HEAD_DIM)


def _in_proj(x, g, w, gain, flag, dw_w, dw_b, ln_g, ln_b, seq, tm=256, tn=512):
    m, d = x.shape
    n_qkv = ATTN_WIDTH + 2 * KV_WIDTH
    n_gate = w.shape[1] - 2 * CONV_CH - n_qkv
    assert seq % tm == 0
    row = lambda i: (i, 0)
    vec = _const_spec((1, CONV_CH))
    return pl.pallas_call(
        functools.partial(_in_proj_kernel, tn=tn, tiles_per_seq=seq // tm),
        grid=(m // tm,),
        in_specs=[pl.BlockSpec((tm, d), row), _const_spec((1, d)), _const_spec(w.shape), _const_spec((1, n_qkv)),
                  _const_spec((1, n_qkv)), _const_spec((LANES, LANES)), _const_spec((CONV_WIDTH, CONV_CH)), vec, vec, vec],
        out_specs=[pl.BlockSpec((tm, CONV_CH), row), pl.BlockSpec((tm, n_qkv), row), pl.BlockSpec((tm, n_gate), row)],
        out_shape=[jax.ShapeDtypeStruct((m, CONV_CH), BF16), jax.ShapeDtypeStruct((m, n_qkv), BF16),
                   jax.ShapeDtypeStruct((m, n_gate), BF16)],
        scratch_shapes=[pltpu.VMEM((tm, d), BF16), pltpu.VMEM((tm + CONV_HALO, CONV_CH), F32),
                        pltpu.VMEM((2, SUBLANES - 1, tm + CONV_HALO - SUBLANES, LANES), F32), pltpu.VMEM((tm, CONV_CH), F32)],
        compiler_params=_cparams(("arbitrary",)),
        name="in_proj",
    )(x, g.reshape(1, d), w, gain.reshape(1, -1), flag.reshape(1, -1), _block_diag_ones(HEAD_DIM),
      dw_w, dw_b.reshape(1, -1), ln_g.reshape(1, -1), ln_b.reshape(1, -1))


def _mem_proj_kernel(x_ref, g_ref, w_ref, gain_ref, flag_ref, bd_ref, o_ref):
    x = x_ref[...]
    xn = (x * _rms_scale(x) * g_ref[...]).astype(BF16)
    y = jnp.dot(xn, w_ref[...], preferred_element_type=F32)
    _head_norm_store(y, gain_ref, flag_ref, bd_ref[...], o_ref, X_HEAD_DIM)


def _mem_proj(x, g, w, gain, flag, tm=256):
    m, d = x.shape
    n = w.shape[1]
    return pl.pallas_call(
        _mem_proj_kernel,
        grid=(m // tm,),
        in_specs=[pl.BlockSpec((tm, d), lambda i: (i, 0)), _const_spec((1, d)), _const_spec(w.shape), _const_spec((1, n)),
                  _const_spec((1, n)), _const_spec((LANES, LANES))],
        out_specs=pl.BlockSpec((tm, n), lambda i: (i, 0)),
        out_shape=jax.ShapeDtypeStruct((m, n), BF16),
        compiler_params=_cparams(("parallel",)),
        name="mem_proj",
    )(x, g.reshape(1, d), w, gain.reshape(1, -1), flag.reshape(1, -1), _block_diag_ones(X_HEAD_DIM))


def _block_diag_ones(group):
    r = np.arange(LANES)
    return jnp.asarray((r[:, None] // group) == (r[None, :] // group), dtype=BF16)


def _swa_kernel(sink_ref, q_ref, kp_ref, kc_ref, vp_ref, vc_ref, bias_ref, o_ref, *, n_chunks):
    first = pl.program_id(1) == 0
    qb = q_ref.shape[0]
    kcat = jnp.concatenate([kp_ref[...], kc_ref[...]], axis=0)
    vcat = jnp.concatenate([vp_ref[...], vc_ref[...]], axis=0)
    lane = lax.broadcasted_iota(jnp.int32, (qb, LANES), 1)
    row = lax.broadcasted_iota(jnp.int32, (2 * qb, 1), 0)
    col = lax.broadcasted_iota(jnp.int32, (2 * qb, 2 * qb), 1)
    no_prev = jnp.where(jnp.logical_and(first, col < qb), NEG, 0.0)
    for c in range(n_chunks):
        q2 = q_ref[:, c * LANES:(c + 1) * LANES]
        zero = jnp.zeros_like(q2)
        qs = jnp.concatenate([jnp.where(lane < HEAD_DIM, q2, zero), jnp.where(lane >= HEAD_DIM, q2, zero)], axis=0)
        s = lax.dot_general(qs, kcat, (((1,), (1,)), ((), ())), preferred_element_type=F32)
        s = s + bias_ref[c] + no_prev
        sink = jnp.where(row < qb, sink_ref[c], sink_ref[n_chunks + c])
        m = jnp.maximum(jnp.max(s, axis=-1, keepdims=True), sink)
        p = jnp.exp(s - m)
        den = jnp.sum(p, axis=-1, keepdims=True) + jnp.exp(sink - m)
        o2 = jnp.dot(p.astype(BF16), vcat, preferred_element_type=F32) / den
        o = jnp.where(lane < HEAD_DIM, o2[:qb], o2[qb:])
        o_ref[:, c * LANES:(c + 1) * LANES] = o.astype(o_ref.dtype)


def _t5_bucket_np(dist):
    n = np.maximum(dist, 0)
    max_exact = NUM_BUCKETS // 2
    large = max_exact + (np.log(np.maximum(n, 1).astype(np.float32) / max_exact)
                         / math.log(MAX_DISTANCE / max_exact) * (NUM_BUCKETS - max_exact)).astype(np.int32)
    large = np.minimum(large, NUM_BUCKETS - 1)
    return np.where(n < max_exact, n, large)


def _swa(qkv, rel_bias, sinks, batch, seq):
    qb = WINDOW
    nb = seq // qb
    n_chunks = N_Q_HEADS // 2
    qi = np.arange(qb)[:, None]
    kj = np.arange(2 * qb)[None, :]
    dist = qi + qb - kj
    valid = (dist >= 0) & (dist < WINDOW)
    onehot = jnp.asarray(_t5_bucket_np(dist)[:, :, None] == np.arange(NUM_BUCKETS), dtype=F32)
    bias = jnp.einsum('qkb,bh->qkh', onehot, rel_bias.astype(F32), precision=lax.Precision.HIGHEST)
    bias = jnp.where(valid[:, :, None], bias, NEG).transpose(2, 0, 1)
    bias = jnp.concatenate([bias[:n_chunks], bias[n_chunks:]], axis=1)
    kcol = ATTN_WIDTH // LANES
    vcol = kcol + 1

    def prev(b, n):
        return (b * nb + jnp.maximum(n - 1, 0))

    return pl.pallas_call(
        functools.partial(_swa_kernel, n_chunks=n_chunks),
        grid=(batch, nb),
        in_specs=[
            pl.BlockSpec(memory_space=pltpu.SMEM),
            pl.BlockSpec((qb, ATTN_WIDTH), lambda b, n: (b * nb + n, 0)),
            pl.BlockSpec((qb, LANES), lambda b, n: (prev(b, n), kcol)),
            pl.BlockSpec((qb, LANES), lambda b, n: (b * nb + n, kcol)),
            pl.BlockSpec((qb, LANES), lambda b, n: (prev(b, n), vcol)),
            pl.BlockSpec((qb, LANES), lambda b, n: (b * nb + n, vcol)),
            _const_spec((n_chunks, 2 * qb, 2 * qb)),
        ],
        out_specs=pl.BlockSpec((qb, ATTN_WIDTH), lambda b, n: (b * nb + n, 0)),
        out_shape=jax.ShapeDtypeStruct((batch * seq, ATTN_WIDTH), BF16),
        compiler_params=_cparams(("parallel", "parallel")),
        name="swa",
    )(sinks.astype(F32), qkv, qkv, qkv, qkv, qkv, bias)


def _mix_kernel(c_ref, a_ref, g0_ref, g1_ref, x_ref, wc_ref, wa_ref, wm_ref, o_ref):
    y_conv = jnp.dot(c_ref[...], wc_ref[...], preferred_element_type=F32)
    y_attn = jnp.dot(a_ref[...], wa_ref[...], preferred_element_type=F32)
    merged = g0_ref[...].astype(F32) * y_conv + g1_ref[...].astype(F32) * y_attn
    o_ref[...] = x_ref[...] + jnp.dot(merged.astype(BF16), wm_ref[...], preferred_element_type=F32)


def _mix(conv_act, attn_o, gates, x, w_conv_out, w_attn_out, w_mix_out, tm=256):
    n, d = x.shape
    return pl.pallas_call(
        _mix_kernel,
        grid=(n // tm,),
        in_specs=[
            pl.BlockSpec((tm, CONV_CH), lambda i: (i, 0)),
            pl.BlockSpec((tm, ATTN_WIDTH), lambda i: (i, 0)),
            pl.BlockSpec((tm, d), lambda i: (i, 0)),
            pl.BlockSpec((tm, d), lambda i: (i, 1)),
            pl.BlockSpec((tm, d), lambda i: (i, 0)),
            _const_spec((CONV_CH, d)),
            _const_spec((ATTN_WIDTH, d)),
            _const_spec((d, d)),
        ],
        out_specs=pl.BlockSpec((tm, d), lambda i: (i, 0)),
        out_shape=jax.ShapeDtypeStruct((n, d), F32),
        compiler_params=_cparams(("parallel",)),
        name="mix",
    )(conv_act, attn_o, gates, gates, x, w_conv_out, w_attn_out, w_mix_out)


def _xattn_route_kernel(h_ref, gx_ref, wq_ref, qgain_ref, bd_ref, k_ref, v_ref, wo_ref, gm_ref,
                        wrh_ref, wrl_ref, br_ref, tri_ref,
                        h2_ref, xp_ref, ri_ref, rf_ref, cnt_ref, run_ref):
    tm, d = h_ref.shape
    half = d // 2

    @pl.when(pl.program_id(0) == 0)
    def _():
        run_ref[...] = jnp.zeros_like(run_ref)

    h = h_ref[...]
    xn = (h * _rms_scale(h) * gx_ref[...]).astype(BF16)
    q = jnp.dot(xn, wq_ref[...], preferred_element_type=F32)
    heads = []
    for hd in range(X_HEADS):
        sl = slice(hd * X_HEAD_DIM, (hd + 1) * X_HEAD_DIM)
        qh = q[:, sl]
        qh = (qh * lax.rsqrt(_group_sumsq(qh, bd_ref[...]) * (1.0 / X_HEAD_DIM) + EPS) * qgain_ref[:, sl]).astype(BF16)
        s = lax.dot_general(qh, k_ref[:, sl], (((1,), (1,)), ((), ())), preferred_element_type=F32)
        p = jnp.exp(s - jnp.max(s, axis=-1, keepdims=True))
        den = jnp.sum(p, axis=-1, keepdims=True)
        heads.append((jnp.dot(p.astype(BF16), v_ref[:, sl], preferred_element_type=F32) / den).astype(BF16))
    o = jnp.concatenate(heads, axis=-1)
    h2 = h + jnp.dot(o, wo_ref[...], preferred_element_type=F32)
    h2_ref[...] = h2

    xn2 = h2 * _rms_scale(h2) * gm_ref[...]
    lo_bits = lax.bitcast_convert_type(xn2[:, :half].astype(BF16).astype(F32), jnp.uint32)
    hi_bits = lax.bitcast_convert_type(xn2[:, half:].astype(BF16).astype(F32), jnp.uint32)
    xp_ref[...] = (lo_bits >> 16) | hi_bits

    x_hi = xn2.astype(BF16)
    x_lo = (xn2 - x_hi.astype(F32)).astype(BF16)
    lg = (jnp.dot(x_hi, wrh_ref[...], preferred_element_type=F32)
          + jnp.dot(x_lo, wrh_ref[...], preferred_element_type=F32)
          + jnp.dot(x_hi, wrl_ref[...], preferred_element_type=F32) + br_ref[...])

    lane = lax.broadcasted_iota(jnp.int32, (tm, LANES), 1)
    lane_f = lane.astype(F32)
    big = jnp.float32(LANES)

    def first_argmax(vals, vmax):
        idx = jnp.min(jnp.where(vals == vmax, lane_f, big), axis=-1, keepdims=True)
        return idx.astype(jnp.int32)

    glog = jnp.where(lane < N_GROUPS, lg, -jnp.inf)
    gmax = jnp.max(glog, axis=-1, keepdims=True)
    grp = first_argmax(glog, gmax)
    pg_top = 1.0 / jnp.sum(jnp.exp(glog - gmax), axis=-1, keepdims=True)
    elane = lane - N_GROUPS
    emask = (elane >= 0) & (elane < N_EXPERTS) & ((elane >> 3) == grp)
    elog = jnp.where(emask, lg, -jnp.inf)
    m1 = jnp.max(elog, axis=-1, keepdims=True)
    i1 = first_argmax(elog, m1)
    elog2 = jnp.where(lane == i1, -jnp.inf, elog)
    m2 = jnp.max(elog2, axis=-1, keepdims=True)
    i2 = first_argmax(elog2, m2)
    den = jnp.sum(jnp.exp(elog - m1), axis=-1, keepdims=True)
    p1 = 1.0 / den
    p2 = jnp.exp(m2 - m1) / den
    psum = p1 + p2
    g1 = pg_top * (p1 / psum)
    g2 = pg_top * (p2 / psum)
    e1 = i1 - N_GROUPS
    e2 = i2 - N_GROUPS

    onehot = jnp.where((elane == e1) | (elane == e2), 1.0, 0.0)
    before = jnp.dot(tri_ref[...], onehot.astype(BF16), preferred_element_type=F32) + run_ref[0:1, :]
    r1 = jnp.sum(jnp.where(elane == e1, before, 0.0), axis=-1, keepdims=True).astype(jnp.int32)
    r2 = jnp.sum(jnp.where(elane == e2, before, 0.0), axis=-1, keepdims=True).astype(jnp.int32)
    run = run_ref[...] + jnp.sum(onehot, axis=0, keepdims=True)
    run_ref[...] = run
    cnt_ref[0] = run.astype(jnp.int32)

    ri_ref[...] = jnp.where(lane == 0, e1, jnp.where(lane == 1, e2, jnp.where(lane == 2, r1, jnp.where(lane == 3, r2, 0))))
    rf_ref[...] = jnp.where(lane == 0, g1, jnp.where(lane == 1, g2, 0.0))


def _xattn_route(h1, memkv, gx, w_xq, q_gain, w_xo, gm, wr_hi, wr_lo, b_r, seq, mem_len, tm=512):
    n, d = h1.shape
    tiles_per_seq = seq // tm
    nt = n // tm
    tri = jnp.asarray(np.tril(np.ones((tm, tm), np.float32), -1), dtype=BF16)
    bd = _block_diag_ones(X_HEAD_DIM)
    return pl.pallas_call(
        _xattn_route_kernel,
        grid=(nt,),
        in_specs=[
            pl.BlockSpec((tm, d), lambda i: (i, 0)),
            _const_spec((1, d)),
            _const_spec((d, X_WIDTH)),
            _const_spec((1, X_WIDTH)),
            _const_spec((LANES, LANES)),
            pl.BlockSpec((mem_len, X_WIDTH), lambda i: (i // tiles_per_seq, 0)),
            pl.BlockSpec((mem_len, X_WIDTH), lambda i: (i // tiles_per_seq, 1)),
            _const_spec((X_WIDTH, d)),
            _const_spec((1, d)),
            _const_spec((d, LANES)),
            _const_spec((d, LANES)),
            _const_spec((1, LANES)),
            _const_spec((tm, tm)),
        ],
        out_specs=[
            pl.BlockSpec((tm, d), lambda i: (i, 0)),
            pl.BlockSpec((tm, d // 2), lambda i: (i, 0)),
            pl.BlockSpec((tm, LANES), lambda i: (i, 0)),
            pl.BlockSpec((tm, LANES), lambda i: (i, 0)),
            pl.BlockSpec((1, 8, LANES), lambda i: (i, 0, 0)),
        ],
        out_shape=[
            jax.ShapeDtypeStruct((n, d), F32),
            jax.ShapeDtypeStruct((n, d // 2), jnp.uint32),
            jax.ShapeDtypeStruct((n, LANES), jnp.int32),
            jax.ShapeDtypeStruct((n, LANES), F32),
            jax.ShapeDtypeStruct((nt, 8, LANES), jnp.int32),
        ],
        scratch_shapes=[pltpu.VMEM((8, LANES), F32)],
        compiler_params=_cparams(("arbitrary",)),
        name="xattn_route",
    )(h1, gx.reshape(1, d), w_xq, q_gain, bd, memkv, memkv, w_xo, gm.reshape(1, d), wr_hi, wr_lo, b_r, tri)


def _dispatch_kernel(tail_ref, dest_ref, x_ref, xs_ref, zero_ref, sem, zsem):
    tm = x_ref.shape[0]

    @pl.when(pl.program_id(0) == 0)
    def _():
        zero_ref[...] = jnp.zeros_like(zero_ref)

        def zero_copy(e):
            start = pl.multiple_of(tail_ref[e], DISPATCH_BLOCK)
            return pltpu.make_async_copy(zero_ref, xs_ref.at[pl.ds(start, DISPATCH_BLOCK)], zsem)

        for e in range(N_EXPERTS):
            pl.when(tail_ref[e] >= 0)(lambda e=e: zero_copy(e).start())
        for e in range(N_EXPERTS):
            pl.when(tail_ref[e] >= 0)(lambda e=e: zero_copy(e).wait())

        def unused_copy(b):
            return pltpu.make_async_copy(zero_ref, xs_ref.at[pl.ds(pl.multiple_of(b * DISPATCH_BLOCK, DISPATCH_BLOCK),
                                                                   DISPATCH_BLOCK)], zsem)

        n_blocks = xs_ref.shape[0] // DISPATCH_BLOCK
        lax.fori_loop(tail_ref[N_EXPERTS], n_blocks, lambda b, c: (unused_copy(b).start(), c)[1], 0)
        lax.fori_loop(tail_ref[N_EXPERTS], n_blocks, lambda b, c: (unused_copy(b).wait(), c)[1], 0)

    def copy(t, k):
        return pltpu.make_async_copy(x_ref.at[pl.ds(t, 1)], xs_ref.at[pl.ds(dest_ref[0, 0, 2 * t + k], 1)], sem)

    def issue(tb, carry):
        for j in range(DMA_UNROLL):
            copy(tb * DMA_UNROLL + j, 0).start()
            copy(tb * DMA_UNROLL + j, 1).start()
        return carry

    lax.fori_loop(0, tm // DMA_UNROLL, issue, 0)

    def drain(tb, carry):
        for j in range(DMA_UNROLL):
            copy(tb * DMA_UNROLL + j, 0).wait()
            copy(tb * DMA_UNROLL + j, 1).wait()
        return carry

    lax.fori_loop(0, tm // DMA_UNROLL, drain, 0)


def _dispatch(xp, dest, tail, rows, tm=512):
    n, w = xp.shape
    nt = n // tm
    grid_spec = pltpu.PrefetchScalarGridSpec(
        num_scalar_prefetch=1,
        grid=(nt,),
        in_specs=[
            pl.BlockSpec((1, 1, 2 * tm), lambda i, tail: (i, 0, 0), memory_space=pltpu.SMEM),
            pl.BlockSpec((tm, w), lambda i, tail: (i, 0)),
        ],
        out_specs=pl.BlockSpec(memory_space=pl.ANY),
        scratch_shapes=[pltpu.VMEM((DISPATCH_BLOCK, w), xp.dtype), pltpu.SemaphoreType.DMA(()),
                        pltpu.SemaphoreType.DMA(())],
    )
    return pl.pallas_call(
        _dispatch_kernel,
        grid_spec=grid_spec,
        out_shape=jax.ShapeDtypeStruct((rows, w), xp.dtype),
        compiler_params=_cparams(("arbitrary",)),
        name="dispatch",
    )(tail, dest.reshape(nt, 1, 2 * tm), xp)


def _experts_kernel(blk_e_ref, blk_x_ref, nused_ref, xs_ref, wgu_ref, wdn_ref, y_ref, wgu_bf, wdn_bf):
    b = pl.program_id(0)
    half = xs_ref.shape[1]
    prev_e = blk_e_ref[jnp.maximum(b - 1, 0)]
    new_expert = jnp.logical_or(b == 0, blk_e_ref[b] != prev_e)

    @pl.when(new_expert)
    def _():
        wgu_bf[...] = wgu_ref[0].astype(BF16)
        wdn_bf[...] = wdn_ref[0].astype(BF16)

    @pl.when(b < nused_ref[0])
    def _():
        w = xs_ref[...]
        x_lo = lax.bitcast_convert_type(w << 16, F32).astype(BF16)
        x_hi = lax.bitcast_convert_type(w & jnp.uint32(0xFFFF0000), F32).astype(BF16)
        gu = (jnp.dot(x_lo, wgu_bf[0:half, :], preferred_element_type=F32)
              + jnp.dot(x_hi, wgu_bf[half:, :], preferred_element_type=F32))
        g = gu[:, :D_EXPERT]
        u = gu[:, D_EXPERT:]
        act = (g * jax.nn.sigmoid(g) * u).astype(BF16)
        y_ref[...] = jnp.dot(act, wdn_bf[...], preferred_element_type=F32)

    @pl.when(b >= nused_ref[0])
    def _():
        y_ref[...] = jnp.zeros_like(y_ref)


def _experts(xs, w_gu, w_dn, blk_e, blk_x, nused):
    rows, half = xs.shape
    d = 2 * half
    n_blocks = rows // DISPATCH_BLOCK
    grid_spec = pltpu.PrefetchScalarGridSpec(
        num_scalar_prefetch=3,
        grid=(n_blocks,),
        in_specs=[
            pl.BlockSpec((DISPATCH_BLOCK, half), lambda b, be, bx, nu: (bx[b], 0)),
            pl.BlockSpec((1, d, 2 * D_EXPERT), lambda b, be, bx, nu: (be[b], 0, 0)),
            pl.BlockSpec((1, D_EXPERT, d), lambda b, be, bx, nu: (be[b], 0, 0)),
        ],
        out_specs=pl.BlockSpec((DISPATCH_BLOCK, d), lambda b, be, bx, nu: (b, 0)),
        scratch_shapes=[pltpu.VMEM((d, 2 * D_EXPERT), BF16), pltpu.VMEM((D_EXPERT, d), BF16)],
    )
    return pl.pallas_call(
        _experts_kernel,
        grid_spec=grid_spec,
        out_shape=jax.ShapeDtypeStruct((rows, d), F32),
        compiler_params=_cparams(("arbitrary",)),
        name="experts",
    )(blk_e, blk_x, nused, xs, w_gu, w_dn)


def _combine_kernel(dest_ref, h_ref, rf_ref, y_ref, o_ref, ya_ref, yb_ref, sem):
    tm = h_ref.shape[0]

    def copy(t, k, buf):
        return pltpu.make_async_copy(y_ref.at[pl.ds(dest_ref[0, 0, 2 * t + k], 1)], buf.at[pl.ds(t, 1)], sem)

    def issue(tb, carry):
        for j in range(DMA_UNROLL):
            copy(tb * DMA_UNROLL + j, 0, ya_ref).start()
            copy(tb * DMA_UNROLL + j, 1, yb_ref).start()
        return carry

    lax.fori_loop(0, tm // DMA_UNROLL, issue, 0)

    def drain(tb, carry):
        for j in range(DMA_UNROLL):
            copy(tb * DMA_UNROLL + j, 0, ya_ref).wait()
            copy(tb * DMA_UNROLL + j, 1, yb_ref).wait()
        return carry

    lax.fori_loop(0, tm // DMA_UNROLL, drain, 0)
    g1 = rf_ref[:, 0:1]
    g2 = rf_ref[:, 1:2]
    o_ref[...] = h_ref[...] + (ya_ref[...] * g1 + yb_ref[...] * g2)


def _combine(h2, rf, dest, ys, tm=256):
    n, d = h2.shape
    nt = n // tm
    return pl.pallas_call(
        _combine_kernel,
        grid=(nt,),
        in_specs=[
            pl.BlockSpec((1, 1, 2 * tm), lambda i: (i, 0, 0), memory_space=pltpu.SMEM),
            pl.BlockSpec((tm, d), lambda i: (i, 0)),
            pl.BlockSpec((tm, LANES), lambda i: (i, 0)),
            pl.BlockSpec(memory_space=pl.ANY),
        ],
        out_specs=pl.BlockSpec((tm, d), lambda i: (i, 0)),
        out_shape=jax.ShapeDtypeStruct((n, d), F32),
        scratch_shapes=[pltpu.VMEM((tm, d), F32), pltpu.VMEM((tm, d), F32), pltpu.SemaphoreType.DMA(())],
        compiler_params=_cparams(("arbitrary",)),
        name="combine",
    )(dest.reshape(nt, 1, 2 * tm), h2, rf, ys)


def kernel(x, mem, rel_bias, norm_mix_g, w_in, conv_dw_w, conv_dw_b, conv_ln_g, conv_ln_b, w_conv_out, q_norm_g, k_norm_g, attn_sinks, w_attn_out, w_mix_out, norm_x_g, norm_mem_g, w_xq, w_xkv, xq_norm_g, xk_norm_g, w_xo, norm_moe_g, w_router_group, b_router_group, w_router_expert, b_router_expert, w_expert_gu, w_expert_down):
    batch, seq, d = x.shape
    mem_len = mem.shape[1]
    n = batch * seq
    n_chunks = N_Q_HEADS // 2
    h = x.reshape(n, d)
    for l in range(norm_mix_g.shape[0]):
        c0, c1 = 2 * CONV_CH, 2 * CONV_CH + ATTN_WIDTH
        wq = w_in[l][:, c0:c1].reshape(d, N_KV_HEADS, n_chunks, HEAD_DIM).transpose(0, 2, 1, 3).reshape(d, ATTN_WIDTH)
        w = jnp.concatenate([w_in[l][:, :c0], wq, w_in[l][:, c1:]], axis=1).astype(BF16)
        q_gain = jnp.tile(q_norm_g[l].astype(F32) * HEAD_DIM ** -0.5, N_Q_HEADS)
        gain = jnp.concatenate([q_gain, jnp.tile(k_norm_g[l].astype(F32), N_KV_HEADS), jnp.ones((KV_WIDTH,), F32)])
        flag = jnp.concatenate([jnp.ones((ATTN_WIDTH + KV_WIDTH,), F32), jnp.zeros((KV_WIDTH,), F32)])
        conv_act, qkv, gates = _in_proj(h, norm_mix_g[l], w, gain, flag, conv_dw_w[l].reshape(CONV_WIDTH, CONV_CH),
                                        conv_dw_b[l], conv_ln_g[l], conv_ln_b[l], seq)
        attn_o = _swa(qkv, rel_bias, attn_sinks[l], batch, seq)
        w_ao = w_attn_out[l].reshape(N_KV_HEADS, n_chunks, HEAD_DIM, d).transpose(1, 0, 2, 3).reshape(ATTN_WIDTH, d)
        h = _mix(conv_act, attn_o, gates, h, w_conv_out[l].astype(BF16), w_ao.astype(BF16), w_mix_out[l].astype(BF16))

        kgain = jnp.concatenate([jnp.tile(xk_norm_g[l].astype(F32), X_HEADS), jnp.ones((X_WIDTH,), F32)])
        kflag = jnp.concatenate([jnp.ones((X_WIDTH,), F32), jnp.zeros((X_WIDTH,), F32)])
        memkv = _mem_proj(mem.reshape(batch * mem_len, d), norm_mem_g[l], w_xkv[l].astype(BF16), kgain, kflag)
        xq_gain = jnp.tile(xq_norm_g[l].astype(F32) * X_HEAD_DIM ** -0.5, X_HEADS).reshape(1, -1)
        w_r = jnp.concatenate([w_router_group[l], w_router_expert[l]], axis=1).astype(F32)
        w_r = jnp.pad(w_r, ((0, 0), (0, LANES - w_r.shape[1])))
        wr_hi = w_r.astype(BF16)
        wr_lo = (w_r - wr_hi.astype(F32)).astype(BF16)
        b_r = jnp.concatenate([b_router_group[l], b_router_expert[l]]).astype(F32)
        b_r = jnp.pad(b_r, (0, LANES - b_r.shape[0])).reshape(1, LANES)
        h2, xp, ri, rf, cnt = _xattn_route(h, memkv, norm_x_g[l], w_xq[l].astype(BF16), xq_gain, w_xo[l].astype(BF16),
                                           norm_moe_g[l], wr_hi, wr_lo, b_r, seq, mem_len)

        counts = cnt[-1, 0, N_GROUPS:N_GROUPS + N_EXPERTS]
        padded = (counts + DISPATCH_BLOCK - 1) // DISPATCH_BLOCK * DISPATCH_BLOCK
        pad_end = jnp.cumsum(padded)
        pad_start = pad_end - padded
        n_blocks = -(-(2 * n) // DISPATCH_BLOCK) + N_EXPERTS
        rows = n_blocks * DISPATCH_BLOCK
        nused = (pad_end[-1] // DISPATCH_BLOCK).astype(jnp.int32)
        blk = jnp.minimum(jnp.arange(n_blocks, dtype=jnp.int32), nused - 1)
        blk_e = jnp.minimum(jnp.sum(pad_end[None, :] <= (blk * DISPATCH_BLOCK)[:, None], axis=1), N_EXPERTS - 1).astype(jnp.int32)
        is_e = ri[:, 0:2, None] == jnp.arange(N_EXPERTS, dtype=jnp.int32)
        dest = (jnp.sum(jnp.where(is_e, pad_start, 0), axis=-1) + ri[:, 2:4]).astype(jnp.int32)

        tail = jnp.concatenate([jnp.where(counts > 0, pad_end - DISPATCH_BLOCK, -1), nused.reshape(1)]).astype(jnp.int32)
        xs = _dispatch(xp, dest, tail, rows)
        ys = _experts(xs, w_expert_gu[l], w_expert_down[l], blk_e, blk, nused.reshape(1))
        h = _combine(h2, rf, dest, ys)
    return h.reshape(batch, seq, d)
```

```python
import functools
import math

import jax
import jax.numpy as jnp
import numpy as np
from jax import lax
from jax.experimental import pallas as pl
from jax.experimental.pallas import tpu as pltpu

EPS = 1e-6
NEG = -1e30

CONV_CH = 1024
CONV_WIDTH = 31
N_Q_HEADS = 16
N_KV_HEADS = 2
HEAD_DIM = 64
ATTN_WIDTH = N_Q_HEADS * HEAD_DIM
KV_WIDTH = N_KV_HEADS * HEAD_DIM
WINDOW = 128
NUM_BUCKETS = 32
MAX_DISTANCE = 128
X_HEADS = 4
X_HEAD_DIM = 128
X_WIDTH = X_HEADS * X_HEAD_DIM
N_GROUPS = 4
EXPERTS_PER_GROUP = 8
N_EXPERTS = N_GROUPS * EXPERTS_PER_GROUP
D_EXPERT = 512
DISPATCH_BLOCK = 256

LANES = 128
SUBLANES = 8
DMA_UNROLL = 8
CONV_HALO = 32
VMEM_LIMIT = 56 * 1024 * 1024

BF16 = jnp.bfloat16
F32 = jnp.float32


def _cparams(sem):
    return pltpu.CompilerParams(dimension_semantics=sem, vmem_limit_bytes=VMEM_LIMIT)


def _const_spec(shape):
    nd = len(shape)
    return pl.BlockSpec(shape, lambda *_: (0,) * nd, pipeline_mode=pl.Buffered(1))


def _rms_scale(x):
    return lax.rsqrt(jnp.mean(x * x, axis=-1, keepdims=True) + EPS)


def _group_sumsq(y, bd):
    sq = y * y
    hi = sq.astype(BF16)
    lo = (sq - hi.astype(F32)).astype(BF16)
    return (jnp.dot(hi, bd, preferred_element_type=F32) + jnp.dot(lo, bd, preferred_element_type=F32))


def _pack_bf16_pair(lo, hi):
    lo_bits = lax.bitcast_convert_type(lo.astype(BF16).astype(F32), jnp.uint32)
    hi_bits = lax.bitcast_convert_type(hi.astype(BF16).astype(F32), jnp.uint32)
    return (lo_bits >> 16) | hi_bits


def _unpack_bf16_pair(w):
    return (lax.bitcast_convert_type(w << 16, F32), lax.bitcast_convert_type(w & jnp.uint32(0xFFFF0000), F32))


def _head_norm_store(y, gain_ref, flag_ref, bd, o_ref, head_dim):
    for c in range(y.shape[1] // LANES):
        sl = slice(c * LANES, (c + 1) * LANES)
        yc = y[:, sl]
        ss = _group_sumsq(yc, bd)
        normed = yc * lax.rsqrt(ss * (1.0 / head_dim) + EPS) * gain_ref[:, sl]
        o_ref[:, sl] = jnp.where(flag_ref[:, sl] > 0.0, normed, yc).astype(o_ref.dtype)


def _conv_chunk(c, ext_ref, sh_ref, cv_ref, dww_ref, dwb_ref, row_chunk=128):
    tm = cv_ref.shape[0]
    off = CONV_HALO - (CONV_WIDTH - 1)
    span = tm + CONV_HALO - SUBLANES
    sl = slice(c * LANES, (c + 1) * LANES)
    for r in range(1, SUBLANES):
        sh_ref[c % 2, r - 1] = ext_ref[r:r + span, sl]
    for rc in range(tm // row_chunk):
        acc = jnp.broadcast_to(dwb_ref[:, sl], (row_chunk, LANES))
        for j in range(CONV_WIDTH):
            q, r = divmod(off + j, SUBLANES)
            lo = SUBLANES * q + rc * row_chunk
            win = ext_ref[lo:lo + row_chunk, sl] if r == 0 else sh_ref[c % 2, r - 1, lo:lo + row_chunk, :]
            acc = acc + dww_ref[j:j + 1, sl] * win
        cv_ref[rc * row_chunk:(rc + 1) * row_chunk, sl] = acc


def _ln_silu(cv_ref, lng_ref, lnb_ref, act_ref):
    cv = cv_ref[...]
    mu = jnp.mean(cv, axis=-1, keepdims=True)
    cen = cv - mu
    var = jnp.mean(cen * cen, axis=-1, keepdims=True)
    ln = cen * lax.rsqrt(var + EPS) * lng_ref[...] + lnb_ref[...]
    act_ref[...] = (ln * jax.nn.sigmoid(ln)).astype(act_ref.dtype)


def _in_proj_kernel(x_ref, g_ref, w_ref, gain_ref, flag_ref, bd_ref, dww_ref, dwb_ref, lng_ref, lnb_ref,
                    act_ref, qkv_ref, gate_ref, xn_ref, ext_ref, sh_ref, cv_ref, *, tn, tiles_per_seq):
    tm = x_ref.shape[0]
    x = x_ref[...]
    xn_ref[...] = (x * _rms_scale(x) * g_ref[...]).astype(BF16)

    def proj(lo, width):
        return jnp.dot(xn_ref[...], w_ref[:, lo:lo + width], preferred_element_type=F32)

    @pl.when(pl.program_id(0) % tiles_per_seq == 0)
    def _():
        ext_ref[0:CONV_HALO, :] = jnp.zeros((CONV_HALO, CONV_CH), F32)

    n_u = act_ref.shape[1]
    for c in range(n_u // tn):
        a = proj(c * tn, tn)
        b = proj(n_u + c * tn, tn)
        ext_ref[CONV_HALO:, c * tn:(c + 1) * tn] = a * jax.nn.sigmoid(b)
    n_qkv = qkv_ref.shape[1]
    g0 = 2 * n_u + n_qkv
    n_conv = CONV_CH // LANES
    n_gate = gate_ref.shape[1] // tn
    for c in range(max(n_conv, n_gate)):
        if c < n_conv:
            _conv_chunk(c, ext_ref, sh_ref, cv_ref, dww_ref, dwb_ref)
        if c < n_gate:
            gate_ref[:, c * tn:(c + 1) * tn] = jax.nn.sigmoid(proj(g0 + c * tn, tn)).astype(gate_ref.dtype)
    ext_ref[0:CONV_HALO, :] = ext_ref[tm:tm + CONV_HALO, :]
    _ln_silu(cv_ref, lng_ref, lnb_ref, act_ref)
    _head_norm_store(proj(2 * n_u, n_qkv), gain_ref, flag_ref, bd_ref[...], qkv_ref, HEAD_DIM)


def _in_proj(x, g, w, gain, flag, dw_w, dw_b, ln_g, ln_b, seq, tm=256, tn=512):
    m, d = x.shape
    n_qkv = ATTN_WIDTH + 2 * KV_WIDTH
    n_gate = w.shape[1] - 2 * CONV_CH - n_qkv
    assert seq % tm == 0
    row = lambda i: (i, 0)
    vec = _const_spec((1, CONV_CH))
    return pl.pallas_call(
        functools.partial(_in_proj_kernel, tn=tn, tiles_per_seq=seq // tm),
        grid=(m // tm,),
        in_specs=[pl.BlockSpec((tm, d), row), _const_spec((1, d)), _const_spec(w.shape), _const_spec((1, n_qkv)),
                  _const_spec((1, n_qkv)), _const_spec((LANES, LANES)), _const_spec((CONV_WIDTH, CONV_CH)), vec, vec, vec],
        out_specs=[pl.BlockSpec((tm, CONV_CH), row), pl.BlockSpec((tm, n_qkv), row), pl.BlockSpec((tm, n_gate), row)],
        out_shape=[jax.ShapeDtypeStruct((m, CONV_CH), BF16), jax.ShapeDtypeStruct((m, n_qkv), BF16),
                   jax.ShapeDtypeStruct((m, n_gate), BF16)],
        scratch_shapes=[pltpu.VMEM((tm, d), BF16), pltpu.VMEM((tm + CONV_HALO, CONV_CH), F32),
                        pltpu.VMEM((2, SUBLANES - 1, tm + CONV_HALO - SUBLANES, LANES), F32), pltpu.VMEM((tm, CONV_CH), F32)],
        compiler_params=_cparams(("arbitrary",)),
        name="in_proj",
    )(x, g.reshape(1, d), w, gain.reshape(1, -1), flag.reshape(1, -1), _block_diag_ones(HEAD_DIM),
      dw_w, dw_b.reshape(1, -1), ln_g.reshape(1, -1), ln_b.reshape(1, -1))


def _mem_proj_kernel(x_ref, g_ref, w_ref, gain_ref, flag_ref, bd_ref, o_ref):
    x = x_ref[...]
    xn = (x * _rms_scale(x) * g_ref[...]).astype(BF16)
    y = jnp.dot(xn, w_ref[...], preferred_element_type=F32)
    _head_norm_store(y, gain_ref, flag_ref, bd_ref[...], o_ref, X_HEAD_DIM)


def _mem_proj(x, g, w, gain, flag, tm=256):
    m, d = x.shape
    n = w.shape[1]
    return pl.pallas_call(
        _mem_proj_kernel,
        grid=(m // tm,),
        in_specs=[pl.BlockSpec((tm, d), lambda i: (i, 0)), _const_spec((1, d)), _const_spec(w.shape), _const_spec((1, n)),
                  _const_spec((1, n)), _const_spec((LANES, LANES))],
        out_specs=pl.BlockSpec((tm, n), lambda i: (i, 0)),
        out_shape=jax.ShapeDtypeStruct((m, n), BF16),
        compiler_params=_cparams(("parallel",)),
        name="mem_proj",
    )(x, g.reshape(1, d), w, gain.reshape(1, -1), flag.reshape(1, -1), _block_diag_ones(X_HEAD_DIM))


def _block_diag_ones(group):
    r = np.arange(LANES)
    return jnp.asarray((r[:, None] // group) == (r[None, :] // group), dtype=BF16)


def _swa_kernel(sink_ref, q_ref, kp_ref, kc_ref, vp_ref, vc_ref, bias_ref, o_ref, *, n_chunks):
    first = pl.program_id(1) == 0
    qb = q_ref.shape[0]
    kcat = jnp.concatenate([kp_ref[...], kc_ref[...]], axis=0)
    vcat = jnp.concatenate([vp_ref[...], vc_ref[...]], axis=0)
    lane = lax.broadcasted_iota(jnp.int32, (qb, LANES), 1)
    row = lax.broadcasted_iota(jnp.int32, (2 * qb, 1), 0)
    col = lax.broadcasted_iota(jnp.int32, (2 * qb, 2 * qb), 1)
    no_prev = jnp.where(jnp.logical_and(first, col < qb), NEG, 0.0)
    for c in range(n_chunks):
        q2 = q_ref[:, c * LANES:(c + 1) * LANES]
        zero = jnp.zeros_like(q2)
        qs = jnp.concatenate([jnp.where(lane < HEAD_DIM, q2, zero), jnp.where(lane >= HEAD_DIM, q2, zero)], axis=0)
        s = lax.dot_general(qs, kcat, (((1,), (1,)), ((), ())), preferred_element_type=F32)
        s = s + bias_ref[c] + no_prev
        sink = jnp.where(row < qb, sink_ref[c], sink_ref[n_chunks + c])
        m = jnp.maximum(jnp.max(s, axis=-1, keepdims=True), sink)
        p = jnp.exp(s - m)
        den = jnp.sum(p, axis=-1, keepdims=True) + jnp.exp(sink - m)
        o2 = jnp.dot(p.astype(BF16), vcat, preferred_element_type=F32) / den
        o = jnp.where(lane < HEAD_DIM, o2[:qb], o2[qb:])
        o_ref[:, c * LANES:(c + 1) * LANES] = o.astype(o_ref.dtype)


def _t5_bucket_np(dist):
    n = np.maximum(dist, 0)
    max_exact = NUM_BUCKETS // 2
    large = max_exact + (np.log(np.maximum(n, 1).astype(np.float32) / max_exact)
                         / math.log(MAX_DISTANCE / max_exact) * (NUM_BUCKETS - max_exact)).astype(np.int32)
    large = np.minimum(large, NUM_BUCKETS - 1)
    return np.where(n < max_exact, n, large)


def _swa(qkv, rel_bias, sinks, batch, seq):
    qb = WINDOW
    nb = seq // qb
    n_chunks = N_Q_HEADS // 2
    qi = np.arange(qb)[:, None]
    kj = np.arange(2 * qb)[None, :]
    dist = qi + qb - kj
    valid = (dist >= 0) & (dist < WINDOW)
    onehot = jnp.asarray(_t5_bucket_np(dist)[:, :, None] == np.arange(NUM_BUCKETS), dtype=F32)
    bias = jnp.einsum('qkb,bh->qkh', onehot, rel_bias.astype(F32), precision=lax.Precision.HIGHEST)
    bias = jnp.where(valid[:, :, None], bias, NEG).transpose(2, 0, 1)
    bias = jnp.concatenate([bias[:n_chunks], bias[n_chunks:]], axis=1)
    kcol = ATTN_WIDTH // LANES
    vcol = kcol + 1

    def prev(b, n):
        return (b * nb + jnp.maximum(n - 1, 0))

    return pl.pallas_call(
        functools.partial(_swa_kernel, n_chunks=n_chunks),
        grid=(batch, nb),
        in_specs=[
            pl.BlockSpec(memory_space=pltpu.SMEM),
            pl.BlockSpec((qb, ATTN_WIDTH), lambda b, n: (b * nb + n, 0)),
            pl.BlockSpec((qb, LANES), lambda b, n: (prev(b, n), kcol)),
            pl.BlockSpec((qb, LANES), lambda b, n: (b * nb + n, kcol)),
            pl.BlockSpec((qb, LANES), lambda b, n: (prev(b, n), vcol)),
            pl.BlockSpec((qb, LANES), lambda b, n: (b * nb + n, vcol)),
            _const_spec((n_chunks, 2 * qb, 2 * qb)),
        ],
        out_specs=pl.BlockSpec((qb, ATTN_WIDTH), lambda b, n: (b * nb + n, 0)),
        out_shape=jax.ShapeDtypeStruct((batch * seq, ATTN_WIDTH), BF16),
        compiler_params=_cparams(("parallel", "parallel")),
        name="swa",
    )(sinks.astype(F32), qkv, qkv, qkv, qkv, qkv, bias)


def _mix_kernel(c_ref, a_ref, g0_ref, g1_ref, x_ref, wc_ref, wa_ref, wm_ref, o_ref):
    y_conv = jnp.dot(c_ref[...], wc_ref[...], preferred_element_type=F32)
    y_attn = jnp.dot(a_ref[...], wa_ref[...], preferred_element_type=F32)
    merged = g0_ref[...].astype(F32) * y_conv + g1_ref[...].astype(F32) * y_attn
    o_ref[...] = x_ref[...] + jnp.dot(merged.astype(BF16), wm_ref[...], preferred_element_type=F32)


def _mix(conv_act, attn_o, gates, x, w_conv_out, w_attn_out, w_mix_out, tm=256):
    n, d = x.shape
    return pl.pallas_call(
        _mix_kernel,
        grid=(n // tm,),
        in_specs=[
            pl.BlockSpec((tm, CONV_CH), lambda i: (i, 0)),
            pl.BlockSpec((tm, ATTN_WIDTH), lambda i: (i, 0)),
            pl.BlockSpec((tm, d), lambda i: (i, 0)),
            pl.BlockSpec((tm, d), lambda i: (i, 1)),
            pl.BlockSpec((tm, d), lambda i: (i, 0)),
            _const_spec((CONV_CH, d)),
            _const_spec((ATTN_WIDTH, d)),
            _const_spec((d, d)),
        ],
        out_specs=pl.BlockSpec((tm, d), lambda i: (i, 0)),
        out_shape=jax.ShapeDtypeStruct((n, d), F32),
        compiler_params=_cparams(("parallel",)),
        name="mix",
    )(conv_act, attn_o, gates, gates, x, w_conv_out, w_attn_out, w_mix_out)


def _xattn_route_kernel(h_ref, gx_ref, wq_ref, qgain_ref, bd_ref, k_ref, v_ref, wo_ref, gm_ref,
                        wrh_ref, wrl_ref, br_ref, tri_ref,
                        h2_ref, xp_ref, ri_ref, rf_ref, cnt_ref, run_ref):
    tm, d = h_ref.shape
    half = d // 2

    @pl.when(pl.program_id(0) == 0)
    def _():
        run_ref[...] = jnp.zeros_like(run_ref)

    h = h_ref[...]
    xn = (h * _rms_scale(h) * gx_ref[...]).astype(BF16)
    q = jnp.dot(xn, wq_ref[...], preferred_element_type=F32)
    heads = []
    for hd in range(X_HEADS):
        sl = slice(hd * X_HEAD_DIM, (hd + 1) * X_HEAD_DIM)
        qh = q[:, sl]
        qh = (qh * lax.rsqrt(_group_sumsq(qh, bd_ref[...]) * (1.0 / X_HEAD_DIM) + EPS) * qgain_ref[:, sl]).astype(BF16)
        s = lax.dot_general(qh, k_ref[:, sl], (((1,), (1,)), ((), ())), preferred_element_type=F32)
        p = jnp.exp(s - jnp.max(s, axis=-1, keepdims=True))
        den = jnp.sum(p, axis=-1, keepdims=True)
        heads.append((jnp.dot(p.astype(BF16), v_ref[:, sl], preferred_element_type=F32) / den).astype(BF16))
    o = jnp.concatenate(heads, axis=-1)
    h2 = h + jnp.dot(o, wo_ref[...], preferred_element_type=F32)
    h2_ref[...] = h2

    xn2 = h2 * _rms_scale(h2) * gm_ref[...]
    xp_ref[...] = _pack_bf16_pair(xn2[:, :half], xn2[:, half:])

    x_hi = xn2.astype(BF16)
    x_lo = (xn2 - x_hi.astype(F32)).astype(BF16)
    lg = (jnp.dot(x_hi, wrh_ref[...], preferred_element_type=F32)
          + jnp.dot(x_lo, wrh_ref[...], preferred_element_type=F32)
          + jnp.dot(x_hi, wrl_ref[...], preferred_element_type=F32) + br_ref[...])

    lane = lax.broadcasted_iota(jnp.int32, (tm, LANES), 1)
    lane_f = lane.astype(F32)
    big = jnp.float32(LANES)

    def first_argmax(vals, vmax):
        idx = jnp.min(jnp.where(vals == vmax, lane_f, big), axis=-1, keepdims=True)
        return idx.astype(jnp.int32)

    glog = jnp.where(lane < N_GROUPS, lg, -jnp.inf)
    gmax = jnp.max(glog, axis=-1, keepdims=True)
    grp = first_argmax(glog, gmax)
    pg_top = 1.0 / jnp.sum(jnp.exp(glog - gmax), axis=-1, keepdims=True)
    elane = lane - N_GROUPS
    emask = (elane >= 0) & (elane < N_EXPERTS) & ((elane >> 3) == grp)
    elog = jnp.where(emask, lg, -jnp.inf)
    m1 = jnp.max(elog, axis=-1, keepdims=True)
    i1 = first_argmax(elog, m1)
    elog2 = jnp.where(lane == i1, -jnp.inf, elog)
    m2 = jnp.max(elog2, axis=-1, keepdims=True)
    i2 = first_argmax(elog2, m2)
    den = jnp.sum(jnp.exp(elog - m1), axis=-1, keepdims=True)
    p1 = 1.0 / den
    p2 = jnp.exp(m2 - m1) / den
    psum = p1 + p2
    g1 = pg_top * (p1 / psum)
    g2 = pg_top * (p2 / psum)
    e1 = i1 - N_GROUPS
    e2 = i2 - N_GROUPS

    onehot = jnp.where((elane == e1) | (elane == e2), 1.0, 0.0)
    before = jnp.dot(tri_ref[...], onehot.astype(BF16), preferred_element_type=F32) + run_ref[0:1, :]
    r1 = jnp.sum(jnp.where(elane == e1, before, 0.0), axis=-1, keepdims=True).astype(jnp.int32)
    r2 = jnp.sum(jnp.where(elane == e2, before, 0.0), axis=-1, keepdims=True).astype(jnp.int32)
    run = run_ref[...] + jnp.sum(onehot, axis=0, keepdims=True)
    run_ref[...] = run
    cnt_ref[0] = run.astype(jnp.int32)

    ri_ref[...] = jnp.where(lane == 0, e1, jnp.where(lane == 1, e2, jnp.where(lane == 2, r1, jnp.where(lane == 3, r2, 0))))
    rf_ref[...] = jnp.where(lane == 0, g1, jnp.where(lane == 1, g2, 0.0))


def _xattn_route(h1, memkv, gx, w_xq, q_gain, w_xo, gm, wr_hi, wr_lo, b_r, seq, mem_len, tm=512):
    n, d = h1.shape
    tiles_per_seq = seq // tm
    nt = n // tm
    tri = jnp.asarray(np.tril(np.ones((tm, tm), np.float32), -1), dtype=BF16)
    bd = _block_diag_ones(X_HEAD_DIM)
    return pl.pallas_call(
        _xattn_route_kernel,
        grid=(nt,),
        in_specs=[
            pl.BlockSpec((tm, d), lambda i: (i, 0)),
            _const_spec((1, d)),
            _const_spec((d, X_WIDTH)),
            _const_spec((1, X_WIDTH)),
            _const_spec((LANES, LANES)),
            pl.BlockSpec((mem_len, X_WIDTH), lambda i: (i // tiles_per_seq, 0)),
            pl.BlockSpec((mem_len, X_WIDTH), lambda i: (i // tiles_per_seq, 1)),
            _const_spec((X_WIDTH, d)),
            _const_spec((1, d)),
            _const_spec((d, LANES)),
            _const_spec((d, LANES)),
            _const_spec((1, LANES)),
            _const_spec((tm, tm)),
        ],
        out_specs=[
            pl.BlockSpec((tm, d), lambda i: (i, 0)),
            pl.BlockSpec((tm, d // 2), lambda i: (i, 0)),
            pl.BlockSpec((tm, LANES), lambda i: (i, 0)),
            pl.BlockSpec((tm, LANES), lambda i: (i, 0)),
            pl.BlockSpec((1, 8, LANES), lambda i: (i, 0, 0)),
        ],
        out_shape=[
            jax.ShapeDtypeStruct((n, d), F32),
            jax.ShapeDtypeStruct((n, d // 2), jnp.uint32),
            jax.ShapeDtypeStruct((n, LANES), jnp.int32),
            jax.ShapeDtypeStruct((n, LANES), F32),
            jax.ShapeDtypeStruct((nt, 8, LANES), jnp.int32),
        ],
        scratch_shapes=[pltpu.VMEM((8, LANES), F32)],
        compiler_params=_cparams(("arbitrary",)),
        name="xattn_route",
    )(h1, gx.reshape(1, d), w_xq, q_gain, bd, memkv, memkv, w_xo, gm.reshape(1, d), wr_hi, wr_lo, b_r, tri)


def _dispatch_kernel(tail_ref, dest_ref, x_ref, xs_ref, zero_ref, sem, zsem):
    tm = x_ref.shape[0]

    @pl.when(pl.program_id(0) == 0)
    def _():
        zero_ref[...] = jnp.zeros_like(zero_ref)

        def zero_copy(e):
            start = pl.multiple_of(tail_ref[e], DISPATCH_BLOCK)
            return pltpu.make_async_copy(zero_ref, xs_ref.at[pl.ds(start, DISPATCH_BLOCK)], zsem)

        for e in range(N_EXPERTS):
            pl.when(tail_ref[e] >= 0)(lambda e=e: zero_copy(e).start())
        for e in range(N_EXPERTS):
            pl.when(tail_ref[e] >= 0)(lambda e=e: zero_copy(e).wait())

        def unused_copy(b):
            return pltpu.make_async_copy(zero_ref, xs_ref.at[pl.ds(pl.multiple_of(b * DISPATCH_BLOCK, DISPATCH_BLOCK),
                                                                   DISPATCH_BLOCK)], zsem)

        n_blocks = xs_ref.shape[0] // DISPATCH_BLOCK
        lax.fori_loop(tail_ref[N_EXPERTS], n_blocks, lambda b, c: (unused_copy(b).start(), c)[1], 0)
        lax.fori_loop(tail_ref[N_EXPERTS], n_blocks, lambda b, c: (unused_copy(b).wait(), c)[1], 0)

    def copy(t, k):
        return pltpu.make_async_copy(x_ref.at[pl.ds(t, 1)], xs_ref.at[pl.ds(dest_ref[0, 0, 2 * t + k], 1)], sem)

    def issue(tb, carry):
        for j in range(DMA_UNROLL):
            copy(tb * DMA_UNROLL + j, 0).start()
            copy(tb * DMA_UNROLL + j, 1).start()
        return carry

    lax.fori_loop(0, tm // DMA_UNROLL, issue, 0)

    def drain(tb, carry):
        for j in range(DMA_UNROLL):
            copy(tb * DMA_UNROLL + j, 0).wait()
            copy(tb * DMA_UNROLL + j, 1).wait()
        return carry

    lax.fori_loop(0, tm // DMA_UNROLL, drain, 0)


def _dispatch(xp, dest, tail, rows, tm=512):
    n, w = xp.shape
    nt = n // tm
    grid_spec = pltpu.PrefetchScalarGridSpec(
        num_scalar_prefetch=1,
        grid=(nt,),
        in_specs=[
            pl.BlockSpec((1, 1, 2 * tm), lambda i, tail: (i, 0, 0), memory_space=pltpu.SMEM),
            pl.BlockSpec((tm, w), lambda i, tail: (i, 0)),
        ],
        out_specs=pl.BlockSpec(memory_space=pl.ANY),
        scratch_shapes=[pltpu.VMEM((DISPATCH_BLOCK, w), xp.dtype), pltpu.SemaphoreType.DMA(()),
                        pltpu.SemaphoreType.DMA(())],
    )
    return pl.pallas_call(
        _dispatch_kernel,
        grid_spec=grid_spec,
        out_shape=jax.ShapeDtypeStruct((rows, w), xp.dtype),
        compiler_params=_cparams(("arbitrary",)),
        name="dispatch",
    )(tail, dest.reshape(nt, 1, 2 * tm), xp)


def _experts_kernel(blk_e_ref, blk_x_ref, nused_ref, first_ref, next_e_ref, xs_ref, wgu_hbm, wdn_hbm, y_ref,
                    wgu_f32, wdn_f32, wgu_bf, wdn_bf, sems):
    b = pl.program_id(0)
    half = xs_ref.shape[1]

    def fetch(e):
        return (pltpu.make_async_copy(wgu_hbm.at[e], wgu_f32, sems.at[0]),
                pltpu.make_async_copy(wdn_hbm.at[e], wdn_f32, sems.at[1]))

    @pl.when(b == 0)
    def _():
        for c in fetch(blk_e_ref[0]):
            c.start()

    @pl.when(first_ref[b] == 1)
    def _():
        for c in fetch(blk_e_ref[b]):
            c.wait()
        wgu_bf[...] = wgu_f32[...].astype(BF16)
        wdn_bf[...] = wdn_f32[...].astype(BF16)

        @pl.when(next_e_ref[b] >= 0)
        def _():
            for c in fetch(next_e_ref[b]):
                c.start()

    @pl.when(b < nused_ref[0])
    def _():
        x_lo, x_hi = _unpack_bf16_pair(xs_ref[...])
        gu = (jnp.dot(x_lo.astype(BF16), wgu_bf[0:half, :], preferred_element_type=F32)
              + jnp.dot(x_hi.astype(BF16), wgu_bf[half:, :], preferred_element_type=F32))
        g = gu[:, :D_EXPERT]
        u = gu[:, D_EXPERT:]
        act = (g * jax.nn.sigmoid(g) * u).astype(BF16)
        y = jnp.dot(act, wdn_bf[...], preferred_element_type=F32)
        y_ref[...] = _pack_bf16_pair(y[:, :half], y[:, half:])

    @pl.when(b >= nused_ref[0])
    def _():
        y_ref[...] = jnp.zeros_like(y_ref)


def _experts(xs, w_gu, w_dn, blk_e, blk_x, nused, first, next_e):
    rows, half = xs.shape
    d = 2 * half
    n_blocks = rows // DISPATCH_BLOCK
    grid_spec = pltpu.PrefetchScalarGridSpec(
        num_scalar_prefetch=5,
        grid=(n_blocks,),
        in_specs=[
            pl.BlockSpec((DISPATCH_BLOCK, half), lambda b, be, bx, *_: (bx[b], 0)),
            pl.BlockSpec(memory_space=pl.ANY),
            pl.BlockSpec(memory_space=pl.ANY),
        ],
        out_specs=pl.BlockSpec((DISPATCH_BLOCK, half), lambda b, *_: (b, 0)),
        scratch_shapes=[pltpu.VMEM((d, 2 * D_EXPERT), F32), pltpu.VMEM((D_EXPERT, d), F32),
                        pltpu.VMEM((d, 2 * D_EXPERT), BF16), pltpu.VMEM((D_EXPERT, d), BF16),
                        pltpu.SemaphoreType.DMA((2,))],
    )
    return pl.pallas_call(
        _experts_kernel,
        grid_spec=grid_spec,
        out_shape=jax.ShapeDtypeStruct((rows, half), jnp.uint32),
        compiler_params=_cparams(("arbitrary",)),
        name="experts",
    )(blk_e, blk_x, nused, first, next_e, xs, w_gu, w_dn)


def _combine_kernel(dcur_ref, dnext_ref, h_ref, rf_ref, y_ref, o_ref, ya_ref, yb_ref, sems):
    i = pl.program_id(0)
    nt = pl.num_programs(0)
    tm, d = h_ref.shape
    half = d // 2
    slot = i % 2

    def issue(dref, s):
        def body(tb, carry):
            for j in range(DMA_UNROLL):
                t = tb * DMA_UNROLL + j
                pltpu.make_async_copy(y_ref.at[pl.ds(dref[0, 0, 2 * t], 1)], ya_ref.at[s, pl.ds(t, 1)], sems.at[s]).start()
                pltpu.make_async_copy(y_ref.at[pl.ds(dref[0, 0, 2 * t + 1], 1)], yb_ref.at[s, pl.ds(t, 1)], sems.at[s]).start()
            return carry

        lax.fori_loop(0, tm // DMA_UNROLL, body, 0)

    pl.when(i == 0)(lambda: issue(dcur_ref, 0))
    pl.when(i + 1 < nt)(lambda: issue(dnext_ref, 1 - slot))
    pltpu.make_async_copy(y_ref.at[pl.ds(0, tm)], ya_ref.at[slot], sems.at[slot]).wait()
    pltpu.make_async_copy(y_ref.at[pl.ds(0, tm)], yb_ref.at[slot], sems.at[slot]).wait()
    g1 = rf_ref[:, 0:1]
    g2 = rf_ref[:, 1:2]
    a_lo, a_hi = _unpack_bf16_pair(ya_ref[slot])
    b_lo, b_hi = _unpack_bf16_pair(yb_ref[slot])
    o_ref[:, :half] = h_ref[:, :half] + (a_lo * g1 + b_lo * g2)
    o_ref[:, half:] = h_ref[:, half:] + (a_hi * g1 + b_hi * g2)


def _combine(h2, rf, dest, ys, tm=256):
    n, d = h2.shape
    nt = n // tm
    half = d // 2
    dest = dest.reshape(nt, 1, 2 * tm)
    return pl.pallas_call(
        _combine_kernel,
        grid=(nt,),
        in_specs=[
            pl.BlockSpec((1, 1, 2 * tm), lambda i: (i, 0, 0), memory_space=pltpu.SMEM),
            pl.BlockSpec((1, 1, 2 * tm), lambda i: (jnp.minimum(i + 1, nt - 1), 0, 0), memory_space=pltpu.SMEM),
            pl.BlockSpec((tm, d), lambda i: (i, 0)),
            pl.BlockSpec((tm, LANES), lambda i: (i, 0)),
            pl.BlockSpec(memory_space=pl.ANY),
        ],
        out_specs=pl.BlockSpec((tm, d), lambda i: (i, 0)),
        out_shape=jax.ShapeDtypeStruct((n, d), F32),
        scratch_shapes=[pltpu.VMEM((2, tm, half), jnp.uint32), pltpu.VMEM((2, tm, half), jnp.uint32),
                        pltpu.SemaphoreType.DMA((2,))],
        compiler_params=_cparams(("arbitrary",)),
        name="combine",
    )(dest, dest, h2, rf, ys)


def kernel(x, mem, rel_bias, norm_mix_g, w_in, conv_dw_w, conv_dw_b, conv_ln_g, conv_ln_b, w_conv_out, q_norm_g, k_norm_g, attn_sinks, w_attn_out, w_mix_out, norm_x_g, norm_mem_g, w_xq, w_xkv, xq_norm_g, xk_norm_g, w_xo, norm_moe_g, w_router_group, b_router_group, w_router_expert, b_router_expert, w_expert_gu, w_expert_down):
    batch, seq, d = x.shape
    mem_len = mem.shape[1]
    n = batch * seq
    n_chunks = N_Q_HEADS // 2
    h = x.reshape(n, d)
    for l in range(norm_mix_g.shape[0]):
        c0, c1 = 2 * CONV_CH, 2 * CONV_CH + ATTN_WIDTH
        wq = w_in[l][:, c0:c1].reshape(d, N_KV_HEADS, n_chunks, HEAD_DIM).transpose(0, 2, 1, 3).reshape(d, ATTN_WIDTH)
        w = jnp.concatenate([w_in[l][:, :c0], wq, w_in[l][:, c1:]], axis=1).astype(BF16)
        q_gain = jnp.tile(q_norm_g[l].astype(F32) * HEAD_DIM ** -0.5, N_Q_HEADS)
        gain = jnp.concatenate([q_gain, jnp.tile(k_norm_g[l].astype(F32), N_KV_HEADS), jnp.ones((KV_WIDTH,), F32)])
        flag = jnp.concatenate([jnp.ones((ATTN_WIDTH + KV_WIDTH,), F32), jnp.zeros((KV_WIDTH,), F32)])
        conv_act, qkv, gates = _in_proj(h, norm_mix_g[l], w, gain, flag, conv_dw_w[l].reshape(CONV_WIDTH, CONV_CH),
                                        conv_dw_b[l], conv_ln_g[l], conv_ln_b[l], seq)
        attn_o = _swa(qkv, rel_bias, attn_sinks[l], batch, seq)
        w_ao = w_attn_out[l].reshape(N_KV_HEADS, n_chunks, HEAD_DIM, d).transpose(1, 0, 2, 3).reshape(ATTN_WIDTH, d)
        h = _mix(conv_act, attn_o, gates, h, w_conv_out[l].astype(BF16), w_ao.astype(BF16), w_mix_out[l].astype(BF16))

        kgain = jnp.concatenate([jnp.tile(xk_norm_g[l].astype(F32), X_HEADS), jnp.ones((X_WIDTH,), F32)])
        kflag = jnp.concatenate([jnp.ones((X_WIDTH,), F32), jnp.zeros((X_WIDTH,), F32)])
        memkv = _mem_proj(mem.reshape(batch * mem_len, d), norm_mem_g[l], w_xkv[l].astype(BF16), kgain, kflag)
        xq_gain = jnp.tile(xq_norm_g[l].astype(F32) * X_HEAD_DIM ** -0.5, X_HEADS).reshape(1, -1)
        w_r = jnp.concatenate([w_router_group[l], w_router_expert[l]], axis=1).astype(F32)
        w_r = jnp.pad(w_r, ((0, 0), (0, LANES - w_r.shape[1])))
        wr_hi = w_r.astype(BF16)
        wr_lo = (w_r - wr_hi.astype(F32)).astype(BF16)
        b_r = jnp.concatenate([b_router_group[l], b_router_expert[l]]).astype(F32)
        b_r = jnp.pad(b_r, (0, LANES - b_r.shape[0])).reshape(1, LANES)
        h2, xp, ri, rf, cnt = _xattn_route(h, memkv, norm_x_g[l], w_xq[l].astype(BF16), xq_gain, w_xo[l].astype(BF16),
                                           norm_moe_g[l], wr_hi, wr_lo, b_r, seq, mem_len)

        counts = cnt[-1, 0, N_GROUPS:N_GROUPS + N_EXPERTS]
        padded = (counts + DISPATCH_BLOCK - 1) // DISPATCH_BLOCK * DISPATCH_BLOCK
        pad_end = jnp.cumsum(padded)
        pad_start = pad_end - padded
        n_blocks = -(-(2 * n) // DISPATCH_BLOCK) + N_EXPERTS
        rows = n_blocks * DISPATCH_BLOCK
        nused = (pad_end[-1] // DISPATCH_BLOCK).astype(jnp.int32)
        blk = jnp.minimum(jnp.arange(n_blocks, dtype=jnp.int32), nused - 1)
        blk_e = jnp.minimum(jnp.sum(pad_end[None, :] <= (blk * DISPATCH_BLOCK)[:, None], axis=1), N_EXPERTS - 1).astype(jnp.int32)
        is_e = ri[:, 0:2, None] == jnp.arange(N_EXPERTS, dtype=jnp.int32)
        dest = (jnp.sum(jnp.where(is_e, pad_start, 0), axis=-1) + ri[:, 2:4]).astype(jnp.int32)

        tail = jnp.concatenate([jnp.where(counts > 0, pad_end - DISPATCH_BLOCK, -1), nused.reshape(1)]).astype(jnp.int32)
        xs = _dispatch(xp, dest, tail, rows)
        first = jnp.concatenate([jnp.ones((1,), jnp.int32), (blk_e[1:] != blk_e[:-1]).astype(jnp.int32)])
        eids = jnp.arange(N_EXPERTS, dtype=jnp.int32)
        later = (eids[None, :] > blk_e[:, None]) & (counts[None, :] > 0)
        next_e = jnp.min(jnp.where(later, eids[None, :], N_EXPERTS), axis=1)
        next_e = jnp.where(next_e < N_EXPERTS, next_e, -1).astype(jnp.int32)
        ys = _experts(xs, w_expert_gu[l], w_expert_down[l], blk_e, blk, nused.reshape(1), first, next_e)
        h = _combine(h2, rf, dest, ys)
    return h.reshape(batch, seq, d)
```

```python
import functools
import math

import jax
import jax.numpy as jnp
import numpy as np
from jax import lax
from jax.experimental import pallas as pl
from jax.experimental.pallas import tpu as pltpu

EPS = 1e-6
NEG = -1e30

CONV_CH = 1024
CONV_WIDTH = 31
N_Q_HEADS = 16
N_KV_HEADS = 2
HEAD_DIM = 64
ATTN_WIDTH = N_Q_HEADS * HEAD_DIM
KV_WIDTH = N_KV_HEADS * HEAD_DIM
WINDOW = 128
NUM_BUCKETS = 32
MAX_DISTANCE = 128
X_HEADS = 4
X_HEAD_DIM = 128
X_WIDTH = X_HEADS * X_HEAD_DIM
N_GROUPS = 4
EXPERTS_PER_GROUP = 8
N_EXPERTS = N_GROUPS * EXPERTS_PER_GROUP
D_EXPERT = 512
DISPATCH_BLOCK = 256

LANES = 128
SUBLANES = 8
MXU_K = 256
DMA_UNROLL = 8
CONV_HALO = 32
VMEM_LIMIT = 56 * 1024 * 1024

BF16 = jnp.bfloat16
F32 = jnp.float32


def _cparams(sem):
    return pltpu.CompilerParams(dimension_semantics=sem, vmem_limit_bytes=VMEM_LIMIT)


def _const_spec(shape):
    nd = len(shape)
    return pl.BlockSpec(shape, lambda *_: (0,) * nd, pipeline_mode=pl.Buffered(1))


def _rms_scale(x):
    return lax.rsqrt(jnp.mean(x * x, axis=-1, keepdims=True) + EPS)


def _group_sumsq(y, bd):
    sq = y * y
    hi = sq.astype(BF16)
    lo = (sq - hi.astype(F32)).astype(BF16)
    return (jnp.dot(hi, bd, preferred_element_type=F32) + jnp.dot(lo, bd, preferred_element_type=F32))


def _pack_bf16_pair(lo, hi):
    lo_bits = lax.bitcast_convert_type(lo.astype(BF16).astype(F32), jnp.uint32)
    hi_bits = lax.bitcast_convert_type(hi.astype(BF16).astype(F32), jnp.uint32)
    return (lo_bits >> 16) | hi_bits


def _unpack_bf16_pair(w):
    return (lax.bitcast_convert_type(w << 16, F32), lax.bitcast_convert_type(w & jnp.uint32(0xFFFF0000), F32))


def _head_norm_store(y, gain_ref, flag_ref, bd, o_ref, head_dim):
    for c in range(y.shape[1] // LANES):
        sl = slice(c * LANES, (c + 1) * LANES)
        yc = y[:, sl]
        ss = _group_sumsq(yc, bd)
        normed = yc * lax.rsqrt(ss * (1.0 / head_dim) + EPS) * gain_ref[:, sl]
        o_ref[:, sl] = jnp.where(flag_ref[:, sl] > 0.0, normed, yc).astype(o_ref.dtype)


def _opaque_zero(dep):
    bits = lax.shift_right_logical(lax.bitcast_convert_type(dep, jnp.uint32), jnp.uint32(32))
    return lax.bitcast_convert_type(bits, F32)[0:1, :]


def _conv_chunk(c, ext_ref, sh_ref, cv_ref, dww_ref, dwb_ref, row_chunk=128, after=None):
    tm = cv_ref.shape[0]
    off = CONV_HALO - (CONV_WIDTH - 1)
    span = tm + CONV_HALO - SUBLANES
    sl = slice(c * LANES, (c + 1) * LANES)
    for r in range(1, SUBLANES):
        sh_ref[c % 2, r - 1] = ext_ref[r:r + span, sl]
    zero = None if after is None else _opaque_zero(after)
    for rc in range(tm // row_chunk):
        acc = jnp.broadcast_to(dwb_ref[:, sl] if zero is None else dwb_ref[:, sl] + zero, (row_chunk, LANES))
        for j in range(CONV_WIDTH):
            q, r = divmod(off + j, SUBLANES)
            lo = SUBLANES * q + rc * row_chunk
            win = ext_ref[lo:lo + row_chunk, sl] if r == 0 else sh_ref[c % 2, r - 1, lo:lo + row_chunk, :]
            wj = dww_ref[j:j + 1, sl] if zero is None else dww_ref[j:j + 1, sl] + zero
            acc = acc + wj * win
        cv_ref[rc * row_chunk:(rc + 1) * row_chunk, sl] = acc


def _ln_silu(cv_ref, lng_ref, lnb_ref, act_ref):
    cv = cv_ref[...]
    mu = jnp.mean(cv, axis=-1, keepdims=True)
    cen = cv - mu
    var = jnp.mean(cen * cen, axis=-1, keepdims=True)
    ln = cen * lax.rsqrt(var + EPS) * lng_ref[...] + lnb_ref[...]
    act_ref[...] = (ln * jax.nn.sigmoid(ln)).astype(act_ref.dtype)


def _in_proj_kernel(x_ref, g_ref, w_ref, gain_ref, flag_ref, bd_ref, dww_ref, dwb_ref, lng_ref, lnb_ref,
                    act_ref, qkv_ref, gate_ref, xn_ref, ext_ref, sh_ref, cv_ref, *, tn, tiles_per_seq):
    tm = x_ref.shape[0]
    x = x_ref[...]
    xn_ref[...] = (x * _rms_scale(x) * g_ref[...]).astype(BF16)

    def proj(lo, width):
        return jnp.dot(xn_ref[...], w_ref[:, lo:lo + width], preferred_element_type=F32)

    @pl.when(pl.program_id(0) % tiles_per_seq == 0)
    def _():
        ext_ref[0:CONV_HALO, :] = jnp.zeros((CONV_HALO, CONV_CH), F32)

    n_u = act_ref.shape[1]
    n_qkv = qkv_ref.shape[1]
    g0 = 2 * n_u + n_qkv
    n_conv = CONV_CH // LANES
    n_gate = gate_ref.shape[1] // tn
    assert n_u == 2 * tn

    def proj_after(lo, after):
        xn = xn_ref[...]
        if after is not None:
            zb = jnp.broadcast_to(_opaque_zero(after), (SUBLANES, LANES)).astype(BF16)
            zb = jnp.tile(zb, (tm // SUBLANES, MXU_K // LANES))
            xn = jnp.concatenate([xn[:, :MXU_K] + zb, xn[:, MXU_K:]], axis=1)
        return jnp.dot(xn, w_ref[:, lo:lo + tn], preferred_element_type=F32)

    ext_ref[CONV_HALO:, 0:tn] = proj(0, tn) * jax.nn.sigmoid(proj(n_u, tn))
    stages = [("a", tn), ("b", n_u + tn)] + [("gate", g0 + c * tn) for c in range(n_gate)]
    last_rows = (slice(tm - SUBLANES, tm), slice(tn - LANES, tn))
    mxu_done = None
    conv_done = None
    glu_a = None
    for c, (kind, lo) in enumerate(stages):
        if c < n_conv:
            _conv_chunk(c, ext_ref, sh_ref, cv_ref, dww_ref, dwb_ref, after=mxu_done)
        y = proj_after(lo, conv_done if c < n_conv else None)
        mxu_done = y[last_rows]
        if kind == "a":
            glu_a = y
        elif kind == "b":
            ext_ref[CONV_HALO:, tn:2 * tn] = glu_a * jax.nn.sigmoid(y)
        else:
            col = lo - g0
            gate_ref[:, col:col + tn] = jax.nn.sigmoid(y).astype(gate_ref.dtype)
        if c < n_conv:
            conv_done = cv_ref[tm - SUBLANES:tm, c * LANES:(c + 1) * LANES]
    ext_ref[0:CONV_HALO, :] = ext_ref[tm:tm + CONV_HALO, :]
    _ln_silu(cv_ref, lng_ref, lnb_ref, act_ref)
    _head_norm_store(proj(2 * n_u, n_qkv), gain_ref, flag_ref, bd_ref[...], qkv_ref, HEAD_DIM)


def _in_proj(x, g, w, gain, flag, dw_w, dw_b, ln_g, ln_b, seq, tm=256, tn=512):
    m, d = x.shape
    n_qkv = ATTN_WIDTH + 2 * KV_WIDTH
    n_gate = w.shape[1] - 2 * CONV_CH - n_qkv
    assert seq % tm == 0
    row = lambda i: (i, 0)
    vec = _const_spec((1, CONV_CH))
    return pl.pallas_call(
        functools.partial(_in_proj_kernel, tn=tn, tiles_per_seq=seq // tm),
        grid=(m // tm,),
        in_specs=[pl.BlockSpec((tm, d), row), _const_spec((1, d)), _const_spec(w.shape), _const_spec((1, n_qkv)),
                  _const_spec((1, n_qkv)), _const_spec((LANES, LANES)), _const_spec((CONV_WIDTH, CONV_CH)), vec, vec, vec],
        out_specs=[pl.BlockSpec((tm, CONV_CH), row), pl.BlockSpec((tm, n_qkv), row), pl.BlockSpec((tm, n_gate), row)],
        out_shape=[jax.ShapeDtypeStruct((m, CONV_CH), BF16), jax.ShapeDtypeStruct((m, n_qkv), BF16),
                   jax.ShapeDtypeStruct((m, n_gate), BF16)],
        scratch_shapes=[pltpu.VMEM((tm, d), BF16), pltpu.VMEM((tm + CONV_HALO, CONV_CH), F32),
                        pltpu.VMEM((2, SUBLANES - 1, tm + CONV_HALO - SUBLANES, LANES), F32), pltpu.VMEM((tm, CONV_CH), F32)],
        compiler_params=_cparams(("arbitrary",)),
        name="in_proj",
    )(x, g.reshape(1, d), w, gain.reshape(1, -1), flag.reshape(1, -1), _block_diag_ones(HEAD_DIM),
      dw_w, dw_b.reshape(1, -1), ln_g.reshape(1, -1), ln_b.reshape(1, -1))


def _mem_proj_kernel(x_ref, g_ref, w_ref, gain_ref, flag_ref, bd_ref, o_ref):
    x = x_ref[...]
    xn = (x * _rms_scale(x) * g_ref[...]).astype(BF16)
    y = jnp.dot(xn, w_ref[...], preferred_element_type=F32)
    _head_norm_store(y, gain_ref, flag_ref, bd_ref[...], o_ref, X_HEAD_DIM)


def _mem_proj(x, g, w, gain, flag, tm=256):
    m, d = x.shape
    n = w.shape[1]
    return pl.pallas_call(
        _mem_proj_kernel,
        grid=(m // tm,),
        in_specs=[pl.BlockSpec((tm, d), lambda i: (i, 0)), _const_spec((1, d)), _const_spec(w.shape), _const_spec((1, n)),
                  _const_spec((1, n)), _const_spec((LANES, LANES))],
        out_specs=pl.BlockSpec((tm, n), lambda i: (i, 0)),
        out_shape=jax.ShapeDtypeStruct((m, n), BF16),
        compiler_params=_cparams(("parallel",)),
        name="mem_proj",
    )(x, g.reshape(1, d), w, gain.reshape(1, -1), flag.reshape(1, -1), _block_diag_ones(X_HEAD_DIM))


def _block_diag_ones(group):
    r = np.arange(LANES)
    return jnp.asarray((r[:, None] // group) == (r[None, :] // group), dtype=BF16)


def _swa_kernel(sink_ref, q_ref, kp_ref, kc_ref, vp_ref, vc_ref, bias_ref, o_ref, *, n_chunks):
    first = pl.program_id(1) == 0
    qb = q_ref.shape[0]
    kcat = jnp.concatenate([kp_ref[...], kc_ref[...]], axis=0)
    vcat = jnp.concatenate([vp_ref[...], vc_ref[...]], axis=0)
    lane = lax.broadcasted_iota(jnp.int32, (qb, LANES), 1)
    row = lax.broadcasted_iota(jnp.int32, (2 * qb, 1), 0)
    col = lax.broadcasted_iota(jnp.int32, (2 * qb, 2 * qb), 1)
    no_prev = jnp.where(jnp.logical_and(first, col < qb), NEG, 0.0)
    for c in range(n_chunks):
        q2 = q_ref[:, c * LANES:(c + 1) * LANES]
        zero = jnp.zeros_like(q2)
        qs = jnp.concatenate([jnp.where(lane < HEAD_DIM, q2, zero), jnp.where(lane >= HEAD_DIM, q2, zero)], axis=0)
        s = lax.dot_general(qs, kcat, (((1,), (1,)), ((), ())), preferred_element_type=F32)
        s = s + bias_ref[c] + no_prev
        sink = jnp.where(row < qb, sink_ref[c], sink_ref[n_chunks + c])
        m = jnp.maximum(jnp.max(s, axis=-1, keepdims=True), sink)
        p = jnp.exp(s - m)
        den = jnp.sum(p, axis=-1, keepdims=True) + jnp.exp(sink - m)
        o2 = jnp.dot(p.astype(BF16), vcat, preferred_element_type=F32) / den
        o = jnp.where(lane < HEAD_DIM, o2[:qb], o2[qb:])
        o_ref[:, c * LANES:(c + 1) * LANES] = o.astype(o_ref.dtype)


def _t5_bucket_np(dist):
    n = np.maximum(dist, 0)
    max_exact = NUM_BUCKETS // 2
    large = max_exact + (np.log(np.maximum(n, 1).astype(np.float32) / max_exact)
                         / math.log(MAX_DISTANCE / max_exact) * (NUM_BUCKETS - max_exact)).astype(np.int32)
    large = np.minimum(large, NUM_BUCKETS - 1)
    return np.where(n < max_exact, n, large)


def _swa(qkv, rel_bias, sinks, batch, seq):
    qb = WINDOW
    nb = seq // qb
    n_chunks = N_Q_HEADS // 2
    qi = np.arange(qb)[:, None]
    kj = np.arange(2 * qb)[None, :]
    dist = qi + qb - kj
    valid = (dist >= 0) & (dist < WINDOW)
    onehot = jnp.asarray(_t5_bucket_np(dist)[:, :, None] == np.arange(NUM_BUCKETS), dtype=F32)
    bias = jnp.einsum('qkb,bh->qkh', onehot, rel_bias.astype(F32), precision=lax.Precision.HIGHEST)
    bias = jnp.where(valid[:, :, None], bias, NEG).transpose(2, 0, 1)
    bias = jnp.concatenate([bias[:n_chunks], bias[n_chunks:]], axis=1)
    kcol = ATTN_WIDTH // LANES
    vcol = kcol + 1

    def prev(b, n):
        return (b * nb + jnp.maximum(n - 1, 0))

    return pl.pallas_call(
        functools.partial(_swa_kernel, n_chunks=n_chunks),
        grid=(batch, nb),
        in_specs=[
            pl.BlockSpec(memory_space=pltpu.SMEM),
            pl.BlockSpec((qb, ATTN_WIDTH), lambda b, n: (b * nb + n, 0)),
            pl.BlockSpec((qb, LANES), lambda b, n: (prev(b, n), kcol)),
            pl.BlockSpec((qb, LANES), lambda b, n: (b * nb + n, kcol)),
            pl.BlockSpec((qb, LANES), lambda b, n: (prev(b, n), vcol)),
            pl.BlockSpec((qb, LANES), lambda b, n: (b * nb + n, vcol)),
            _const_spec((n_chunks, 2 * qb, 2 * qb)),
        ],
        out_specs=pl.BlockSpec((qb, ATTN_WIDTH), lambda b, n: (b * nb + n, 0)),
        out_shape=jax.ShapeDtypeStruct((batch * seq, ATTN_WIDTH), BF16),
        compiler_params=_cparams(("parallel", "parallel")),
        name="swa",
    )(sinks.astype(F32), qkv, qkv, qkv, qkv, qkv, bias)


def _mix_kernel(c_ref, a_ref, g0_ref, g1_ref, x_ref, wc_ref, wa_ref, wm_ref, o_ref):
    y_conv = jnp.dot(c_ref[...], wc_ref[...], preferred_element_type=F32)
    y_attn = jnp.dot(a_ref[...], wa_ref[...], preferred_element_type=F32)
    merged = g0_ref[...].astype(F32) * y_conv + g1_ref[...].astype(F32) * y_attn
    o_ref[...] = x_ref[...] + jnp.dot(merged.astype(BF16), wm_ref[...], preferred_element_type=F32)


def _mix(conv_act, attn_o, gates, x, w_conv_out, w_attn_out, w_mix_out, tm=256):
    n, d = x.shape
    return pl.pallas_call(
        _mix_kernel,
        grid=(n // tm,),
        in_specs=[
            pl.BlockSpec((tm, CONV_CH), lambda i: (i, 0)),
            pl.BlockSpec((tm, ATTN_WIDTH), lambda i: (i, 0)),
            pl.BlockSpec((tm, d), lambda i: (i, 0)),
            pl.BlockSpec((tm, d), lambda i: (i, 1)),
            pl.BlockSpec((tm, d), lambda i: (i, 0)),
            _const_spec((CONV_CH, d)),
            _const_spec((ATTN_WIDTH, d)),
            _const_spec((d, d)),
        ],
        out_specs=pl.BlockSpec((tm, d), lambda i: (i, 0)),
        out_shape=jax.ShapeDtypeStruct((n, d), F32),
        compiler_params=_cparams(("parallel",)),
        name="mix",
    )(conv_act, attn_o, gates, gates, x, w_conv_out, w_attn_out, w_mix_out)


def _xattn_route_kernel(h_ref, gx_ref, wq_ref, qgain_ref, bd_ref, k_ref, v_ref, wo_ref, gm_ref,
                        wr_ref, br_ref, tri_ref,
                        h2_ref, xp_ref, ri_ref, rf_ref, cnt_ref, run_ref):
    tm, d = h_ref.shape
    half = d // 2

    @pl.when(pl.program_id(0) == 0)
    def _():
        run_ref[...] = jnp.zeros_like(run_ref)

    def attend(rows):
        h = h_ref[rows, :]
        xn = (h * _rms_scale(h) * gx_ref[...]).astype(BF16)
        q = jnp.dot(xn, wq_ref[...], preferred_element_type=F32)
        heads = []
        for hd in range(X_HEADS):
            sl = slice(hd * X_HEAD_DIM, (hd + 1) * X_HEAD_DIM)
            qh = q[:, sl]
            qh = (qh * lax.rsqrt(_group_sumsq(qh, bd_ref[...]) * (1.0 / X_HEAD_DIM) + EPS) * qgain_ref[:, sl]).astype(BF16)
            s = lax.dot_general(qh, k_ref[:, sl], (((1,), (1,)), ((), ())), preferred_element_type=F32)
            p = jnp.exp(s - jnp.max(s, axis=-1, keepdims=True))
            den = jnp.sum(p, axis=-1, keepdims=True)
            heads.append((jnp.dot(p.astype(BF16), v_ref[:, sl], preferred_element_type=F32) / den).astype(BF16))
        o = jnp.concatenate(heads, axis=-1)
        h2 = h + jnp.dot(o, wo_ref[...], preferred_element_type=F32)
        h2_ref[rows, :] = h2

        xn2 = h2 * _rms_scale(h2) * gm_ref[...]
        xp_ref[rows, :] = _pack_bf16_pair(xn2[:, :half], xn2[:, half:])

        x_hi = xn2.astype(BF16)
        x_lo = (xn2 - x_hi.astype(F32)).astype(BF16)
        r = jnp.dot(jnp.concatenate([x_hi, x_lo], axis=0), wr_ref[...], preferred_element_type=F32)
        n_r = xn2.shape[0]
        return (r[:n_r, :LANES] + r[:n_r, LANES:]) + (r[n_r:, :LANES] + r[n_r:, LANES:]) + br_ref[...]

    lg = attend(slice(0, tm))

    lane = lax.broadcasted_iota(jnp.int32, (tm, LANES), 1)
    lane_f = lane.astype(F32)
    big = jnp.float32(LANES)

    def first_argmax(vals, vmax):
        idx = jnp.min(jnp.where(vals == vmax, lane_f, big), axis=-1, keepdims=True)
        return idx.astype(jnp.int32)

    glog = jnp.where(lane < N_GROUPS, lg, -jnp.inf)
    gmax = jnp.max(glog, axis=-1, keepdims=True)
    grp = first_argmax(glog, gmax)
    pg_top = 1.0 / jnp.sum(jnp.exp(glog - gmax), axis=-1, keepdims=True)
    elane = lane - N_GROUPS
    emask = (elane >= 0) & (elane < N_EXPERTS) & ((elane >> 3) == grp)
    elog = jnp.where(emask, lg, -jnp.inf)
    m1 = jnp.max(elog, axis=-1, keepdims=True)
    i1 = first_argmax(elog, m1)
    elog2 = jnp.where(lane == i1, -jnp.inf, elog)
    m2 = jnp.max(elog2, axis=-1, keepdims=True)
    i2 = first_argmax(elog2, m2)
    den = jnp.sum(jnp.exp(elog - m1), axis=-1, keepdims=True)
    p1 = 1.0 / den
    p2 = jnp.exp(m2 - m1) / den
    psum = p1 + p2
    g1 = pg_top * (p1 / psum)
    g2 = pg_top * (p2 / psum)
    e1 = i1 - N_GROUPS
    e2 = i2 - N_GROUPS

    onehot = jnp.where((elane == e1) | (elane == e2), 1.0, 0.0)
    before = jnp.dot(tri_ref[...], onehot.astype(BF16), preferred_element_type=F32) + run_ref[0:1, :]
    r1 = jnp.sum(jnp.where(elane == e1, before, 0.0), axis=-1, keepdims=True).astype(jnp.int32)
    r2 = jnp.sum(jnp.where(elane == e2, before, 0.0), axis=-1, keepdims=True).astype(jnp.int32)
    run = run_ref[...] + jnp.sum(onehot, axis=0, keepdims=True)
    run_ref[...] = run
    cnt_ref[0] = run.astype(jnp.int32)

    ri_ref[...] = jnp.where(lane == 0, e1, jnp.where(lane == 1, e2, jnp.where(lane == 2, r1, jnp.where(lane == 3, r2, 0))))
    rf_ref[...] = jnp.where(lane == 0, g1, jnp.where(lane == 1, g2, 0.0))


def _xattn_route(h1, memkv, gx, w_xq, q_gain, w_xo, gm, wr, b_r, seq, mem_len, tm=512):
    n, d = h1.shape
    tiles_per_seq = seq // tm
    nt = n // tm
    tri = jnp.asarray(np.tril(np.ones((tm, tm), np.float32), -1), dtype=BF16)
    bd = _block_diag_ones(X_HEAD_DIM)
    return pl.pallas_call(
        _xattn_route_kernel,
        grid=(nt,),
        in_specs=[
            pl.BlockSpec((tm, d), lambda i: (i, 0)),
            _const_spec((1, d)),
            _const_spec((d, X_WIDTH)),
            _const_spec((1, X_WIDTH)),
            _const_spec((LANES, LANES)),
            pl.BlockSpec((mem_len, X_WIDTH), lambda i: (i // tiles_per_seq, 0)),
            pl.BlockSpec((mem_len, X_WIDTH), lambda i: (i // tiles_per_seq, 1)),
            _const_spec((X_WIDTH, d)),
            _const_spec((1, d)),
            _const_spec((d, 2 * LANES)),
            _const_spec((1, LANES)),
            _const_spec((tm, tm)),
        ],
        out_specs=[
            pl.BlockSpec((tm, d), lambda i: (i, 0)),
            pl.BlockSpec((tm, d // 2), lambda i: (i, 0)),
            pl.BlockSpec((tm, LANES), lambda i: (i, 0)),
            pl.BlockSpec((tm, LANES), lambda i: (i, 0)),
            pl.BlockSpec((1, 8, LANES), lambda i: (i, 0, 0)),
        ],
        out_shape=[
            jax.ShapeDtypeStruct((n, d), F32),
            jax.ShapeDtypeStruct((n, d // 2), jnp.uint32),
            jax.ShapeDtypeStruct((n, LANES), jnp.int32),
            jax.ShapeDtypeStruct((n, LANES), F32),
            jax.ShapeDtypeStruct((nt, 8, LANES), jnp.int32),
        ],
        scratch_shapes=[pltpu.VMEM((8, LANES), F32)],
        compiler_params=_cparams(("arbitrary",)),
        name="xattn_route",
    )(h1, gx.reshape(1, d), w_xq, q_gain, bd, memkv, memkv, w_xo, gm.reshape(1, d), wr, b_r, tri)


def _dispatch_kernel(tail_ref, dest_ref, x_ref, xs_ref, zero_ref, sem, zsem):
    tm = x_ref.shape[0]

    @pl.when(pl.program_id(0) == 0)
    def _():
        zero_ref[...] = jnp.zeros_like(zero_ref)

        def zero_copy(e):
            start = pl.multiple_of(tail_ref[e], DISPATCH_BLOCK)
            return pltpu.make_async_copy(zero_ref, xs_ref.at[pl.ds(start, DISPATCH_BLOCK)], zsem)

        for e in range(N_EXPERTS):
            pl.when(tail_ref[e] >= 0)(lambda e=e: zero_copy(e).start())
        for e in range(N_EXPERTS):
            pl.when(tail_ref[e] >= 0)(lambda e=e: zero_copy(e).wait())

        def unused_copy(b):
            return pltpu.make_async_copy(zero_ref, xs_ref.at[pl.ds(pl.multiple_of(b * DISPATCH_BLOCK, DISPATCH_BLOCK),
                                                                   DISPATCH_BLOCK)], zsem)

        n_blocks = xs_ref.shape[0] // DISPATCH_BLOCK
        lax.fori_loop(tail_ref[N_EXPERTS], n_blocks, lambda b, c: (unused_copy(b).start(), c)[1], 0)
        lax.fori_loop(tail_ref[N_EXPERTS], n_blocks, lambda b, c: (unused_copy(b).wait(), c)[1], 0)

    def copy(t, k):
        return pltpu.make_async_copy(x_ref.at[pl.ds(t, 1)], xs_ref.at[pl.ds(dest_ref[0, 0, 2 * t + k], 1)], sem)

    def issue(tb, carry):
        for j in range(DMA_UNROLL):
            copy(tb * DMA_UNROLL + j, 0).start()
            copy(tb * DMA_UNROLL + j, 1).start()
        return carry

    lax.fori_loop(0, tm // DMA_UNROLL, issue, 0)

    def drain(tb, carry):
        for j in range(DMA_UNROLL):
            copy(tb * DMA_UNROLL + j, 0).wait()
            copy(tb * DMA_UNROLL + j, 1).wait()
        return carry

    lax.fori_loop(0, tm // DMA_UNROLL, drain, 0)


def _dispatch(xp, dest, tail, rows, tm=512):
    n, w = xp.shape
    nt = n // tm
    grid_spec = pltpu.PrefetchScalarGridSpec(
        num_scalar_prefetch=1,
        grid=(nt,),
        in_specs=[
            pl.BlockSpec((1, 1, 2 * tm), lambda i, tail: (i, 0, 0), memory_space=pltpu.SMEM),
            pl.BlockSpec((tm, w), lambda i, tail: (i, 0)),
        ],
        out_specs=pl.BlockSpec(memory_space=pl.ANY),
        scratch_shapes=[pltpu.VMEM((DISPATCH_BLOCK, w), xp.dtype), pltpu.SemaphoreType.DMA(()),
                        pltpu.SemaphoreType.DMA(())],
    )
    return pl.pallas_call(
        _dispatch_kernel,
        grid_spec=grid_spec,
        out_shape=jax.ShapeDtypeStruct((rows, w), xp.dtype),
        compiler_params=_cparams(("arbitrary",)),
        name="dispatch",
    )(tail, dest.reshape(nt, 1, 2 * tm), xp)


def _experts_kernel(blk_e_ref, blk_x_ref, nused_ref, first_ref, next_e_ref, xs_ref, wgu_hbm, wdn_hbm, y_ref,
                    wgu_f32, wdn_f32, wgu_bf, wdn_bf, sems):
    b = pl.program_id(0)
    half = xs_ref.shape[1]

    def fetch(e):
        return (pltpu.make_async_copy(wgu_hbm.at[e], wgu_f32, sems.at[0]),
                pltpu.make_async_copy(wdn_hbm.at[e], wdn_f32, sems.at[1]))

    @pl.when(b == 0)
    def _():
        for c in fetch(blk_e_ref[0]):
            c.start()

    @pl.when(first_ref[b] == 1)
    def _():
        for c in fetch(blk_e_ref[b]):
            c.wait()
        wgu_bf[...] = wgu_f32[...].astype(BF16)
        wdn_bf[...] = wdn_f32[...].astype(BF16)

        @pl.when(next_e_ref[b] >= 0)
        def _():
            for c in fetch(next_e_ref[b]):
                c.start()

    @pl.when(b < nused_ref[0])
    def _():
        x_lo, x_hi = _unpack_bf16_pair(xs_ref[...])
        gu = (jnp.dot(x_lo.astype(BF16), wgu_bf[0:half, :], preferred_element_type=F32)
              + jnp.dot(x_hi.astype(BF16), wgu_bf[half:, :], preferred_element_type=F32))
        g = gu[:, :D_EXPERT]
        u = gu[:, D_EXPERT:]
        act = (g * jax.nn.sigmoid(g) * u).astype(BF16)
        y = jnp.dot(act, wdn_bf[...], preferred_element_type=F32)
        y_ref[...] = _pack_bf16_pair(y[:, :half], y[:, half:])

    @pl.when(b >= nused_ref[0])
    def _():
        y_ref[...] = jnp.zeros_like(y_ref)


def _experts(xs, w_gu, w_dn, blk_e, blk_x, nused, first, next_e):
    rows, half = xs.shape
    d = 2 * half
    n_blocks = rows // DISPATCH_BLOCK
    grid_spec = pltpu.PrefetchScalarGridSpec(
        num_scalar_prefetch=5,
        grid=(n_blocks,),
        in_specs=[
            pl.BlockSpec((DISPATCH_BLOCK, half), lambda b, be, bx, *_: (bx[b], 0)),
            pl.BlockSpec(memory_space=pl.ANY),
            pl.BlockSpec(memory_space=pl.ANY),
        ],
        out_specs=pl.BlockSpec((DISPATCH_BLOCK, half), lambda b, *_: (b, 0)),
        scratch_shapes=[pltpu.VMEM((d, 2 * D_EXPERT), F32), pltpu.VMEM((D_EXPERT, d), F32),
                        pltpu.VMEM((d, 2 * D_EXPERT), BF16), pltpu.VMEM((D_EXPERT, d), BF16),
                        pltpu.SemaphoreType.DMA((2,))],
    )
    return pl.pallas_call(
        _experts_kernel,
        grid_spec=grid_spec,
        out_shape=jax.ShapeDtypeStruct((rows, half), jnp.uint32),
        compiler_params=_cparams(("arbitrary",)),
        name="experts",
    )(blk_e, blk_x, nused, first, next_e, xs, w_gu, w_dn)


def _combine_kernel(dcur_ref, dnext_ref, h_ref, rf_ref, y_ref, o_ref, ya_ref, yb_ref, sems):
    i = pl.program_id(0)
    nt = pl.num_programs(0)
    tm, d = h_ref.shape
    half = d // 2
    slot = i % 2

    def issue(dref, s):
        def body(tb, carry):
            for j in range(DMA_UNROLL):
                t = tb * DMA_UNROLL + j
                pltpu.make_async_copy(y_ref.at[pl.ds(dref[0, 0, 2 * t], 1)], ya_ref.at[s, pl.ds(t, 1)], sems.at[s]).start()
                pltpu.make_async_copy(y_ref.at[pl.ds(dref[0, 0, 2 * t + 1], 1)], yb_ref.at[s, pl.ds(t, 1)], sems.at[s]).start()
            return carry

        lax.fori_loop(0, tm // DMA_UNROLL, body, 0)

    pl.when(i == 0)(lambda: issue(dcur_ref, 0))
    pl.when(i + 1 < nt)(lambda: issue(dnext_ref, 1 - slot))
    pltpu.make_async_copy(y_ref.at[pl.ds(0, tm)], ya_ref.at[slot], sems.at[slot]).wait()
    pltpu.make_async_copy(y_ref.at[pl.ds(0, tm)], yb_ref.at[slot], sems.at[slot]).wait()
    g1 = rf_ref[:, 0:1]
    g2 = rf_ref[:, 1:2]
    a_lo, a_hi = _unpack_bf16_pair(ya_ref[slot])
    b_lo, b_hi = _unpack_bf16_pair(yb_ref[slot])
    o_ref[:, :half] = h_ref[:, :half] + (a_lo * g1 + b_lo * g2)
    o_ref[:, half:] = h_ref[:, half:] + (a_hi * g1 + b_hi * g2)


def _combine(h2, rf, dest, ys, tm=256):
    n, d = h2.shape
    nt = n // tm
    half = d // 2
    dest = dest.reshape(nt, 1, 2 * tm)
    return pl.pallas_call(
        _combine_kernel,
        grid=(nt,),
        in_specs=[
            pl.BlockSpec((1, 1, 2 * tm), lambda i: (i, 0, 0), memory_space=pltpu.SMEM),
            pl.BlockSpec((1, 1, 2 * tm), lambda i: (jnp.minimum(i + 1, nt - 1), 0, 0), memory_space=pltpu.SMEM),
            pl.BlockSpec((tm, d), lambda i: (i, 0)),
            pl.BlockSpec((tm, LANES), lambda i: (i, 0)),
            pl.BlockSpec(memory_space=pl.ANY),
        ],
        out_specs=pl.BlockSpec((tm, d), lambda i: (i, 0)),
        out_shape=jax.ShapeDtypeStruct((n, d), F32),
        scratch_shapes=[pltpu.VMEM((2, tm, half), jnp.uint32), pltpu.VMEM((2, tm, half), jnp.uint32),
                        pltpu.SemaphoreType.DMA((2,))],
        compiler_params=_cparams(("arbitrary",)),
        name="combine",
    )(dest, dest, h2, rf, ys)


def kernel(x, mem, rel_bias, norm_mix_g, w_in, conv_dw_w, conv_dw_b, conv_ln_g, conv_ln_b, w_conv_out, q_norm_g, k_norm_g, attn_sinks, w_attn_out, w_mix_out, norm_x_g, norm_mem_g, w_xq, w_xkv, xq_norm_g, xk_norm_g, w_xo, norm_moe_g, w_router_group, b_router_group, w_router_expert, b_router_expert, w_expert_gu, w_expert_down):
    batch, seq, d = x.shape
    mem_len = mem.shape[1]
    n = batch * seq
    n_chunks = N_Q_HEADS // 2
    h = x.reshape(n, d)
    for l in range(norm_mix_g.shape[0]):
        c0, c1 = 2 * CONV_CH, 2 * CONV_CH + ATTN_WIDTH
        wq = w_in[l][:, c0:c1].reshape(d, N_KV_HEADS, n_chunks, HEAD_DIM).transpose(0, 2, 1, 3).reshape(d, ATTN_WIDTH)
        w = jnp.concatenate([w_in[l][:, :c0], wq, w_in[l][:, c1:]], axis=1).astype(BF16)
        q_gain = jnp.tile(q_norm_g[l].astype(F32) * HEAD_DIM ** -0.5, N_Q_HEADS)
        gain = jnp.concatenate([q_gain, jnp.tile(k_norm_g[l].astype(F32), N_KV_HEADS), jnp.ones((KV_WIDTH,), F32)])
        flag = jnp.concatenate([jnp.ones((ATTN_WIDTH + KV_WIDTH,), F32), jnp.zeros((KV_WIDTH,), F32)])
        conv_act, qkv, gates = _in_proj(h, norm_mix_g[l], w, gain, flag, conv_dw_w[l].reshape(CONV_WIDTH, CONV_CH),
                                        conv_dw_b[l], conv_ln_g[l], conv_ln_b[l], seq)
        attn_o = _swa(qkv, rel_bias, attn_sinks[l], batch, seq)
        w_ao = w_attn_out[l].reshape(N_KV_HEADS, n_chunks, HEAD_DIM, d).transpose(1, 0, 2, 3).reshape(ATTN_WIDTH, d)
        h = _mix(conv_act, attn_o, gates, h, w_conv_out[l].astype(BF16), w_ao.astype(BF16), w_mix_out[l].astype(BF16))

        kgain = jnp.concatenate([jnp.tile(xk_norm_g[l].astype(F32), X_HEADS), jnp.ones((X_WIDTH,), F32)])
        kflag = jnp.concatenate([jnp.ones((X_WIDTH,), F32), jnp.zeros((X_WIDTH,), F32)])
        memkv = _mem_proj(mem.reshape(batch * mem_len, d), norm_mem_g[l], w_xkv[l].astype(BF16), kgain, kflag)
        xq_gain = jnp.tile(xq_norm_g[l].astype(F32) * X_HEAD_DIM ** -0.5, X_HEADS).reshape(1, -1)
        w_r = jnp.concatenate([w_router_group[l], w_router_expert[l]], axis=1).astype(F32)
        w_r = jnp.pad(w_r, ((0, 0), (0, LANES - w_r.shape[1])))
        wr_hi = w_r.astype(BF16)
        wr = jnp.concatenate([wr_hi, (w_r - wr_hi.astype(F32)).astype(BF16)], axis=1)
        b_r = jnp.concatenate([b_router_group[l], b_router_expert[l]]).astype(F32)
        b_r = jnp.pad(b_r, (0, LANES - b_r.shape[0])).reshape(1, LANES)
        h2, xp, ri, rf, cnt = _xattn_route(h, memkv, norm_x_g[l], w_xq[l].astype(BF16), xq_gain, w_xo[l].astype(BF16),
                                           norm_moe_g[l], wr, b_r, seq, mem_len)

        counts = cnt[-1, 0, N_GROUPS:N_GROUPS + N_EXPERTS]
        padded = (counts + DISPATCH_BLOCK - 1) // DISPATCH_BLOCK * DISPATCH_BLOCK
        pad_end = jnp.cumsum(padded)
        pad_start = pad_end - padded
        n_blocks = -(-(2 * n) // DISPATCH_BLOCK) + N_EXPERTS
        rows = n_blocks * DISPATCH_BLOCK
        nused = (pad_end[-1] // DISPATCH_BLOCK).astype(jnp.int32)
        blk = jnp.minimum(jnp.arange(n_blocks, dtype=jnp.int32), nused - 1)
        blk_e = jnp.minimum(jnp.sum(pad_end[None, :] <= (blk * DISPATCH_BLOCK)[:, None], axis=1), N_EXPERTS - 1).astype(jnp.int32)
        is_e = ri[:, 0:2, None] == jnp.arange(N_EXPERTS, dtype=jnp.int32)
        dest = (jnp.sum(jnp.where(is_e, pad_start, 0), axis=-1) + ri[:, 2:4]).astype(jnp.int32)

        tail = jnp.concatenate([jnp.where(counts > 0, pad_end - DISPATCH_BLOCK, -1), nused.reshape(1)]).astype(jnp.int32)
        xs = _dispatch(xp, dest, tail, rows)
        first = jnp.concatenate([jnp.ones((1,), jnp.int32), (blk_e[1:] != blk_e[:-1]).astype(jnp.int32)])
        eids = jnp.arange(N_EXPERTS, dtype=jnp.int32)
        later = (eids[None, :] > blk_e[:, None]) & (counts[None, :] > 0)
        next_e = jnp.min(jnp.where(later, eids[None, :], N_EXPERTS), axis=1)
        next_e = jnp.where(next_e < N_EXPERTS, next_e, -1).astype(jnp.int32)
        ys = _experts(xs, w_expert_gu[l], w_expert_down[l], blk_e, blk, nused.reshape(1), first, next_e)
        h = _combine(h2, rf, dest, ys)
    return h.reshape(batch, seq, d)
```

```python
import functools
import math

import jax
import jax.numpy as jnp
import numpy as np
from jax import lax
from jax.experimental import pallas as pl
from jax.experimental.pallas import tpu as pltpu

EPS = 1e-6
NEG = -1e30

CONV_CH = 1024
CONV_WIDTH = 31
N_Q_HEADS = 16
N_KV_HEADS = 2
HEAD_DIM = 64
ATTN_WIDTH = N_Q_HEADS * HEAD_DIM
KV_WIDTH = N_KV_HEADS * HEAD_DIM
WINDOW = 128
NUM_BUCKETS = 32
MAX_DISTANCE = 128
X_HEADS = 4
X_HEAD_DIM = 128
X_WIDTH = X_HEADS * X_HEAD_DIM
N_GROUPS = 4
EXPERTS_PER_GROUP = 8
N_EXPERTS = N_GROUPS * EXPERTS_PER_GROUP
D_EXPERT = 512
DISPATCH_BLOCK = 256

LANES = 128
SUBLANES = 8
MXU_K = 256
DMA_UNROLL = 8
CONV_HALO = 32
VMEM_LIMIT = 56 * 1024 * 1024

BF16 = jnp.bfloat16
F32 = jnp.float32


def _cparams(sem):
    return pltpu.CompilerParams(dimension_semantics=sem, vmem_limit_bytes=VMEM_LIMIT)


def _const_spec(shape):
    nd = len(shape)
    return pl.BlockSpec(shape, lambda *_: (0,) * nd, pipeline_mode=pl.Buffered(1))


def _rms_scale(x):
    return lax.rsqrt(jnp.mean(x * x, axis=-1, keepdims=True) + EPS)


def _group_sumsq(y, bd):
    sq = y * y
    hi = sq.astype(BF16)
    lo = (sq - hi.astype(F32)).astype(BF16)
    return (jnp.dot(hi, bd, preferred_element_type=F32) + jnp.dot(lo, bd, preferred_element_type=F32))


def _pack_bf16_pair(lo, hi):
    lo_bits = lax.bitcast_convert_type(lo.astype(BF16).astype(F32), jnp.uint32)
    hi_bits = lax.bitcast_convert_type(hi.astype(BF16).astype(F32), jnp.uint32)
    return (lo_bits >> 16) | hi_bits


def _unpack_bf16_pair(w):
    return (lax.bitcast_convert_type(w << 16, F32), lax.bitcast_convert_type(w & jnp.uint32(0xFFFF0000), F32))


def _head_norm_store(y, gain_ref, flag_ref, bd, o_ref, head_dim):
    for c in range(y.shape[1] // LANES):
        sl = slice(c * LANES, (c + 1) * LANES)
        yc = y[:, sl]
        ss = _group_sumsq(yc, bd)
        normed = yc * lax.rsqrt(ss * (1.0 / head_dim) + EPS) * gain_ref[:, sl]
        o_ref[:, sl] = jnp.where(flag_ref[:, sl] > 0.0, normed, yc).astype(o_ref.dtype)


def _opaque_zero(dep):
    bits = lax.shift_right_logical(lax.bitcast_convert_type(dep, jnp.uint32), jnp.uint32(32))
    return lax.bitcast_convert_type(bits, F32)[0:1, :]


def _conv_chunk(c, ext_ref, sh_ref, cv_ref, dww_ref, dwb_ref, row_chunk=128, after=None):
    tm = cv_ref.shape[0]
    off = CONV_HALO - (CONV_WIDTH - 1)
    span = tm + CONV_HALO - SUBLANES
    sl = slice(c * LANES, (c + 1) * LANES)
    for r in range(1, SUBLANES):
        sh_ref[c % 2, r - 1] = ext_ref[r:r + span, sl]
    zero = None if after is None else _opaque_zero(after)
    for rc in range(tm // row_chunk):
        acc = jnp.broadcast_to(dwb_ref[:, sl] if zero is None else dwb_ref[:, sl] + zero, (row_chunk, LANES))
        for j in range(CONV_WIDTH):
            q, r = divmod(off + j, SUBLANES)
            lo = SUBLANES * q + rc * row_chunk
            win = ext_ref[lo:lo + row_chunk, sl] if r == 0 else sh_ref[c % 2, r - 1, lo:lo + row_chunk, :]
            wj = dww_ref[j:j + 1, sl] if zero is None else dww_ref[j:j + 1, sl] + zero
            acc = acc + wj * win
        cv_ref[rc * row_chunk:(rc + 1) * row_chunk, sl] = acc


def _ln_silu(cv_ref, lng_ref, lnb_ref, act_ref):
    cv = cv_ref[...]
    mu = jnp.mean(cv, axis=-1, keepdims=True)
    cen = cv - mu
    var = jnp.mean(cen * cen, axis=-1, keepdims=True)
    ln = cen * lax.rsqrt(var + EPS) * lng_ref[...] + lnb_ref[...]
    act_ref[...] = (ln * jax.nn.sigmoid(ln)).astype(act_ref.dtype)


def _in_proj_kernel(x_ref, g_ref, w_ref, gain_ref, flag_ref, bd_ref, dww_ref, dwb_ref, lng_ref, lnb_ref,
                    act_ref, qkv_ref, gate_ref, xn_ref, ext_ref, sh_ref, cv_ref, *, tn, tiles_per_seq):
    tm = x_ref.shape[0]
    x = x_ref[...]
    xn_ref[...] = (x * _rms_scale(x) * g_ref[...]).astype(BF16)

    def proj(lo, width):
        return jnp.dot(xn_ref[...], w_ref[:, lo:lo + width], preferred_element_type=F32)

    @pl.when(pl.program_id(0) % tiles_per_seq == 0)
    def _():
        ext_ref[0:CONV_HALO, :] = jnp.zeros((CONV_HALO, CONV_CH), F32)

    n_u = act_ref.shape[1]
    n_qkv = qkv_ref.shape[1]
    g0 = 2 * n_u + n_qkv
    n_conv = CONV_CH // LANES
    n_gate = gate_ref.shape[1] // tn
    assert n_u == 2 * tn

    def proj_after(lo, after):
        xn = xn_ref[...]
        if after is not None:
            zb = jnp.broadcast_to(_opaque_zero(after), (SUBLANES, LANES)).astype(BF16)
            zb = jnp.tile(zb, (tm // SUBLANES, MXU_K // LANES))
            xn = jnp.concatenate([xn[:, :MXU_K] + zb, xn[:, MXU_K:]], axis=1)
        return jnp.dot(xn, w_ref[:, lo:lo + tn], preferred_element_type=F32)

    ext_ref[CONV_HALO:, 0:tn] = proj(0, tn) * jax.nn.sigmoid(proj(n_u, tn))
    stages = [("a", tn), ("b", n_u + tn)] + [("gate", g0 + c * tn) for c in range(n_gate)]
    last_rows = (slice(tm - SUBLANES, tm), slice(tn - LANES, tn))
    mxu_done = None
    conv_done = None
    glu_a = None
    for c, (kind, lo) in enumerate(stages):
        if c < n_conv:
            _conv_chunk(c, ext_ref, sh_ref, cv_ref, dww_ref, dwb_ref, after=mxu_done)
        y = proj_after(lo, conv_done if c < n_conv else None)
        mxu_done = y[last_rows]
        if kind == "a":
            glu_a = y
        elif kind == "b":
            ext_ref[CONV_HALO:, tn:2 * tn] = glu_a * jax.nn.sigmoid(y)
        else:
            col = lo - g0
            gate_ref[:, col:col + tn] = jax.nn.sigmoid(y).astype(gate_ref.dtype)
        if c < n_conv:
            conv_done = cv_ref[tm - SUBLANES:tm, c * LANES:(c + 1) * LANES]
    ext_ref[0:CONV_HALO, :] = ext_ref[tm:tm + CONV_HALO, :]
    _ln_silu(cv_ref, lng_ref, lnb_ref, act_ref)
    _head_norm_store(proj(2 * n_u, n_qkv), gain_ref, flag_ref, bd_ref[...], qkv_ref, HEAD_DIM)


def _in_proj(x, g, w, gain, flag, dw_w, dw_b, ln_g, ln_b, seq, tm=256, tn=512):
    m, d = x.shape
    n_qkv = ATTN_WIDTH + 2 * KV_WIDTH
    n_gate = w.shape[1] - 2 * CONV_CH - n_qkv
    assert seq % tm == 0
    row = lambda i: (i, 0)
    vec = _const_spec((1, CONV_CH))
    return pl.pallas_call(
        functools.partial(_in_proj_kernel, tn=tn, tiles_per_seq=seq // tm),
        grid=(m // tm,),
        in_specs=[pl.BlockSpec((tm, d), row), _const_spec((1, d)), _const_spec(w.shape), _const_spec((1, n_qkv)),
                  _const_spec((1, n_qkv)), _const_spec((LANES, LANES)), _const_spec((CONV_WIDTH, CONV_CH)), vec, vec, vec],
        out_specs=[pl.BlockSpec((tm, CONV_CH), row), pl.BlockSpec((tm, n_qkv), row), pl.BlockSpec((tm, n_gate), row)],
        out_shape=[jax.ShapeDtypeStruct((m, CONV_CH), BF16), jax.ShapeDtypeStruct((m, n_qkv), BF16),
                   jax.ShapeDtypeStruct((m, n_gate), BF16)],
        scratch_shapes=[pltpu.VMEM((tm, d), BF16), pltpu.VMEM((tm + CONV_HALO, CONV_CH), F32),
                        pltpu.VMEM((2, SUBLANES - 1, tm + CONV_HALO - SUBLANES, LANES), F32), pltpu.VMEM((tm, CONV_CH), F32)],
        compiler_params=_cparams(("arbitrary",)),
        name="in_proj",
    )(x, g.reshape(1, d), w, gain.reshape(1, -1), flag.reshape(1, -1), _block_diag_ones(HEAD_DIM),
      dw_w, dw_b.reshape(1, -1), ln_g.reshape(1, -1), ln_b.reshape(1, -1))


def _mem_proj_kernel(x_ref, g_ref, w_ref, gain_ref, flag_ref, bd_ref, o_ref):
    x = x_ref[...]
    xn = (x * _rms_scale(x) * g_ref[...]).astype(BF16)
    y = jnp.dot(xn, w_ref[...], preferred_element_type=F32)
    _head_norm_store(y, gain_ref, flag_ref, bd_ref[...], o_ref, X_HEAD_DIM)


def _mem_proj(x, g, w, gain, flag, tm=256):
    m, d = x.shape
    n = w.shape[1]
    return pl.pallas_call(
        _mem_proj_kernel,
        grid=(m // tm,),
        in_specs=[pl.BlockSpec((tm, d), lambda i: (i, 0)), _const_spec((1, d)), _const_spec(w.shape), _const_spec((1, n)),
                  _const_spec((1, n)), _const_spec((LANES, LANES))],
        out_specs=pl.BlockSpec((tm, n), lambda i: (i, 0)),
        out_shape=jax.ShapeDtypeStruct((m, n), BF16),
        compiler_params=_cparams(("parallel",)),
        name="mem_proj",
    )(x, g.reshape(1, d), w, gain.reshape(1, -1), flag.reshape(1, -1), _block_diag_ones(X_HEAD_DIM))


def _block_diag_ones(group):
    r = np.arange(LANES)
    return jnp.asarray((r[:, None] // group) == (r[None, :] // group), dtype=BF16)


def _swa_kernel(sink_ref, q_ref, kp_ref, kc_ref, vp_ref, vc_ref, bias_ref, o_ref, *, n_chunks):
    first = pl.program_id(1) == 0
    qb = q_ref.shape[0]
    lane = lax.broadcasted_iota(jnp.int32, (qb, LANES), 1)
    lane2 = lax.broadcasted_iota(jnp.int32, (2 * qb, LANES), 1)
    row = lax.broadcasted_iota(jnp.int32, (2 * qb, 1), 0)
    col = lax.broadcasted_iota(jnp.int32, (2 * qb, 2 * qb), 1)

    def both_halves(prev_ref, cur_ref):
        t = jnp.concatenate([prev_ref[...], cur_ref[...]], axis=0).astype(F32)
        swapped = pltpu.roll(t, HEAD_DIM, 1)
        return (jnp.where(lane2 < HEAD_DIM, t, swapped).astype(BF16), jnp.where(lane2 < HEAD_DIM, swapped, t).astype(BF16))

    ks = both_halves(kp_ref, kc_ref)
    vs = both_halves(vp_ref, vc_ref)
    no_prev = jnp.where(jnp.logical_and(first, col < qb), NEG, 0.0)
    chunks_per_kv = n_chunks // N_KV_HEADS
    for c in range(n_chunks):
        q2 = q_ref[:, c * LANES:(c + 1) * LANES]
        zero = jnp.zeros_like(q2)
        qs = jnp.concatenate([jnp.where(lane < HEAD_DIM, q2, zero), jnp.where(lane >= HEAD_DIM, q2, zero)], axis=0)
        s = lax.dot_general(qs, ks[c // chunks_per_kv], (((1,), (1,)), ((), ())), preferred_element_type=F32)
        s = s + bias_ref[c] + no_prev
        sink = jnp.where(row < qb, sink_ref[2 * c], sink_ref[2 * c + 1])
        m = jnp.maximum(jnp.max(s, axis=-1, keepdims=True), sink)
        p = jnp.exp(s - m)
        den = jnp.sum(p, axis=-1, keepdims=True) + jnp.exp(sink - m)
        o2 = jnp.dot(p.astype(BF16), vs[c // chunks_per_kv], preferred_element_type=F32) / den
        o = jnp.where(lane < HEAD_DIM, o2[:qb], o2[qb:])
        o_ref[:, c * LANES:(c + 1) * LANES] = o.astype(o_ref.dtype)


def _t5_bucket_np(dist):
    n = np.maximum(dist, 0)
    max_exact = NUM_BUCKETS // 2
    large = max_exact + (np.log(np.maximum(n, 1).astype(np.float32) / max_exact)
                         / math.log(MAX_DISTANCE / max_exact) * (NUM_BUCKETS - max_exact)).astype(np.int32)
    large = np.minimum(large, NUM_BUCKETS - 1)
    return np.where(n < max_exact, n, large)


def _swa(qkv, rel_bias, sinks, batch, seq):
    qb = WINDOW
    nb = seq // qb
    n_chunks = N_Q_HEADS // 2
    qi = np.arange(qb)[:, None]
    kj = np.arange(2 * qb)[None, :]
    dist = qi + qb - kj
    valid = (dist >= 0) & (dist < WINDOW)
    onehot = jnp.asarray(_t5_bucket_np(dist)[:, :, None] == np.arange(NUM_BUCKETS), dtype=F32)
    bias = jnp.einsum('qkb,bh->qkh', onehot, rel_bias.astype(F32), precision=lax.Precision.HIGHEST)
    bias = jnp.where(valid[:, :, None], bias, NEG).transpose(2, 0, 1)
    bias = bias.reshape(n_chunks, 2 * qb, 2 * qb)
    kcol = ATTN_WIDTH // LANES
    vcol = kcol + 1

    def prev(b, n):
        return (b * nb + jnp.maximum(n - 1, 0))

    return pl.pallas_call(
        functools.partial(_swa_kernel, n_chunks=n_chunks),
        grid=(batch, nb),
        in_specs=[
            pl.BlockSpec(memory_space=pltpu.SMEM),
            pl.BlockSpec((qb, ATTN_WIDTH), lambda b, n: (b * nb + n, 0)),
            pl.BlockSpec((qb, LANES), lambda b, n: (prev(b, n), kcol)),
            pl.BlockSpec((qb, LANES), lambda b, n: (b * nb + n, kcol)),
            pl.BlockSpec((qb, LANES), lambda b, n: (prev(b, n), vcol)),
            pl.BlockSpec((qb, LANES), lambda b, n: (b * nb + n, vcol)),
            _const_spec((n_chunks, 2 * qb, 2 * qb)),
        ],
        out_specs=pl.BlockSpec((qb, ATTN_WIDTH), lambda b, n: (b * nb + n, 0)),
        out_shape=jax.ShapeDtypeStruct((batch * seq, ATTN_WIDTH), BF16),
        compiler_params=_cparams(("parallel", "parallel")),
        name="swa",
    )(sinks.astype(F32), qkv, qkv, qkv, qkv, qkv, bias)


def _mix_kernel(c_ref, a_ref, g0_ref, g1_ref, x_ref, wc_ref, wa_ref, wm_ref, o_ref):
    y_conv = jnp.dot(c_ref[...], wc_ref[...], preferred_element_type=F32)
    y_attn = jnp.dot(a_ref[...], wa_ref[...], preferred_element_type=F32)
    merged = g0_ref[...].astype(F32) * y_conv + g1_ref[...].astype(F32) * y_attn
    o_ref[...] = x_ref[...] + jnp.dot(merged.astype(BF16), wm_ref[...], preferred_element_type=F32)


def _mix(conv_act, attn_o, gates, x, w_conv_out, w_attn_out, w_mix_out, tm=256):
    n, d = x.shape
    return pl.pallas_call(
        _mix_kernel,
        grid=(n // tm,),
        in_specs=[
            pl.BlockSpec((tm, CONV_CH), lambda i: (i, 0)),
            pl.BlockSpec((tm, ATTN_WIDTH), lambda i: (i, 0)),
            pl.BlockSpec((tm, d), lambda i: (i, 0)),
            pl.BlockSpec((tm, d), lambda i: (i, 1)),
            pl.BlockSpec((tm, d), lambda i: (i, 0)),
            _const_spec((CONV_CH, d)),
            _const_spec((ATTN_WIDTH, d)),
            _const_spec((d, d)),
        ],
        out_specs=pl.BlockSpec((tm, d), lambda i: (i, 0)),
        out_shape=jax.ShapeDtypeStruct((n, d), F32),
        compiler_params=_cparams(("parallel",)),
        name="mix",
    )(conv_act, attn_o, gates, gates, x, w_conv_out, w_attn_out, w_mix_out)


def _xattn_route_kernel(h_ref, gx_ref, wq_ref, qgain_ref, bd_ref, k_ref, v_ref, wo_ref, gm_ref,
                        wr_ref, br_ref, tri_ref,
                        h2_ref, xp_ref, ri_ref, rf_ref, cnt_ref, run_ref):
    tm, d = h_ref.shape
    half = d // 2

    @pl.when(pl.program_id(0) == 0)
    def _():
        run_ref[...] = jnp.zeros_like(run_ref)

    def attend(rows):
        h = h_ref[rows, :]
        xn = (h * _rms_scale(h) * gx_ref[...]).astype(BF16)
        q = jnp.dot(xn, wq_ref[...], preferred_element_type=F32)
        heads = []
        for hd in range(X_HEADS):
            sl = slice(hd * X_HEAD_DIM, (hd + 1) * X_HEAD_DIM)
            qh = q[:, sl]
            qh = (qh * lax.rsqrt(_group_sumsq(qh, bd_ref[...]) * (1.0 / X_HEAD_DIM) + EPS) * qgain_ref[:, sl]).astype(BF16)
            s = lax.dot_general(qh, k_ref[:, sl], (((1,), (1,)), ((), ())), preferred_element_type=F32)
            p = jnp.exp(s - jnp.max(s, axis=-1, keepdims=True))
            den = jnp.sum(p, axis=-1, keepdims=True)
            heads.append((jnp.dot(p.astype(BF16), v_ref[:, sl], preferred_element_type=F32) / den).astype(BF16))
        o = jnp.concatenate(heads, axis=-1)
        h2 = h + jnp.dot(o, wo_ref[...], preferred_element_type=F32)
        h2_ref[rows, :] = h2

        xn2 = h2 * _rms_scale(h2) * gm_ref[...]
        xp_ref[rows, :] = _pack_bf16_pair(xn2[:, :half], xn2[:, half:])

        x_hi = xn2.astype(BF16)
        x_lo = (xn2 - x_hi.astype(F32)).astype(BF16)
        r = jnp.dot(jnp.concatenate([x_hi, x_lo], axis=0), wr_ref[...], preferred_element_type=F32)
        n_r = xn2.shape[0]
        return (r[:n_r, :LANES] + r[:n_r, LANES:]) + (r[n_r:, :LANES] + r[n_r:, LANES:]) + br_ref[...]

    lg = attend(slice(0, tm))

    lane = lax.broadcasted_iota(jnp.int32, (tm, LANES), 1)
    lane_f = lane.astype(F32)
    big = jnp.float32(LANES)

    def first_argmax(vals, vmax):
        idx = jnp.min(jnp.where(vals == vmax, lane_f, big), axis=-1, keepdims=True)
        return idx.astype(jnp.int32)

    glog = jnp.where(lane < N_GROUPS, lg, -jnp.inf)
    gmax = jnp.max(glog, axis=-1, keepdims=True)
    grp = first_argmax(glog, gmax)
    pg_top = 1.0 / jnp.sum(jnp.exp(glog - gmax), axis=-1, keepdims=True)
    elane = lane - N_GROUPS
    emask = (elane >= 0) & (elane < N_EXPERTS) & ((elane >> 3) == grp)
    elog = jnp.where(emask, lg, -jnp.inf)
    m1 = jnp.max(elog, axis=-1, keepdims=True)
    i1 = first_argmax(elog, m1)
    elog2 = jnp.where(lane == i1, -jnp.inf, elog)
    m2 = jnp.max(elog2, axis=-1, keepdims=True)
    i2 = first_argmax(elog2, m2)
    den = jnp.sum(jnp.exp(elog - m1), axis=-1, keepdims=True)
    p1 = 1.0 / den
    p2 = jnp.exp(m2 - m1) / den
    psum = p1 + p2
    g1 = pg_top * (p1 / psum)
    g2 = pg_top * (p2 / psum)
    e1 = i1 - N_GROUPS
    e2 = i2 - N_GROUPS

    onehot = jnp.where((elane == e1) | (elane == e2), 1.0, 0.0)
    before = jnp.dot(tri_ref[...], onehot.astype(BF16), preferred_element_type=F32) + run_ref[0:1, :]
    r1 = jnp.sum(jnp.where(elane == e1, before, 0.0), axis=-1, keepdims=True).astype(jnp.int32)
    r2 = jnp.sum(jnp.where(elane == e2, before, 0.0), axis=-1, keepdims=True).astype(jnp.int32)
    run = run_ref[...] + jnp.sum(onehot, axis=0, keepdims=True)
    run_ref[...] = run
    cnt_ref[0] = run.astype(jnp.int32)

    ri = jnp.where(lane == 0, e1, jnp.where(lane == 1, e2, jnp.where(lane == 2, r1, jnp.where(lane == 3, r2, 0))))
    ri_ref[0] = jnp.transpose(ri)[0:SUBLANES, :]
    rf_ref[...] = jnp.where(lane == 0, g1, jnp.where(lane == 1, g2, 0.0))


def _xattn_route(h1, memkv, gx, w_xq, q_gain, w_xo, gm, wr, b_r, seq, mem_len, tm=512):
    n, d = h1.shape
    tiles_per_seq = seq // tm
    nt = n // tm
    tri = jnp.asarray(np.tril(np.ones((tm, tm), np.float32), -1), dtype=BF16)
    bd = _block_diag_ones(X_HEAD_DIM)
    return pl.pallas_call(
        _xattn_route_kernel,
        grid=(nt,),
        in_specs=[
            pl.BlockSpec((tm, d), lambda i: (i, 0)),
            _const_spec((1, d)),
            _const_spec((d, X_WIDTH)),
            _const_spec((1, X_WIDTH)),
            _const_spec((LANES, LANES)),
            pl.BlockSpec((mem_len, X_WIDTH), lambda i: (i // tiles_per_seq, 0)),
            pl.BlockSpec((mem_len, X_WIDTH), lambda i: (i // tiles_per_seq, 1)),
            _const_spec((X_WIDTH, d)),
            _const_spec((1, d)),
            _const_spec((d, 2 * LANES)),
            _const_spec((1, LANES)),
            _const_spec((tm, tm)),
        ],
        out_specs=[
            pl.BlockSpec((tm, d), lambda i: (i, 0)),
            pl.BlockSpec((tm, d // 2), lambda i: (i, 0)),
            pl.BlockSpec((1, SUBLANES, tm), lambda i: (i, 0, 0)),
            pl.BlockSpec((tm, LANES), lambda i: (i, 0)),
            pl.BlockSpec((1, 8, LANES), lambda i: (i, 0, 0)),
        ],
        out_shape=[
            jax.ShapeDtypeStruct((n, d), F32),
            jax.ShapeDtypeStruct((n, d // 2), jnp.uint32),
            jax.ShapeDtypeStruct((nt, SUBLANES, tm), jnp.int32),
            jax.ShapeDtypeStruct((n, LANES), F32),
            jax.ShapeDtypeStruct((nt, 8, LANES), jnp.int32),
        ],
        scratch_shapes=[pltpu.VMEM((8, LANES), F32)],
        compiler_params=_cparams(("arbitrary",)),
        name="xattn_route",
    )(h1, gx.reshape(1, d), w_xq, q_gain, bd, memkv, memkv, w_xo, gm.reshape(1, d), wr, b_r, tri)


def _dispatch_kernel(tail_ref, d0_ref, d1_ref, x_ref, xs_ref, zero_ref, sem, zsem):
    tm = x_ref.shape[0]

    @pl.when(pl.program_id(0) == 0)
    def _():
        zero_ref[...] = jnp.zeros_like(zero_ref)

        def zero_copy(e):
            start = pl.multiple_of(tail_ref[e], DISPATCH_BLOCK)
            return pltpu.make_async_copy(zero_ref, xs_ref.at[pl.ds(start, DISPATCH_BLOCK)], zsem)

        for e in range(N_EXPERTS):
            pl.when(tail_ref[e] >= 0)(lambda e=e: zero_copy(e).start())
        for e in range(N_EXPERTS):
            pl.when(tail_ref[e] >= 0)(lambda e=e: zero_copy(e).wait())

        def unused_copy(b):
            return pltpu.make_async_copy(zero_ref, xs_ref.at[pl.ds(pl.multiple_of(b * DISPATCH_BLOCK, DISPATCH_BLOCK),
                                                                   DISPATCH_BLOCK)], zsem)

        n_blocks = xs_ref.shape[0] // DISPATCH_BLOCK
        lax.fori_loop(tail_ref[N_EXPERTS], n_blocks, lambda b, c: (unused_copy(b).start(), c)[1], 0)
        lax.fori_loop(tail_ref[N_EXPERTS], n_blocks, lambda b, c: (unused_copy(b).wait(), c)[1], 0)

    def copy(t, k):
        return pltpu.make_async_copy(x_ref.at[pl.ds(t, 1)], xs_ref.at[pl.ds((d0_ref, d1_ref)[k][0, 0, t], 1)], sem)

    def issue(tb, carry):
        for j in range(DMA_UNROLL):
            copy(tb * DMA_UNROLL + j, 0).start()
            copy(tb * DMA_UNROLL + j, 1).start()
        return carry

    lax.fori_loop(0, tm // DMA_UNROLL, issue, 0)

    def drain(tb, carry):
        for j in range(DMA_UNROLL):
            copy(tb * DMA_UNROLL + j, 0).wait()
            copy(tb * DMA_UNROLL + j, 1).wait()
        return carry

    lax.fori_loop(0, tm // DMA_UNROLL, drain, 0)


def _dispatch(xp, dest, tail, rows):
    n, w = xp.shape
    nt, tm = dest[0].shape
    grid_spec = pltpu.PrefetchScalarGridSpec(
        num_scalar_prefetch=1,
        grid=(nt,),
        in_specs=[
            pl.BlockSpec((1, 1, tm), lambda i, tail: (i, 0, 0), memory_space=pltpu.SMEM),
            pl.BlockSpec((1, 1, tm), lambda i, tail: (i, 0, 0), memory_space=pltpu.SMEM),
            pl.BlockSpec((tm, w), lambda i, tail: (i, 0)),
        ],
        out_specs=pl.BlockSpec(memory_space=pl.ANY),
        scratch_shapes=[pltpu.VMEM((DISPATCH_BLOCK, w), xp.dtype), pltpu.SemaphoreType.DMA(()),
                        pltpu.SemaphoreType.DMA(())],
    )
    return pl.pallas_call(
        _dispatch_kernel,
        grid_spec=grid_spec,
        out_shape=jax.ShapeDtypeStruct((rows, w), xp.dtype),
        compiler_params=_cparams(("arbitrary",)),
        name="dispatch",
    )(tail, dest[0].reshape(nt, 1, tm), dest[1].reshape(nt, 1, tm), xp)


def _experts_kernel(blk_e_ref, blk_x_ref, nused_ref, first_ref, next_e_ref, xs_ref, wgu_hbm, wdn_hbm, y_ref,
                    wgu_f32, wdn_f32, wgu_bf, wdn_bf, sems):
    b = pl.program_id(0)
    half = xs_ref.shape[1]

    def fetch(e):
        return (pltpu.make_async_copy(wgu_hbm.at[e], wgu_f32, sems.at[0]),
                pltpu.make_async_copy(wdn_hbm.at[e], wdn_f32, sems.at[1]))

    @pl.when(b == 0)
    def _():
        for c in fetch(blk_e_ref[0]):
            c.start()

    @pl.when(first_ref[b] == 1)
    def _():
        for c in fetch(blk_e_ref[b]):
            c.wait()
        wgu_bf[...] = wgu_f32[...].astype(BF16)
        wdn_bf[...] = wdn_f32[...].astype(BF16)

        @pl.when(next_e_ref[b] >= 0)
        def _():
            for c in fetch(next_e_ref[b]):
                c.start()

    @pl.when(b < nused_ref[0])
    def _():
        x_lo, x_hi = _unpack_bf16_pair(xs_ref[...])
        gu = (jnp.dot(x_lo.astype(BF16), wgu_bf[0:half, :], preferred_element_type=F32)
              + jnp.dot(x_hi.astype(BF16), wgu_bf[half:, :], preferred_element_type=F32))
        g = gu[:, :D_EXPERT]
        u = gu[:, D_EXPERT:]
        act = (g * jax.nn.sigmoid(g) * u).astype(BF16)
        y = jnp.dot(act, wdn_bf[...], preferred_element_type=F32)
        y_ref[...] = _pack_bf16_pair(y[:, :half], y[:, half:])

    @pl.when(b >= nused_ref[0])
    def _():
        y_ref[...] = jnp.zeros_like(y_ref)


def _experts(xs, w_gu, w_dn, blk_e, blk_x, nused, first, next_e):
    rows, half = xs.shape
    d = 2 * half
    n_blocks = rows // DISPATCH_BLOCK
    grid_spec = pltpu.PrefetchScalarGridSpec(
        num_scalar_prefetch=5,
        grid=(n_blocks,),
        in_specs=[
            pl.BlockSpec((DISPATCH_BLOCK, half), lambda b, be, bx, *_: (bx[b], 0)),
            pl.BlockSpec(memory_space=pl.ANY),
            pl.BlockSpec(memory_space=pl.ANY),
        ],
        out_specs=pl.BlockSpec((DISPATCH_BLOCK, half), lambda b, *_: (b, 0)),
        scratch_shapes=[pltpu.VMEM((d, 2 * D_EXPERT), F32), pltpu.VMEM((D_EXPERT, d), F32),
                        pltpu.VMEM((d, 2 * D_EXPERT), BF16), pltpu.VMEM((D_EXPERT, d), BF16),
                        pltpu.SemaphoreType.DMA((2,))],
    )
    return pl.pallas_call(
        _experts_kernel,
        grid_spec=grid_spec,
        out_shape=jax.ShapeDtypeStruct((rows, half), jnp.uint32),
        compiler_params=_cparams(("arbitrary",)),
        name="experts",
    )(blk_e, blk_x, nused, first, next_e, xs, w_gu, w_dn)


def _combine_kernel(cur0_ref, cur1_ref, next0_ref, next1_ref, h_ref, rf_ref, y_ref, o_ref, ya_ref, yb_ref, sems):
    i = pl.program_id(0)
    nt = pl.num_programs(0)
    tm, d = h_ref.shape
    half = d // 2
    slot = i % 2

    def issue(d0_ref, d1_ref, s):
        def body(tb, carry):
            for j in range(DMA_UNROLL):
                t = tb * DMA_UNROLL + j
                pltpu.make_async_copy(y_ref.at[pl.ds(d0_ref[0, 0, t], 1)], ya_ref.at[s, pl.ds(t, 1)], sems.at[s]).start()
                pltpu.make_async_copy(y_ref.at[pl.ds(d1_ref[0, 0, t], 1)], yb_ref.at[s, pl.ds(t, 1)], sems.at[s]).start()
            return carry

        lax.fori_loop(0, tm // DMA_UNROLL, body, 0)

    pl.when(i == 0)(lambda: issue(cur0_ref, cur1_ref, 0))
    pl.when(i + 1 < nt)(lambda: issue(next0_ref, next1_ref, 1 - slot))
    pltpu.make_async_copy(y_ref.at[pl.ds(0, tm)], ya_ref.at[slot], sems.at[slot]).wait()
    pltpu.make_async_copy(y_ref.at[pl.ds(0, tm)], yb_ref.at[slot], sems.at[slot]).wait()
    g1 = rf_ref[:, 0:1]
    g2 = rf_ref[:, 1:2]
    a_lo, a_hi = _unpack_bf16_pair(ya_ref[slot])
    b_lo, b_hi = _unpack_bf16_pair(yb_ref[slot])
    o_ref[:, :half] = h_ref[:, :half] + (a_lo * g1 + b_lo * g2)
    o_ref[:, half:] = h_ref[:, half:] + (a_hi * g1 + b_hi * g2)


def _combine(h2, rf, dest, ys, tm=256):
    n, d = h2.shape
    nt = n // tm
    half = d // 2
    cur = pl.BlockSpec((1, 1, tm), lambda i: (i, 0, 0), memory_space=pltpu.SMEM)
    nxt = pl.BlockSpec((1, 1, tm), lambda i: (jnp.minimum(i + 1, nt - 1), 0, 0), memory_space=pltpu.SMEM)
    d0 = dest[0].reshape(nt, 1, tm)
    d1 = dest[1].reshape(nt, 1, tm)
    return pl.pallas_call(
        _combine_kernel,
        grid=(nt,),
        in_specs=[
            cur, cur, nxt, nxt,
            pl.BlockSpec((tm, d), lambda i: (i, 0)),
            pl.BlockSpec((tm, LANES), lambda i: (i, 0)),
            pl.BlockSpec(memory_space=pl.ANY),
        ],
        out_specs=pl.BlockSpec((tm, d), lambda i: (i, 0)),
        out_shape=jax.ShapeDtypeStruct((n, d), F32),
        scratch_shapes=[pltpu.VMEM((2, tm, half), jnp.uint32), pltpu.VMEM((2, tm, half), jnp.uint32),
                        pltpu.SemaphoreType.DMA((2,))],
        compiler_params=_cparams(("arbitrary",)),
        name="combine",
    )(d0, d1, d0, d1, h2, rf, ys)


def kernel(x, mem, rel_bias, norm_mix_g, w_in, conv_dw_w, conv_dw_b, conv_ln_g, conv_ln_b, w_conv_out, q_norm_g, k_norm_g, attn_sinks, w_attn_out, w_mix_out, norm_x_g, norm_mem_g, w_xq, w_xkv, xq_norm_g, xk_norm_g, w_xo, norm_moe_g, w_router_group, b_router_group, w_router_expert, b_router_expert, w_expert_gu, w_expert_down):
    batch, seq, d = x.shape
    mem_len = mem.shape[1]
    n = batch * seq
    h = x.reshape(n, d)
    for l in range(norm_mix_g.shape[0]):
        w = w_in[l].astype(BF16)
        q_gain = jnp.tile(q_norm_g[l].astype(F32) * HEAD_DIM ** -0.5, N_Q_HEADS)
        gain = jnp.concatenate([q_gain, jnp.tile(k_norm_g[l].astype(F32), N_KV_HEADS), jnp.ones((KV_WIDTH,), F32)])
        flag = jnp.concatenate([jnp.ones((ATTN_WIDTH + KV_WIDTH,), F32), jnp.zeros((KV_WIDTH,), F32)])
        conv_act, qkv, gates = _in_proj(h, norm_mix_g[l], w, gain, flag, conv_dw_w[l].reshape(CONV_WIDTH, CONV_CH),
                                        conv_dw_b[l], conv_ln_g[l], conv_ln_b[l], seq)
        attn_o = _swa(qkv, rel_bias, attn_sinks[l], batch, seq)
        h = _mix(conv_act, attn_o, gates, h, w_conv_out[l].astype(BF16), w_attn_out[l].astype(BF16),
                 w_mix_out[l].astype(BF16))

        kgain = jnp.concatenate([jnp.tile(xk_norm_g[l].astype(F32), X_HEADS), jnp.ones((X_WIDTH,), F32)])
        kflag = jnp.concatenate([jnp.ones((X_WIDTH,), F32), jnp.zeros((X_WIDTH,), F32)])
        memkv = _mem_proj(mem.reshape(batch * mem_len, d), norm_mem_g[l], w_xkv[l].astype(BF16), kgain, kflag)
        xq_gain = jnp.tile(xq_norm_g[l].astype(F32) * X_HEAD_DIM ** -0.5, X_HEADS).reshape(1, -1)
        w_r = jnp.concatenate([w_router_group[l], w_router_expert[l]], axis=1).astype(F32)
        w_r = jnp.pad(w_r, ((0, 0), (0, LANES - w_r.shape[1])))
        wr_hi = w_r.astype(BF16)
        wr = jnp.concatenate([wr_hi, (w_r - wr_hi.astype(F32)).astype(BF16)], axis=1)
        b_r = jnp.concatenate([b_router_group[l], b_router_expert[l]]).astype(F32)
        b_r = jnp.pad(b_r, (0, LANES - b_r.shape[0])).reshape(1, LANES)
        h2, xp, ri, rf, cnt = _xattn_route(h, memkv, norm_x_g[l], w_xq[l].astype(BF16), xq_gain, w_xo[l].astype(BF16),
                                           norm_moe_g[l], wr, b_r, seq, mem_len)

        counts = cnt[-1, 0, N_GROUPS:N_GROUPS + N_EXPERTS]
        padded = (counts + DISPATCH_BLOCK - 1) // DISPATCH_BLOCK * DISPATCH_BLOCK
        pad_end = jnp.cumsum(padded)
        n_blocks = -(-(2 * n) // DISPATCH_BLOCK) + N_EXPERTS
        rows = n_blocks * DISPATCH_BLOCK
        nused = (pad_end[-1] // DISPATCH_BLOCK).astype(jnp.int32)
        blk = jnp.minimum(jnp.arange(n_blocks, dtype=jnp.int32), nused - 1)
        blk_e = jnp.minimum(jnp.sum(pad_end[None, :] <= (blk * DISPATCH_BLOCK)[:, None], axis=1), N_EXPERTS - 1).astype(jnp.int32)
        dest = [ri[:, 2 + k, :] + sum(jnp.where(ri[:, k, :] > e, padded[e], 0) for e in range(N_EXPERTS - 1))
                for k in range(2)]

        tail = jnp.concatenate([jnp.where(counts > 0, pad_end - DISPATCH_BLOCK, -1), nused.reshape(1)]).astype(jnp.int32)
        xs = _dispatch(xp, dest, tail, rows)
        first = jnp.concatenate([jnp.ones((1,), jnp.int32), (blk_e[1:] != blk_e[:-1]).astype(jnp.int32)])
        eids = jnp.arange(N_EXPERTS, dtype=jnp.int32)
        later = (eids[None, :] > blk_e[:, None]) & (counts[None, :] > 0)
        next_e = jnp.min(jnp.where(later, eids[None, :], N_EXPERTS), axis=1)
        next_e = jnp.where(next_e < N_EXPERTS, next_e, -1).astype(jnp.int32)
        ys = _experts(xs, w_expert_gu[l], w_expert_down[l], blk_e, blk, nused.reshape(1), first, next_e)
        h = _combine(h2, rf, dest, ys)
    return h.reshape(batch, seq, d)
```

```python
import functools
import math

import jax
import jax.numpy as jnp
import numpy as np
from jax import lax
from jax.experimental import pallas as pl
from jax.experimental.pallas import tpu as pltpu

EPS = 1e-6
NEG = -1e30

CONV_CH = 1024
CONV_WIDTH = 31
N_Q_HEADS = 16
N_KV_HEADS = 2
HEAD_DIM = 64
ATTN_WIDTH = N_Q_HEADS * HEAD_DIM
KV_WIDTH = N_KV_HEADS * HEAD_DIM
WINDOW = 128
NUM_BUCKETS = 32
MAX_DISTANCE = 128
X_HEADS = 4
X_HEAD_DIM = 128
X_WIDTH = X_HEADS * X_HEAD_DIM
N_GROUPS = 4
EXPERTS_PER_GROUP = 8
N_EXPERTS = N_GROUPS * EXPERTS_PER_GROUP
D_EXPERT = 512
DISPATCH_BLOCK = 256

LANES = 128
SUBLANES = 8
MXU_K = 256
DMA_UNROLL = 8
CONV_HALO = 32
VMEM_LIMIT = 56 * 1024 * 1024

BF16 = jnp.bfloat16
F32 = jnp.float32


def _cparams(sem):
    return pltpu.CompilerParams(dimension_semantics=sem, vmem_limit_bytes=VMEM_LIMIT)


def _const_spec(shape):
    nd = len(shape)
    return pl.BlockSpec(shape, lambda *_: (0,) * nd, pipeline_mode=pl.Buffered(1))


def _rms_scale(x):
    return lax.rsqrt(jnp.mean(x * x, axis=-1, keepdims=True) + EPS)


def _group_sumsq(y, bd):
    sq = y * y
    hi = sq.astype(BF16)
    lo = (sq - hi.astype(F32)).astype(BF16)
    return (jnp.dot(hi, bd, preferred_element_type=F32) + jnp.dot(lo, bd, preferred_element_type=F32))


def _pack_bf16_pair(lo, hi):
    lo_bits = lax.bitcast_convert_type(lo.astype(BF16).astype(F32), jnp.uint32)
    hi_bits = lax.bitcast_convert_type(hi.astype(BF16).astype(F32), jnp.uint32)
    return (lo_bits >> 16) | hi_bits


def _unpack_bf16_pair(w):
    return (lax.bitcast_convert_type(w << 16, F32), lax.bitcast_convert_type(w & jnp.uint32(0xFFFF0000), F32))


def _head_norm_store(y, gain_ref, flag_ref, bd, o_ref, head_dim):
    for c in range(y.shape[1] // LANES):
        sl = slice(c * LANES, (c + 1) * LANES)
        yc = y[:, sl]
        ss = _group_sumsq(yc, bd)
        normed = yc * lax.rsqrt(ss * (1.0 / head_dim) + EPS) * gain_ref[:, sl]
        o_ref[:, sl] = jnp.where(flag_ref[:, sl] > 0.0, normed, yc).astype(o_ref.dtype)


def _opaque_zero(dep):
    bits = lax.shift_right_logical(lax.bitcast_convert_type(dep, jnp.uint32), jnp.uint32(32))
    return lax.bitcast_convert_type(bits, F32)[0:1, :]


def _conv_chunk(c, ext_ref, sh_ref, cv_ref, dww_ref, dwb_ref, row_chunk=128, after=None):
    tm = cv_ref.shape[0]
    off = CONV_HALO - (CONV_WIDTH - 1)
    span = tm + CONV_HALO - SUBLANES
    sl = slice(c * LANES, (c + 1) * LANES)
    for r in range(1, SUBLANES):
        sh_ref[c % 2, r - 1] = ext_ref[r:r + span, sl]
    zero = None if after is None else _opaque_zero(after)
    for rc in range(tm // row_chunk):
        acc = jnp.broadcast_to(dwb_ref[:, sl] if zero is None else dwb_ref[:, sl] + zero, (row_chunk, LANES))
        for j in range(CONV_WIDTH):
            q, r = divmod(off + j, SUBLANES)
            lo = SUBLANES * q + rc * row_chunk
            win = ext_ref[lo:lo + row_chunk, sl] if r == 0 else sh_ref[c % 2, r - 1, lo:lo + row_chunk, :]
            wj = dww_ref[j:j + 1, sl] if zero is None else dww_ref[j:j + 1, sl] + zero
            acc = acc + wj * win
        cv_ref[rc * row_chunk:(rc + 1) * row_chunk, sl] = acc


def _ln_silu(cv_ref, lng_ref, lnb_ref, act_ref):
    cv = cv_ref[...]
    mu = jnp.mean(cv, axis=-1, keepdims=True)
    cen = cv - mu
    var = jnp.mean(cen * cen, axis=-1, keepdims=True)
    ln = cen * lax.rsqrt(var + EPS) * lng_ref[...] + lnb_ref[...]
    act_ref[...] = (ln * jax.nn.sigmoid(ln)).astype(act_ref.dtype)


def _in_proj_kernel(x_ref, g_ref, w_ref, gain_ref, flag_ref, bd_ref, dww_ref, dwb_ref, lng_ref, lnb_ref,
                    act_ref, qkv_ref, gate_ref, xn_ref, ext_ref, sh_ref, cv_ref, *, tn, tiles_per_seq):
    tm = x_ref.shape[0]
    x = x_ref[...]
    xn_ref[...] = (x * _rms_scale(x) * g_ref[...]).astype(BF16)

    def proj(lo, width):
        return jnp.dot(xn_ref[...], w_ref[:, lo:lo + width], preferred_element_type=F32)

    @pl.when(pl.program_id(0) % tiles_per_seq == 0)
    def _():
        ext_ref[0:CONV_HALO, :] = jnp.zeros((CONV_HALO, CONV_CH), F32)

    n_u = act_ref.shape[1]
    n_qkv = qkv_ref.shape[1]
    g0 = 2 * n_u + n_qkv
    n_conv = CONV_CH // LANES
    n_gate = gate_ref.shape[1] // tn
    assert n_u == 2 * tn

    def proj_after(lo, after):
        xn = xn_ref[...]
        if after is not None:
            zb = jnp.broadcast_to(_opaque_zero(after), (SUBLANES, LANES)).astype(BF16)
            zb = jnp.tile(zb, (tm // SUBLANES, MXU_K // LANES))
            xn = jnp.concatenate([xn[:, :MXU_K] + zb, xn[:, MXU_K:]], axis=1)
        return jnp.dot(xn, w_ref[:, lo:lo + tn], preferred_element_type=F32)

    ext_ref[CONV_HALO:, 0:tn] = proj(0, tn) * jax.nn.sigmoid(proj(n_u, tn))
    stages = [("a", tn), ("b", n_u + tn)] + [("gate", g0 + c * tn) for c in range(n_gate)]
    last_rows = (slice(tm - SUBLANES, tm), slice(tn - LANES, tn))
    mxu_done = None
    conv_done = None
    glu_a = None
    for c, (kind, lo) in enumerate(stages):
        if c < n_conv:
            _conv_chunk(c, ext_ref, sh_ref, cv_ref, dww_ref, dwb_ref, after=mxu_done)
        y = proj_after(lo, conv_done if c < n_conv else None)
        mxu_done = y[last_rows]
        if kind == "a":
            glu_a = y
        elif kind == "b":
            ext_ref[CONV_HALO:, tn:2 * tn] = glu_a * jax.nn.sigmoid(y)
        else:
            col = lo - g0
            gate_ref[:, col:col + tn] = jax.nn.sigmoid(y).astype(gate_ref.dtype)
        if c < n_conv:
            conv_done = cv_ref[tm - SUBLANES:tm, c * LANES:(c + 1) * LANES]
    ext_ref[0:CONV_HALO, :] = ext_ref[tm:tm + CONV_HALO, :]
    _ln_silu(cv_ref, lng_ref, lnb_ref, act_ref)
    _head_norm_store(proj(2 * n_u, n_qkv), gain_ref, flag_ref, bd_ref[...], qkv_ref, HEAD_DIM)


def _in_proj(x, g, w, gain, flag, dw_w, dw_b, ln_g, ln_b, seq, tm=256, tn=512):
    m, d = x.shape
    n_qkv = ATTN_WIDTH + 2 * KV_WIDTH
    n_gate = w.shape[1] - 2 * CONV_CH - n_qkv
    assert seq % tm == 0
    row = lambda i: (i, 0)
    vec = _const_spec((1, CONV_CH))
    return pl.pallas_call(
        functools.partial(_in_proj_kernel, tn=tn, tiles_per_seq=seq // tm),
        grid=(m // tm,),
        in_specs=[pl.BlockSpec((tm, d), row), _const_spec((1, d)), _const_spec(w.shape), _const_spec((1, n_qkv)),
                  _const_spec((1, n_qkv)), _const_spec((LANES, LANES)), _const_spec((CONV_WIDTH, CONV_CH)), vec, vec, vec],
        out_specs=[pl.BlockSpec((tm, CONV_CH), row), pl.BlockSpec((tm, n_qkv), row), pl.BlockSpec((tm, n_gate), row)],
        out_shape=[jax.ShapeDtypeStruct((m, CONV_CH), BF16), jax.ShapeDtypeStruct((m, n_qkv), BF16),
                   jax.ShapeDtypeStruct((m, n_gate), BF16)],
        scratch_shapes=[pltpu.VMEM((tm, d), BF16), pltpu.VMEM((tm + CONV_HALO, CONV_CH), F32),
                        pltpu.VMEM((2, SUBLANES - 1, tm + CONV_HALO - SUBLANES, LANES), F32), pltpu.VMEM((tm, CONV_CH), F32)],
        compiler_params=_cparams(("arbitrary",)),
        name="in_proj",
    )(x, g.reshape(1, d), w, gain.reshape(1, -1), flag.reshape(1, -1), _block_diag_ones(HEAD_DIM),
      dw_w, dw_b.reshape(1, -1), ln_g.reshape(1, -1), ln_b.reshape(1, -1))


def _mem_proj_kernel(x_ref, g_ref, w_ref, gain_ref, flag_ref, bd_ref, o_ref):
    x = x_ref[...]
    xn = (x * _rms_scale(x) * g_ref[...]).astype(BF16)
    y = jnp.dot(xn, w_ref[...], preferred_element_type=F32)
    _head_norm_store(y, gain_ref, flag_ref, bd_ref[...], o_ref, X_HEAD_DIM)


def _mem_proj(x, g, w, gain, flag, tm=256):
    m, d = x.shape
    n = w.shape[1]
    return pl.pallas_call(
        _mem_proj_kernel,
        grid=(m // tm,),
        in_specs=[pl.BlockSpec((tm, d), lambda i: (i, 0)), _const_spec((1, d)), _const_spec(w.shape), _const_spec((1, n)),
                  _const_spec((1, n)), _const_spec((LANES, LANES))],
        out_specs=pl.BlockSpec((tm, n), lambda i: (i, 0)),
        out_shape=jax.ShapeDtypeStruct((m, n), BF16),
        compiler_params=_cparams(("parallel",)),
        name="mem_proj",
    )(x, g.reshape(1, d), w, gain.reshape(1, -1), flag.reshape(1, -1), _block_diag_ones(X_HEAD_DIM))


def _block_diag_ones(group):
    r = np.arange(LANES)
    return jnp.asarray((r[:, None] // group) == (r[None, :] // group), dtype=BF16)


def _swa_kernel(sink_ref, q_ref, kp_ref, kc_ref, vp_ref, vc_ref, bias_ref, o_ref, *, n_chunks):
    first = pl.program_id(1) == 0
    qb = q_ref.shape[0]
    lane = lax.broadcasted_iota(jnp.int32, (qb, LANES), 1)
    lane2 = lax.broadcasted_iota(jnp.int32, (2 * qb, LANES), 1)
    row = lax.broadcasted_iota(jnp.int32, (2 * qb, 1), 0)
    col = lax.broadcasted_iota(jnp.int32, (2 * qb, 2 * qb), 1)

    def both_halves(prev_ref, cur_ref):
        t = jnp.concatenate([prev_ref[...], cur_ref[...]], axis=0).astype(F32)
        swapped = pltpu.roll(t, HEAD_DIM, 1)
        return (jnp.where(lane2 < HEAD_DIM, t, swapped).astype(BF16), jnp.where(lane2 < HEAD_DIM, swapped, t).astype(BF16))

    ks = both_halves(kp_ref, kc_ref)
    vs = both_halves(vp_ref, vc_ref)
    no_prev = jnp.where(jnp.logical_and(first, col < qb), NEG, 0.0)
    chunks_per_kv = n_chunks // N_KV_HEADS
    for c in range(n_chunks):
        q2 = q_ref[:, c * LANES:(c + 1) * LANES]
        zero = jnp.zeros_like(q2)
        qs = jnp.concatenate([jnp.where(lane < HEAD_DIM, q2, zero), jnp.where(lane >= HEAD_DIM, q2, zero)], axis=0)
        s = lax.dot_general(qs, ks[c // chunks_per_kv], (((1,), (1,)), ((), ())), preferred_element_type=F32)
        s = s + bias_ref[c] + no_prev
        sink = jnp.where(row < qb, sink_ref[2 * c], sink_ref[2 * c + 1])
        m = jnp.maximum(jnp.max(s, axis=-1, keepdims=True), sink)
        p = jnp.exp(s - m)
        den = jnp.sum(p, axis=-1, keepdims=True) + jnp.exp(sink - m)
        o2 = jnp.dot(p.astype(BF16), vs[c // chunks_per_kv], preferred_element_type=F32) / den
        o = jnp.where(lane < HEAD_DIM, o2[:qb], o2[qb:])
        o_ref[:, c * LANES:(c + 1) * LANES] = o.astype(o_ref.dtype)


def _t5_bucket_np(dist):
    n = np.maximum(dist, 0)
    max_exact = NUM_BUCKETS // 2
    large = max_exact + (np.log(np.maximum(n, 1).astype(np.float32) / max_exact)
                         / math.log(MAX_DISTANCE / max_exact) * (NUM_BUCKETS - max_exact)).astype(np.int32)
    large = np.minimum(large, NUM_BUCKETS - 1)
    return np.where(n < max_exact, n, large)


def _swa(qkv, rel_bias, sinks, batch, seq):
    qb = WINDOW
    nb = seq // qb
    n_chunks = N_Q_HEADS // 2
    qi = np.arange(qb)[:, None]
    kj = np.arange(2 * qb)[None, :]
    dist = qi + qb - kj
    valid = (dist >= 0) & (dist < WINDOW)
    onehot = jnp.asarray(_t5_bucket_np(dist)[:, :, None] == np.arange(NUM_BUCKETS), dtype=F32)
    bias = jnp.einsum('qkb,bh->qkh', onehot, rel_bias.astype(F32), precision=lax.Precision.HIGHEST)
    bias = jnp.where(valid[:, :, None], bias, NEG).transpose(2, 0, 1)
    bias = bias.reshape(n_chunks, 2 * qb, 2 * qb)
    kcol = ATTN_WIDTH // LANES
    vcol = kcol + 1

    def prev(b, n):
        return (b * nb + jnp.maximum(n - 1, 0))

    return pl.pallas_call(
        functools.partial(_swa_kernel, n_chunks=n_chunks),
        grid=(batch, nb),
        in_specs=[
            pl.BlockSpec(memory_space=pltpu.SMEM),
            pl.BlockSpec((qb, ATTN_WIDTH), lambda b, n: (b * nb + n, 0)),
            pl.BlockSpec((qb, LANES), lambda b, n: (prev(b, n), kcol)),
            pl.BlockSpec((qb, LANES), lambda b, n: (b * nb + n, kcol)),
            pl.BlockSpec((qb, LANES), lambda b, n: (prev(b, n), vcol)),
            pl.BlockSpec((qb, LANES), lambda b, n: (b * nb + n, vcol)),
            _const_spec((n_chunks, 2 * qb, 2 * qb)),
        ],
        out_specs=pl.BlockSpec((qb, ATTN_WIDTH), lambda b, n: (b * nb + n, 0)),
        out_shape=jax.ShapeDtypeStruct((batch * seq, ATTN_WIDTH), BF16),
        compiler_params=_cparams(("parallel", "parallel")),
        name="swa",
    )(sinks.astype(F32), qkv, qkv, qkv, qkv, qkv, bias)


def _mix_kernel(c_ref, a_ref, g0_ref, g1_ref, x_ref, wc_ref, wa_ref, wm_ref, o_ref):
    y_conv = jnp.dot(c_ref[...], wc_ref[...], preferred_element_type=F32)
    y_attn = jnp.dot(a_ref[...], wa_ref[...], preferred_element_type=F32)
    merged = g0_ref[...].astype(F32) * y_conv + g1_ref[...].astype(F32) * y_attn
    o_ref[...] = x_ref[...] + jnp.dot(merged.astype(BF16), wm_ref[...], preferred_element_type=F32)


def _mix(conv_act, attn_o, gates, x, w_conv_out, w_attn_out, w_mix_out, tm=256):
    n, d = x.shape
    return pl.pallas_call(
        _mix_kernel,
        grid=(n // tm,),
        in_specs=[
            pl.BlockSpec((tm, CONV_CH), lambda i: (i, 0)),
            pl.BlockSpec((tm, ATTN_WIDTH), lambda i: (i, 0)),
            pl.BlockSpec((tm, d), lambda i: (i, 0)),
            pl.BlockSpec((tm, d), lambda i: (i, 1)),
            pl.BlockSpec((tm, d), lambda i: (i, 0)),
            _const_spec((CONV_CH, d)),
            _const_spec((ATTN_WIDTH, d)),
            _const_spec((d, d)),
        ],
        out_specs=pl.BlockSpec((tm, d), lambda i: (i, 0)),
        out_shape=jax.ShapeDtypeStruct((n, d), F32),
        compiler_params=_cparams(("parallel",)),
        name="mix",
    )(conv_act, attn_o, gates, gates, x, w_conv_out, w_attn_out, w_mix_out)


def _xattn_route_kernel(h_ref, gx_ref, wq_ref, qgain_ref, bd_ref, k_ref, v_ref, wo_ref, gm_ref,
                        wr_ref, br_ref, tri_ref,
                        h2_ref, xp_ref, ri_ref, rf_ref, cnt_ref, run_ref):
    tm, d = h_ref.shape
    half = d // 2

    @pl.when(pl.program_id(0) == 0)
    def _():
        run_ref[...] = jnp.zeros_like(run_ref)

    def attend(rows):
        h = h_ref[rows, :]
        xn = (h * _rms_scale(h) * gx_ref[...]).astype(BF16)
        q = jnp.dot(xn, wq_ref[...], preferred_element_type=F32)
        heads = []
        for hd in range(X_HEADS):
            sl = slice(hd * X_HEAD_DIM, (hd + 1) * X_HEAD_DIM)
            qh = q[:, sl]
            qh = (qh * lax.rsqrt(_group_sumsq(qh, bd_ref[...]) * (1.0 / X_HEAD_DIM) + EPS) * qgain_ref[:, sl]).astype(BF16)
            s = lax.dot_general(qh, k_ref[:, sl], (((1,), (1,)), ((), ())), preferred_element_type=F32)
            p = jnp.exp(s - jnp.max(s, axis=-1, keepdims=True))
            den = jnp.sum(p, axis=-1, keepdims=True)
            heads.append((jnp.dot(p.astype(BF16), v_ref[:, sl], preferred_element_type=F32) / den).astype(BF16))
        o = jnp.concatenate(heads, axis=-1)
        h2 = h + jnp.dot(o, wo_ref[...], preferred_element_type=F32)
        h2_ref[rows, :] = h2

        xn2 = h2 * _rms_scale(h2) * gm_ref[...]
        xp_ref[rows, :] = _pack_bf16_pair(xn2[:, :half], xn2[:, half:])

        x_hi = xn2.astype(BF16)
        x_lo = (xn2 - x_hi.astype(F32)).astype(BF16)
        r = jnp.dot(jnp.concatenate([x_hi, x_lo], axis=0), wr_ref[...], preferred_element_type=F32)
        n_r = xn2.shape[0]
        return (r[:n_r, :LANES] + r[:n_r, LANES:]) + (r[n_r:, :LANES] + r[n_r:, LANES:]) + br_ref[...]

    lg = attend(slice(0, tm))

    lane = lax.broadcasted_iota(jnp.int32, (tm, LANES), 1)
    lane_f = lane.astype(F32)
    big = jnp.float32(LANES)

    def first_argmax(vals, vmax):
        idx = jnp.min(jnp.where(vals == vmax, lane_f, big), axis=-1, keepdims=True)
        return idx.astype(jnp.int32)

    glog = jnp.where(lane < N_GROUPS, lg, -jnp.inf)
    gmax = jnp.max(glog, axis=-1, keepdims=True)
    grp = first_argmax(glog, gmax)
    pg_top = 1.0 / jnp.sum(jnp.exp(glog - gmax), axis=-1, keepdims=True)
    elane = lane - N_GROUPS
    emask = (elane >= 0) & (elane < N_EXPERTS) & ((elane >> 3) == grp)
    elog = jnp.where(emask, lg, -jnp.inf)
    m1 = jnp.max(elog, axis=-1, keepdims=True)
    i1 = first_argmax(elog, m1)
    elog2 = jnp.where(lane == i1, -jnp.inf, elog)
    m2 = jnp.max(elog2, axis=-1, keepdims=True)
    i2 = first_argmax(elog2, m2)
    den = jnp.sum(jnp.exp(elog - m1), axis=-1, keepdims=True)
    p1 = 1.0 / den
    p2 = jnp.exp(m2 - m1) / den
    psum = p1 + p2
    g1 = pg_top * (p1 / psum)
    g2 = pg_top * (p2 / psum)
    e1 = i1 - N_GROUPS
    e2 = i2 - N_GROUPS

    onehot = jnp.where((elane == e1) | (elane == e2), 1.0, 0.0)
    before = jnp.dot(tri_ref[...], onehot.astype(BF16), preferred_element_type=F32) + run_ref[0:1, :]
    r1 = jnp.sum(jnp.where(elane == e1, before, 0.0), axis=-1, keepdims=True).astype(jnp.int32)
    r2 = jnp.sum(jnp.where(elane == e2, before, 0.0), axis=-1, keepdims=True).astype(jnp.int32)
    run = run_ref[...] + jnp.sum(onehot, axis=0, keepdims=True)
    run_ref[...] = run
    cnt_ref[0] = run.astype(jnp.int32)

    ri = jnp.where(lane == 0, e1, jnp.where(lane == 1, e2, jnp.where(lane == 2, r1, jnp.where(lane == 3, r2, 0))))
    ri_ref[0] = jnp.transpose(ri)[0:SUBLANES, :]
    rf_ref[...] = jnp.where(lane == 0, g1, jnp.where(lane == 1, g2, 0.0))


def _xattn_route(h1, memkv, gx, w_xq, q_gain, w_xo, gm, wr, b_r, seq, mem_len, tm=512):
    n, d = h1.shape
    tiles_per_seq = seq // tm
    nt = n // tm
    tri = jnp.asarray(np.tril(np.ones((tm, tm), np.float32), -1), dtype=BF16)
    bd = _block_diag_ones(X_HEAD_DIM)
    return pl.pallas_call(
        _xattn_route_kernel,
        grid=(nt,),
        in_specs=[
            pl.BlockSpec((tm, d), lambda i: (i, 0)),
            _const_spec((1, d)),
            _const_spec((d, X_WIDTH)),
            _const_spec((1, X_WIDTH)),
            _const_spec((LANES, LANES)),
            pl.BlockSpec((mem_len, X_WIDTH), lambda i: (i // tiles_per_seq, 0)),
            pl.BlockSpec((mem_len, X_WIDTH), lambda i: (i // tiles_per_seq, 1)),
            _const_spec((X_WIDTH, d)),
            _const_spec((1, d)),
            _const_spec((d, 2 * LANES)),
            _const_spec((1, LANES)),
            _const_spec((tm, tm)),
        ],
        out_specs=[
            pl.BlockSpec((tm, d), lambda i: (i, 0)),
            pl.BlockSpec((tm, d // 2), lambda i: (i, 0)),
            pl.BlockSpec((1, SUBLANES, tm), lambda i: (i, 0, 0)),
            pl.BlockSpec((tm, LANES), lambda i: (i, 0)),
            pl.BlockSpec((1, 8, LANES), lambda i: (i, 0, 0)),
        ],
        out_shape=[
            jax.ShapeDtypeStruct((n, d), F32),
            jax.ShapeDtypeStruct((n, d // 2), jnp.uint32),
            jax.ShapeDtypeStruct((nt, SUBLANES, tm), jnp.int32),
            jax.ShapeDtypeStruct((n, LANES), F32),
            jax.ShapeDtypeStruct((nt, 8, LANES), jnp.int32),
        ],
        scratch_shapes=[pltpu.VMEM((8, LANES), F32)],
        compiler_params=_cparams(("arbitrary",)),
        name="xattn_route",
    )(h1, gx.reshape(1, d), w_xq, q_gain, bd, memkv, memkv, w_xo, gm.reshape(1, d), wr, b_r, tri)


def _dest_rows_kernel(padded_ref, ri_ref, d0_ref, d1_ref):
    for k, d_ref in enumerate((d0_ref, d1_ref)):
        e = ri_ref[:, k, :]
        row = ri_ref[:, 2 + k, :]
        for j in range(N_EXPERTS - 1):
            row = row + jnp.where(e > j, padded_ref[j], 0)
        d_ref[...] = row


def _dest_rows(ri, padded):
    nt, _, tm = ri.shape
    out = jax.ShapeDtypeStruct((nt, tm), jnp.int32)
    return pl.pallas_call(
        _dest_rows_kernel,
        in_specs=[pl.BlockSpec(memory_space=pltpu.SMEM), pl.BlockSpec(memory_space=pltpu.VMEM)],
        out_specs=[pl.BlockSpec(memory_space=pltpu.VMEM)] * 2,
        out_shape=[out, out],
        name="dest_rows",
    )(padded, ri)


def _dispatch_kernel(tail_ref, d0_ref, d1_ref, x_ref, xs_ref, zero_ref, sem, zsem):
    tm = x_ref.shape[0]

    @pl.when(pl.program_id(0) == 0)
    def _():
        zero_ref[...] = jnp.zeros_like(zero_ref)

        def zero_copy(e):
            start = pl.multiple_of(tail_ref[e], DISPATCH_BLOCK)
            return pltpu.make_async_copy(zero_ref, xs_ref.at[pl.ds(start, DISPATCH_BLOCK)], zsem)

        for e in range(N_EXPERTS):
            pl.when(tail_ref[e] >= 0)(lambda e=e: zero_copy(e).start())
        for e in range(N_EXPERTS):
            pl.when(tail_ref[e] >= 0)(lambda e=e: zero_copy(e).wait())

        def unused_copy(b):
            return pltpu.make_async_copy(zero_ref, xs_ref.at[pl.ds(pl.multiple_of(b * DISPATCH_BLOCK, DISPATCH_BLOCK),
                                                                   DISPATCH_BLOCK)], zsem)

        n_blocks = xs_ref.shape[0] // DISPATCH_BLOCK
        lax.fori_loop(tail_ref[N_EXPERTS], n_blocks, lambda b, c: (unused_copy(b).start(), c)[1], 0)
        lax.fori_loop(tail_ref[N_EXPERTS], n_blocks, lambda b, c: (unused_copy(b).wait(), c)[1], 0)

    def copy(t, k):
        return pltpu.make_async_copy(x_ref.at[pl.ds(t, 1)], xs_ref.at[pl.ds((d0_ref, d1_ref)[k][0, 0, t], 1)], sem)

    def issue(tb, carry):
        for j in range(DMA_UNROLL):
            copy(tb * DMA_UNROLL + j, 0).start()
            copy(tb * DMA_UNROLL + j, 1).start()
        return carry

    lax.fori_loop(0, tm // DMA_UNROLL, issue, 0)

    def drain(tb, carry):
        for j in range(DMA_UNROLL):
            copy(tb * DMA_UNROLL + j, 0).wait()
            copy(tb * DMA_UNROLL + j, 1).wait()
        return carry

    lax.fori_loop(0, tm // DMA_UNROLL, drain, 0)


def _dispatch(xp, dest, tail, rows):
    n, w = xp.shape
    nt, tm = dest[0].shape
    grid_spec = pltpu.PrefetchScalarGridSpec(
        num_scalar_prefetch=1,
        grid=(nt,),
        in_specs=[
            pl.BlockSpec((1, 1, tm), lambda i, tail: (i, 0, 0), memory_space=pltpu.SMEM),
            pl.BlockSpec((1, 1, tm), lambda i, tail: (i, 0, 0), memory_space=pltpu.SMEM),
            pl.BlockSpec((tm, w), lambda i, tail: (i, 0)),
        ],
        out_specs=pl.BlockSpec(memory_space=pl.ANY),
        scratch_shapes=[pltpu.VMEM((DISPATCH_BLOCK, w), xp.dtype), pltpu.SemaphoreType.DMA(()),
                        pltpu.SemaphoreType.DMA(())],
    )
    return pl.pallas_call(
        _dispatch_kernel,
        grid_spec=grid_spec,
        out_shape=jax.ShapeDtypeStruct((rows, w), xp.dtype),
        compiler_params=_cparams(("arbitrary",)),
        name="dispatch",
    )(tail, dest[0].reshape(nt, 1, tm), dest[1].reshape(nt, 1, tm), xp)


def _experts_kernel(blk_e_ref, blk_x_ref, nused_ref, first_ref, next_e_ref, xs_ref, wgu_hbm, wdn_hbm, y_ref,
                    wgu_f32, wdn_f32, wgu_bf, wdn_bf, sems):
    b = pl.program_id(0)
    half = xs_ref.shape[1]

    def fetch(e):
        return (pltpu.make_async_copy(wgu_hbm.at[e], wgu_f32, sems.at[0]),
                pltpu.make_async_copy(wdn_hbm.at[e], wdn_f32, sems.at[1]))

    @pl.when(b == 0)
    def _():
        for c in fetch(blk_e_ref[0]):
            c.start()

    @pl.when(first_ref[b] == 1)
    def _():
        for c in fetch(blk_e_ref[b]):
            c.wait()
        wgu_bf[...] = wgu_f32[...].astype(BF16)
        wdn_bf[...] = wdn_f32[...].astype(BF16)

        @pl.when(next_e_ref[b] >= 0)
        def _():
            for c in fetch(next_e_ref[b]):
                c.start()

    @pl.when(b < nused_ref[0])
    def _():
        x_lo, x_hi = _unpack_bf16_pair(xs_ref[...])
        gu = (jnp.dot(x_lo.astype(BF16), wgu_bf[0:half, :], preferred_element_type=F32)
              + jnp.dot(x_hi.astype(BF16), wgu_bf[half:, :], preferred_element_type=F32))
        g = gu[:, :D_EXPERT]
        u = gu[:, D_EXPERT:]
        act = (g * jax.nn.sigmoid(g) * u).astype(BF16)
        y = jnp.dot(act, wdn_bf[...], preferred_element_type=F32)
        y_ref[...] = _pack_bf16_pair(y[:, :half], y[:, half:])

    @pl.when(b >= nused_ref[0])
    def _():
        y_ref[...] = jnp.zeros_like(y_ref)


def _experts(xs, w_gu, w_dn, blk_e, blk_x, nused, first, next_e):
    rows, half = xs.shape
    d = 2 * half
    n_blocks = rows // DISPATCH_BLOCK
    grid_spec = pltpu.PrefetchScalarGridSpec(
        num_scalar_prefetch=5,
        grid=(n_blocks,),
        in_specs=[
            pl.BlockSpec((DISPATCH_BLOCK, half), lambda b, be, bx, *_: (bx[b], 0)),
            pl.BlockSpec(memory_space=pl.ANY),
            pl.BlockSpec(memory_space=pl.ANY),
        ],
        out_specs=pl.BlockSpec((DISPATCH_BLOCK, half), lambda b, *_: (b, 0)),
        scratch_shapes=[pltpu.VMEM((d, 2 * D_EXPERT), F32), pltpu.VMEM((D_EXPERT, d), F32),
                        pltpu.VMEM((d, 2 * D_EXPERT), BF16), pltpu.VMEM((D_EXPERT, d), BF16),
                        pltpu.SemaphoreType.DMA((2,))],
    )
    return pl.pallas_call(
        _experts_kernel,
        grid_spec=grid_spec,
        out_shape=jax.ShapeDtypeStruct((rows, half), jnp.uint32),
        compiler_params=_cparams(("arbitrary",)),
        name="experts",
    )(blk_e, blk_x, nused, first, next_e, xs, w_gu, w_dn)


def _combine_kernel(cur0_ref, cur1_ref, next0_ref, next1_ref, h_ref, rf_ref, y_ref, o_ref, ya_ref, yb_ref, sems):
    i = pl.program_id(0)
    nt = pl.num_programs(0)
    tm, d = h_ref.shape
    half = d // 2
    slot = i % 2

    def issue(d0_ref, d1_ref, s):
        def body(tb, carry):
            for j in range(DMA_UNROLL):
                t = tb * DMA_UNROLL + j
                pltpu.make_async_copy(y_ref.at[pl.ds(d0_ref[0, 0, t], 1)], ya_ref.at[s, pl.ds(t, 1)], sems.at[s]).start()
                pltpu.make_async_copy(y_ref.at[pl.ds(d1_ref[0, 0, t], 1)], yb_ref.at[s, pl.ds(t, 1)], sems.at[s]).start()
            return carry

        lax.fori_loop(0, tm // DMA_UNROLL, body, 0)

    pl.when(i == 0)(lambda: issue(cur0_ref, cur1_ref, 0))
    pl.when(i + 1 < nt)(lambda: issue(next0_ref, next1_ref, 1 - slot))
    pltpu.make_async_copy(y_ref.at[pl.ds(0, tm)], ya_ref.at[slot], sems.at[slot]).wait()
    pltpu.make_async_copy(y_ref.at[pl.ds(0, tm)], yb_ref.at[slot], sems.at[slot]).wait()
    g1 = rf_ref[:, 0:1]
    g2 = rf_ref[:, 1:2]
    a_lo, a_hi = _unpack_bf16_pair(ya_ref[slot])
    b_lo, b_hi = _unpack_bf16_pair(yb_ref[slot])
    o_ref[:, :half] = h_ref[:, :half] + (a_lo * g1 + b_lo * g2)
    o_ref[:, half:] = h_ref[:, half:] + (a_hi * g1 + b_hi * g2)


def _combine(h2, rf, dest, ys, tm=256):
    n, d = h2.shape
    nt = n // tm
    half = d // 2
    cur = pl.BlockSpec((1, 1, tm), lambda i: (i, 0, 0), memory_space=pltpu.SMEM)
    nxt = pl.BlockSpec((1, 1, tm), lambda i: (jnp.minimum(i + 1, nt - 1), 0, 0), memory_space=pltpu.SMEM)
    d0 = dest[0].reshape(nt, 1, tm)
    d1 = dest[1].reshape(nt, 1, tm)
    return pl.pallas_call(
        _combine_kernel,
        grid=(nt,),
        in_specs=[
            cur, cur, nxt, nxt,
            pl.BlockSpec((tm, d), lambda i: (i, 0)),
            pl.BlockSpec((tm, LANES), lambda i: (i, 0)),
            pl.BlockSpec(memory_space=pl.ANY),
        ],
        out_specs=pl.BlockSpec((tm, d), lambda i: (i, 0)),
        out_shape=jax.ShapeDtypeStruct((n, d), F32),
        scratch_shapes=[pltpu.VMEM((2, tm, half), jnp.uint32), pltpu.VMEM((2, tm, half), jnp.uint32),
                        pltpu.SemaphoreType.DMA((2,))],
        compiler_params=_cparams(("arbitrary",)),
        name="combine",
    )(d0, d1, d0, d1, h2, rf, ys)


def kernel(x, mem, rel_bias, norm_mix_g, w_in, conv_dw_w, conv_dw_b, conv_ln_g, conv_ln_b, w_conv_out, q_norm_g, k_norm_g, attn_sinks, w_attn_out, w_mix_out, norm_x_g, norm_mem_g, w_xq, w_xkv, xq_norm_g, xk_norm_g, w_xo, norm_moe_g, w_router_group, b_router_group, w_router_expert, b_router_expert, w_expert_gu, w_expert_down):
    batch, seq, d = x.shape
    mem_len = mem.shape[1]
    n = batch * seq
    h = x.reshape(n, d)
    for l in range(norm_mix_g.shape[0]):
        w = w_in[l].astype(BF16)
        q_gain = jnp.tile(q_norm_g[l].astype(F32) * HEAD_DIM ** -0.5, N_Q_HEADS)
        gain = jnp.concatenate([q_gain, jnp.tile(k_norm_g[l].astype(F32), N_KV_HEADS), jnp.ones((KV_WIDTH,), F32)])
        flag = jnp.concatenate([jnp.ones((ATTN_WIDTH + KV_WIDTH,), F32), jnp.zeros((KV_WIDTH,), F32)])
        conv_act, qkv, gates = _in_proj(h, norm_mix_g[l], w, gain, flag, conv_dw_w[l].reshape(CONV_WIDTH, CONV_CH),
                                        conv_dw_b[l], conv_ln_g[l], conv_ln_b[l], seq)
        attn_o = _swa(qkv, rel_bias, attn_sinks[l], batch, seq)
        h = _mix(conv_act, attn_o, gates, h, w_conv_out[l].astype(BF16), w_attn_out[l].astype(BF16),
                 w_mix_out[l].astype(BF16))

        kgain = jnp.concatenate([jnp.tile(xk_norm_g[l].astype(F32), X_HEADS), jnp.ones((X_WIDTH,), F32)])
        kflag = jnp.concatenate([jnp.ones((X_WIDTH,), F32), jnp.zeros((X_WIDTH,), F32)])
        memkv = _mem_proj(mem.reshape(batch * mem_len, d), norm_mem_g[l], w_xkv[l].astype(BF16), kgain, kflag)
        xq_gain = jnp.tile(xq_norm_g[l].astype(F32) * X_HEAD_DIM ** -0.5, X_HEADS).reshape(1, -1)
        w_r = jnp.concatenate([w_router_group[l], w_router_expert[l]], axis=1).astype(F32)
        w_r = jnp.pad(w_r, ((0, 0), (0, LANES - w_r.shape[1])))
        wr_hi = w_r.astype(BF16)
        wr = jnp.concatenate([wr_hi, (w_r - wr_hi.astype(F32)).astype(BF16)], axis=1)
        b_r = jnp.concatenate([b_router_group[l], b_router_expert[l]]).astype(F32)
        b_r = jnp.pad(b_r, (0, LANES - b_r.shape[0])).reshape(1, LANES)
        h2, xp, ri, rf, cnt = _xattn_route(h, memkv, norm_x_g[l], w_xq[l].astype(BF16), xq_gain, w_xo[l].astype(BF16),
                                           norm_moe_g[l], wr, b_r, seq, mem_len)

        counts = cnt[-1, 0, N_GROUPS:N_GROUPS + N_EXPERTS]
        padded = (counts + DISPATCH_BLOCK - 1) // DISPATCH_BLOCK * DISPATCH_BLOCK
        pad_end = jnp.cumsum(padded)
        n_blocks = -(-(2 * n) // DISPATCH_BLOCK) + N_EXPERTS
        rows = n_blocks * DISPATCH_BLOCK
        nused = (pad_end[-1] // DISPATCH_BLOCK).astype(jnp.int32)
        blk = jnp.minimum(jnp.arange(n_blocks, dtype=jnp.int32), nused - 1)
        blk_e = jnp.minimum(jnp.sum(pad_end[None, :] <= (blk * DISPATCH_BLOCK)[:, None], axis=1), N_EXPERTS - 1).astype(jnp.int32)
        dest = _dest_rows(ri, padded.astype(jnp.int32))

        tail = jnp.concatenate([jnp.where(counts > 0, pad_end - DISPATCH_BLOCK, -1), nused.reshape(1)]).astype(jnp.int32)
        xs = _dispatch(xp, dest, tail, rows)
        first = jnp.concatenate([jnp.ones((1,), jnp.int32), (blk_e[1:] != blk_e[:-1]).astype(jnp.int32)])
        eids = jnp.arange(N_EXPERTS, dtype=jnp.int32)
        later = (eids[None, :] > blk_e[:, None]) & (counts[None, :] > 0)
        next_e = jnp.min(jnp.where(later, eids[None, :], N_EXPERTS), axis=1)
        next_e = jnp.where(next_e < N_EXPERTS, next_e, -1).astype(jnp.int32)
        ys = _experts(xs, w_expert_gu[l], w_expert_down[l], blk_e, blk, nused.reshape(1), first, next_e)
        h = _combine(h2, rf, dest, ys)
    return h.reshape(batch, seq, d)
```

```python
import functools
import math

import jax
import jax.numpy as jnp
import numpy as np
from jax import lax
from jax.experimental import pallas as pl
from jax.experimental.pallas import tpu as pltpu

EPS = 1e-6
NEG = -1e30

CONV_CH = 1024
CONV_WIDTH = 31
N_Q_HEADS = 16
N_KV_HEADS = 2
HEAD_DIM = 64
ATTN_WIDTH = N_Q_HEADS * HEAD_DIM
KV_WIDTH = N_KV_HEADS * HEAD_DIM
WINDOW = 128
NUM_BUCKETS = 32
MAX_DISTANCE = 128
X_HEADS = 4
X_HEAD_DIM = 128
X_WIDTH = X_HEADS * X_HEAD_DIM
N_GROUPS = 4
EXPERTS_PER_GROUP = 8
N_EXPERTS = N_GROUPS * EXPERTS_PER_GROUP
D_EXPERT = 512
DISPATCH_BLOCK = 256

LANES = 128
SUBLANES = 8
MXU_K = 256
DMA_UNROLL = 8
CONV_HALO = 32
VMEM_LIMIT = 56 * 1024 * 1024

BF16 = jnp.bfloat16
F32 = jnp.float32


def _cparams(sem):
    return pltpu.CompilerParams(dimension_semantics=sem, vmem_limit_bytes=VMEM_LIMIT)


def _const_spec(shape):
    nd = len(shape)
    return pl.BlockSpec(shape, lambda *_: (0,) * nd, pipeline_mode=pl.Buffered(1))


def _rms_scale(x):
    return lax.rsqrt(jnp.mean(x * x, axis=-1, keepdims=True) + EPS)


def _group_sumsq(y, bd):
    sq = y * y
    hi = sq.astype(BF16)
    lo = (sq - hi.astype(F32)).astype(BF16)
    return (jnp.dot(hi, bd, preferred_element_type=F32) + jnp.dot(lo, bd, preferred_element_type=F32))


def _pack_bf16_pair(lo, hi):
    lo_bits = lax.bitcast_convert_type(lo.astype(BF16).astype(F32), jnp.uint32)
    hi_bits = lax.bitcast_convert_type(hi.astype(BF16).astype(F32), jnp.uint32)
    return (lo_bits >> 16) | hi_bits


def _unpack_bf16_pair(w):
    return (lax.bitcast_convert_type(w << 16, F32), lax.bitcast_convert_type(w & jnp.uint32(0xFFFF0000), F32))


def _head_norm_store(y, gain_ref, flag_ref, bd, o_ref, head_dim):
    for c in range(y.shape[1] // LANES):
        sl = slice(c * LANES, (c + 1) * LANES)
        yc = y[:, sl]
        ss = _group_sumsq(yc, bd)
        normed = yc * lax.rsqrt(ss * (1.0 / head_dim) + EPS) * gain_ref[:, sl]
        o_ref[:, sl] = jnp.where(flag_ref[:, sl] > 0.0, normed, yc).astype(o_ref.dtype)


def _opaque_zero(dep):
    bits = lax.shift_right_logical(lax.bitcast_convert_type(dep, jnp.uint32), jnp.uint32(32))
    return lax.bitcast_convert_type(bits, F32)[0:1, :]


def _conv_chunk(c, ext_ref, sh_ref, cv_ref, dww_ref, dwb_ref, row_chunk=128, after=None):
    tm = cv_ref.shape[0]
    off = CONV_HALO - (CONV_WIDTH - 1)
    span = tm + CONV_HALO - SUBLANES
    sl = slice(c * LANES, (c + 1) * LANES)
    for r in range(1, SUBLANES):
        sh_ref[c % 2, r - 1] = ext_ref[r:r + span, sl]
    zero = None if after is None else _opaque_zero(after)
    for rc in range(tm // row_chunk):
        acc = jnp.broadcast_to(dwb_ref[:, sl] if zero is None else dwb_ref[:, sl] + zero, (row_chunk, LANES))
        for j in range(CONV_WIDTH):
            q, r = divmod(off + j, SUBLANES)
            lo = SUBLANES * q + rc * row_chunk
            win = ext_ref[lo:lo + row_chunk, sl] if r == 0 else sh_ref[c % 2, r - 1, lo:lo + row_chunk, :]
            wj = dww_ref[j:j + 1, sl] if zero is None else dww_ref[j:j + 1, sl] + zero
            acc = acc + wj * win
        cv_ref[rc * row_chunk:(rc + 1) * row_chunk, sl] = acc


def _ln_silu(cv_ref, lng_ref, lnb_ref, act_ref):
    cv = cv_ref[...]
    mu = jnp.mean(cv, axis=-1, keepdims=True)
    cen = cv - mu
    var = jnp.mean(cen * cen, axis=-1, keepdims=True)
    ln = cen * lax.rsqrt(var + EPS) * lng_ref[...] + lnb_ref[...]
    act_ref[...] = (ln * jax.nn.sigmoid(ln)).astype(act_ref.dtype)


def _in_proj_kernel(x_ref, g_ref, w_ref, gain_ref, flag_ref, bd_ref, dww_ref, dwb_ref, lng_ref, lnb_ref,
                    act_ref, qkv_ref, gate_ref, xn_ref, ext_ref, sh_ref, cv_ref, *, tn, tiles_per_seq):
    tm = x_ref.shape[0]
    x = x_ref[...]
    xn_ref[...] = (x * _rms_scale(x) * g_ref[...]).astype(BF16)

    def proj(lo, width):
        return jnp.dot(xn_ref[...], w_ref[:, lo:lo + width], preferred_element_type=F32)

    @pl.when(pl.program_id(0) % tiles_per_seq == 0)
    def _():
        ext_ref[0:CONV_HALO, :] = jnp.zeros((CONV_HALO, CONV_CH), F32)

    n_u = act_ref.shape[1]
    n_qkv = qkv_ref.shape[1]
    g0 = 2 * n_u + n_qkv
    n_conv = CONV_CH // LANES
    n_gate = gate_ref.shape[1] // tn
    assert n_u == 2 * tn

    def proj_after(lo, after):
        xn = xn_ref[...]
        if after is not None:
            zb = jnp.broadcast_to(_opaque_zero(after), (SUBLANES, LANES)).astype(BF16)
            zb = jnp.tile(zb, (tm // SUBLANES, MXU_K // LANES))
            xn = jnp.concatenate([xn[:, :MXU_K] + zb, xn[:, MXU_K:]], axis=1)
        return jnp.dot(xn, w_ref[:, lo:lo + tn], preferred_element_type=F32)

    ext_ref[CONV_HALO:, 0:tn] = proj(0, tn) * jax.nn.sigmoid(proj(n_u, tn))
    stages = [("a", tn), ("b", n_u + tn)] + [("gate", g0 + c * tn) for c in range(n_gate)]
    last_rows = (slice(tm - SUBLANES, tm), slice(tn - LANES, tn))
    mxu_done = None
    conv_done = None
    glu_a = None
    for c, (kind, lo) in enumerate(stages):
        if c < n_conv:
            _conv_chunk(c, ext_ref, sh_ref, cv_ref, dww_ref, dwb_ref, after=mxu_done)
        y = proj_after(lo, conv_done if c < n_conv else None)
        mxu_done = y[last_rows]
        if kind == "a":
            glu_a = y
        elif kind == "b":
            ext_ref[CONV_HALO:, tn:2 * tn] = glu_a * jax.nn.sigmoid(y)
        else:
            col = lo - g0
            gate_ref[:, col:col + tn] = jax.nn.sigmoid(y).astype(gate_ref.dtype)
        if c < n_conv:
            conv_done = cv_ref[tm - SUBLANES:tm, c * LANES:(c + 1) * LANES]
    ext_ref[0:CONV_HALO, :] = ext_ref[tm:tm + CONV_HALO, :]
    _ln_silu(cv_ref, lng_ref, lnb_ref, act_ref)
    _head_norm_store(proj(2 * n_u, n_qkv), gain_ref, flag_ref, bd_ref[...], qkv_ref, HEAD_DIM)


def _in_proj(x, g, w, gain, flag, dw_w, dw_b, ln_g, ln_b, seq, tm=256, tn=512):
    m, d = x.shape
    n_qkv = ATTN_WIDTH + 2 * KV_WIDTH
    n_gate = w.shape[1] - 2 * CONV_CH - n_qkv
    assert seq % tm == 0
    row = lambda i: (i, 0)
    vec = _const_spec((1, CONV_CH))
    return pl.pallas_call(
        functools.partial(_in_proj_kernel, tn=tn, tiles_per_seq=seq // tm),
        grid=(m // tm,),
        in_specs=[pl.BlockSpec((tm, d), row), _const_spec((1, d)), _const_spec(w.shape), _const_spec((1, n_qkv)),
                  _const_spec((1, n_qkv)), _const_spec((LANES, LANES)), _const_spec((CONV_WIDTH, CONV_CH)), vec, vec, vec],
        out_specs=[pl.BlockSpec((tm, CONV_CH), row), pl.BlockSpec((tm, n_qkv), row), pl.BlockSpec((tm, n_gate), row)],
        out_shape=[jax.ShapeDtypeStruct((m, CONV_CH), BF16), jax.ShapeDtypeStruct((m, n_qkv), BF16),
                   jax.ShapeDtypeStruct((m, n_gate), BF16)],
        scratch_shapes=[pltpu.VMEM((tm, d), BF16), pltpu.VMEM((tm + CONV_HALO, CONV_CH), F32),
                        pltpu.VMEM((2, SUBLANES - 1, tm + CONV_HALO - SUBLANES, LANES), F32), pltpu.VMEM((tm, CONV_CH), F32)],
        compiler_params=_cparams(("arbitrary",)),
        name="in_proj",
    )(x, g.reshape(1, d), w, gain.reshape(1, -1), flag.reshape(1, -1), _block_diag_ones(HEAD_DIM),
      dw_w, dw_b.reshape(1, -1), ln_g.reshape(1, -1), ln_b.reshape(1, -1))


def _mem_proj_kernel(x_ref, g_ref, w_ref, gain_ref, flag_ref, bd_ref, o_ref):
    x = x_ref[...]
    xn = (x * _rms_scale(x) * g_ref[...]).astype(BF16)
    y = jnp.dot(xn, w_ref[...], preferred_element_type=F32)
    _head_norm_store(y, gain_ref, flag_ref, bd_ref[...], o_ref, X_HEAD_DIM)


def _mem_proj(x, g, w, gain, flag, tm=256):
    m, d = x.shape
    n = w.shape[1]
    return pl.pallas_call(
        _mem_proj_kernel,
        grid=(m // tm,),
        in_specs=[pl.BlockSpec((tm, d), lambda i: (i, 0)), _const_spec((1, d)), _const_spec(w.shape), _const_spec((1, n)),
                  _const_spec((1, n)), _const_spec((LANES, LANES))],
        out_specs=pl.BlockSpec((tm, n), lambda i: (i, 0)),
        out_shape=jax.ShapeDtypeStruct((m, n), BF16),
        compiler_params=_cparams(("parallel",)),
        name="mem_proj",
    )(x, g.reshape(1, d), w, gain.reshape(1, -1), flag.reshape(1, -1), _block_diag_ones(X_HEAD_DIM))


def _block_diag_ones(group):
    r = np.arange(LANES)
    return jnp.asarray((r[:, None] // group) == (r[None, :] // group), dtype=BF16)


def _swa_kernel(sink_ref, q_ref, kp_ref, kc_ref, vp_ref, vc_ref, bias_ref, o_ref, *, n_chunks):
    first = pl.program_id(1) == 0
    qb = q_ref.shape[0]
    lane = lax.broadcasted_iota(jnp.int32, (qb, LANES), 1)
    lane2 = lax.broadcasted_iota(jnp.int32, (2 * qb, LANES), 1)
    row = lax.broadcasted_iota(jnp.int32, (2 * qb, 1), 0)
    col = lax.broadcasted_iota(jnp.int32, (2 * qb, 2 * qb), 1)

    def both_halves(prev_ref, cur_ref):
        t = jnp.concatenate([prev_ref[...], cur_ref[...]], axis=0).astype(F32)
        swapped = pltpu.roll(t, HEAD_DIM, 1)
        return (jnp.where(lane2 < HEAD_DIM, t, swapped).astype(BF16), jnp.where(lane2 < HEAD_DIM, swapped, t).astype(BF16))

    ks = both_halves(kp_ref, kc_ref)
    vs = both_halves(vp_ref, vc_ref)
    no_prev = jnp.where(jnp.logical_and(first, col < qb), NEG, 0.0)
    chunks_per_kv = n_chunks // N_KV_HEADS
    for c in range(n_chunks):
        q2 = q_ref[:, c * LANES:(c + 1) * LANES]
        zero = jnp.zeros_like(q2)
        qs = jnp.concatenate([jnp.where(lane < HEAD_DIM, q2, zero), jnp.where(lane >= HEAD_DIM, q2, zero)], axis=0)
        s = lax.dot_general(qs, ks[c // chunks_per_kv], (((1,), (1,)), ((), ())), preferred_element_type=F32)
        s = s + bias_ref[c] + no_prev
        sink = jnp.where(row < qb, sink_ref[2 * c], sink_ref[2 * c + 1])
        m = jnp.maximum(jnp.max(s, axis=-1, keepdims=True), sink)
        p = jnp.exp(s - m)
        den = jnp.sum(p, axis=-1, keepdims=True) + jnp.exp(sink - m)
        o2 = jnp.dot(p.astype(BF16), vs[c // chunks_per_kv], preferred_element_type=F32) / den
        o = jnp.where(lane < HEAD_DIM, o2[:qb], o2[qb:])
        o_ref[:, c * LANES:(c + 1) * LANES] = o.astype(o_ref.dtype)


def _t5_bucket_np(dist):
    n = np.maximum(dist, 0)
    max_exact = NUM_BUCKETS // 2
    large = max_exact + (np.log(np.maximum(n, 1).astype(np.float32) / max_exact)
                         / math.log(MAX_DISTANCE / max_exact) * (NUM_BUCKETS - max_exact)).astype(np.int32)
    large = np.minimum(large, NUM_BUCKETS - 1)
    return np.where(n < max_exact, n, large)


def _swa(qkv, rel_bias, sinks, batch, seq):
    qb = WINDOW
    nb = seq // qb
    n_chunks = N_Q_HEADS // 2
    qi = np.arange(qb)[:, None]
    kj = np.arange(2 * qb)[None, :]
    dist = qi + qb - kj
    valid = (dist >= 0) & (dist < WINDOW)
    onehot = jnp.asarray(_t5_bucket_np(dist)[:, :, None] == np.arange(NUM_BUCKETS), dtype=F32)
    bias = jnp.einsum('qkb,bh->qkh', onehot, rel_bias.astype(F32), precision=lax.Precision.HIGHEST)
    bias = jnp.where(valid[:, :, None], bias, NEG).transpose(2, 0, 1)
    bias = bias.reshape(n_chunks, 2 * qb, 2 * qb)
    kcol = ATTN_WIDTH // LANES
    vcol = kcol + 1

    def prev(b, n):
        return (b * nb + jnp.maximum(n - 1, 0))

    return pl.pallas_call(
        functools.partial(_swa_kernel, n_chunks=n_chunks),
        grid=(batch, nb),
        in_specs=[
            pl.BlockSpec(memory_space=pltpu.SMEM),
            pl.BlockSpec((qb, ATTN_WIDTH), lambda b, n: (b * nb + n, 0)),
            pl.BlockSpec((qb, LANES), lambda b, n: (prev(b, n), kcol)),
            pl.BlockSpec((qb, LANES), lambda b, n: (b * nb + n, kcol)),
            pl.BlockSpec((qb, LANES), lambda b, n: (prev(b, n), vcol)),
            pl.BlockSpec((qb, LANES), lambda b, n: (b * nb + n, vcol)),
            _const_spec((n_chunks, 2 * qb, 2 * qb)),
        ],
        out_specs=pl.BlockSpec((qb, ATTN_WIDTH), lambda b, n: (b * nb + n, 0)),
        out_shape=jax.ShapeDtypeStruct((batch * seq, ATTN_WIDTH), BF16),
        compiler_params=_cparams(("parallel", "parallel")),
        name="swa",
    )(sinks.astype(F32), qkv, qkv, qkv, qkv, qkv, bias)


def _mix_kernel(c_ref, a_ref, g0_ref, g1_ref, x_ref, wc_ref, wa_ref, wm_ref, o_ref):
    y_conv = jnp.dot(c_ref[...], wc_ref[...], preferred_element_type=F32)
    y_attn = jnp.dot(a_ref[...], wa_ref[...], preferred_element_type=F32)
    merged = g0_ref[...].astype(F32) * y_conv + g1_ref[...].astype(F32) * y_attn
    o_ref[...] = x_ref[...] + jnp.dot(merged.astype(BF16), wm_ref[...], preferred_element_type=F32)


def _mix(conv_act, attn_o, gates, x, w_conv_out, w_attn_out, w_mix_out, tm=256):
    n, d = x.shape
    return pl.pallas_call(
        _mix_kernel,
        grid=(n // tm,),
        in_specs=[
            pl.BlockSpec((tm, CONV_CH), lambda i: (i, 0)),
            pl.BlockSpec((tm, ATTN_WIDTH), lambda i: (i, 0)),
            pl.BlockSpec((tm, d), lambda i: (i, 0)),
            pl.BlockSpec((tm, d), lambda i: (i, 1)),
            pl.BlockSpec((tm, d), lambda i: (i, 0)),
            _const_spec((CONV_CH, d)),
            _const_spec((ATTN_WIDTH, d)),
            _const_spec((d, d)),
        ],
        out_specs=pl.BlockSpec((tm, d), lambda i: (i, 0)),
        out_shape=jax.ShapeDtypeStruct((n, d), F32),
        compiler_params=_cparams(("parallel",)),
        name="mix",
    )(conv_act, attn_o, gates, gates, x, w_conv_out, w_attn_out, w_mix_out)


def _xattn_route_kernel(h_ref, gx_ref, wq_ref, qgain_ref, bd_ref, k_ref, v_ref, wo_ref, gm_ref,
                        wr_ref, br_ref, tri_ref,
                        h2_ref, xp_ref, ri_ref, rf_ref, cnt_ref, run_ref):
    tm, d = h_ref.shape
    half = d // 2

    @pl.when(pl.program_id(0) == 0)
    def _():
        run_ref[...] = jnp.zeros_like(run_ref)

    def attend(rows):
        h = h_ref[rows, :]
        xn = (h * _rms_scale(h) * gx_ref[...]).astype(BF16)
        q = jnp.dot(xn, wq_ref[...], preferred_element_type=F32)
        heads = []
        for hd in range(X_HEADS):
            sl = slice(hd * X_HEAD_DIM, (hd + 1) * X_HEAD_DIM)
            qh = q[:, sl]
            qh = (qh * lax.rsqrt(_group_sumsq(qh, bd_ref[...]) * (1.0 / X_HEAD_DIM) + EPS) * qgain_ref[:, sl]).astype(BF16)
            s = lax.dot_general(qh, k_ref[:, sl], (((1,), (1,)), ((), ())), preferred_element_type=F32)
            p = jnp.exp(s - jnp.max(s, axis=-1, keepdims=True))
            den = jnp.sum(p, axis=-1, keepdims=True)
            heads.append((jnp.dot(p.astype(BF16), v_ref[:, sl], preferred_element_type=F32) / den).astype(BF16))
        o = jnp.concatenate(heads, axis=-1)
        h2 = h + jnp.dot(o, wo_ref[...], preferred_element_type=F32)
        h2_ref[rows, :] = h2

        xn2 = h2 * _rms_scale(h2) * gm_ref[...]
        xp_ref[rows, :] = _pack_bf16_pair(xn2[:, :half], xn2[:, half:])

        x_hi = xn2.astype(BF16)
        x_lo = (xn2 - x_hi.astype(F32)).astype(BF16)
        r = jnp.dot(jnp.concatenate([x_hi, x_lo], axis=0), wr_ref[...], preferred_element_type=F32)
        n_r = xn2.shape[0]
        return (r[:n_r, :LANES] + r[:n_r, LANES:]) + (r[n_r:, :LANES] + r[n_r:, LANES:]) + br_ref[...]

    lg = attend(slice(0, tm))

    lane = lax.broadcasted_iota(jnp.int32, (tm, LANES), 1)
    lane_f = lane.astype(F32)
    big = jnp.float32(LANES)

    def first_argmax(vals, vmax):
        idx = jnp.min(jnp.where(vals == vmax, lane_f, big), axis=-1, keepdims=True)
        return idx.astype(jnp.int32)

    glog = jnp.where(lane < N_GROUPS, lg, -jnp.inf)
    gmax = jnp.max(glog, axis=-1, keepdims=True)
    grp = first_argmax(glog, gmax)
    pg_top = 1.0 / jnp.sum(jnp.exp(glog - gmax), axis=-1, keepdims=True)
    elane = lane - N_GROUPS
    emask = (elane >= 0) & (elane < N_EXPERTS) & ((elane >> 3) == grp)
    elog = jnp.where(emask, lg, -jnp.inf)
    m1 = jnp.max(elog, axis=-1, keepdims=True)
    i1 = first_argmax(elog, m1)
    elog2 = jnp.where(lane == i1, -jnp.inf, elog)
    m2 = jnp.max(elog2, axis=-1, keepdims=True)
    i2 = first_argmax(elog2, m2)
    den = jnp.sum(jnp.exp(elog - m1), axis=-1, keepdims=True)
    p1 = 1.0 / den
    p2 = jnp.exp(m2 - m1) / den
    psum = p1 + p2
    g1 = pg_top * (p1 / psum)
    g2 = pg_top * (p2 / psum)
    e1 = i1 - N_GROUPS
    e2 = i2 - N_GROUPS

    onehot = jnp.where((elane == e1) | (elane == e2), 1.0, 0.0)
    before = jnp.dot(tri_ref[...], onehot.astype(BF16), preferred_element_type=F32) + run_ref[0:1, :]
    r1 = jnp.sum(jnp.where(elane == e1, before, 0.0), axis=-1, keepdims=True).astype(jnp.int32)
    r2 = jnp.sum(jnp.where(elane == e2, before, 0.0), axis=-1, keepdims=True).astype(jnp.int32)
    run = run_ref[...] + jnp.sum(onehot, axis=0, keepdims=True)
    run_ref[...] = run
    cnt_ref[0] = run.astype(jnp.int32)

    ri = jnp.where(lane == 0, e1, jnp.where(lane == 1, e2, jnp.where(lane == 2, r1, jnp.where(lane == 3, r2, 0))))
    ri_ref[0] = jnp.transpose(ri)[0:SUBLANES, :]
    rf_ref[...] = jnp.where(lane == 0, g1, jnp.where(lane == 1, g2, 0.0))


def _xattn_route(h1, memkv, gx, w_xq, q_gain, w_xo, gm, wr, b_r, seq, mem_len, tm=512):
    n, d = h1.shape
    tiles_per_seq = seq // tm
    nt = n // tm
    tri = jnp.asarray(np.tril(np.ones((tm, tm), np.float32), -1), dtype=BF16)
    bd = _block_diag_ones(X_HEAD_DIM)
    return pl.pallas_call(
        _xattn_route_kernel,
        grid=(nt,),
        in_specs=[
            pl.BlockSpec((tm, d), lambda i: (i, 0)),
            _const_spec((1, d)),
            _const_spec((d, X_WIDTH)),
            _const_spec((1, X_WIDTH)),
            _const_spec((LANES, LANES)),
            pl.BlockSpec((mem_len, X_WIDTH), lambda i: (i // tiles_per_seq, 0)),
            pl.BlockSpec((mem_len, X_WIDTH), lambda i: (i // tiles_per_seq, 1)),
            _const_spec((X_WIDTH, d)),
            _const_spec((1, d)),
            _const_spec((d, 2 * LANES)),
            _const_spec((1, LANES)),
            _const_spec((tm, tm)),
        ],
        out_specs=[
            pl.BlockSpec((tm, d), lambda i: (i, 0)),
            pl.BlockSpec((tm, d // 2), lambda i: (i, 0)),
            pl.BlockSpec((1, SUBLANES, tm), lambda i: (i, 0, 0)),
            pl.BlockSpec((tm, LANES), lambda i: (i, 0)),
            pl.BlockSpec((1, 8, LANES), lambda i: (i, 0, 0)),
        ],
        out_shape=[
            jax.ShapeDtypeStruct((n, d), F32),
            jax.ShapeDtypeStruct((n, d // 2), jnp.uint32),
            jax.ShapeDtypeStruct((nt, SUBLANES, tm), jnp.int32),
            jax.ShapeDtypeStruct((n, LANES), F32),
            jax.ShapeDtypeStruct((nt, 8, LANES), jnp.int32),
        ],
        scratch_shapes=[pltpu.VMEM((8, LANES), F32)],
        compiler_params=_cparams(("arbitrary",)),
        name="xattn_route",
    )(h1, gx.reshape(1, d), w_xq, q_gain, bd, memkv, memkv, w_xo, gm.reshape(1, d), wr, b_r, tri)


def _dest_rows_kernel(padded_ref, ri_ref, d0_ref, d1_ref):
    for k, d_ref in enumerate((d0_ref, d1_ref)):
        e = ri_ref[:, k, :]
        row = ri_ref[:, 2 + k, :]
        for j in range(N_EXPERTS - 1):
            row = row + jnp.where(e > j, padded_ref[j], 0)
        d_ref[...] = row


def _dest_rows(ri, padded):
    nt, _, tm = ri.shape
    out = jax.ShapeDtypeStruct((nt, tm), jnp.int32)
    return pl.pallas_call(
        _dest_rows_kernel,
        in_specs=[pl.BlockSpec(memory_space=pltpu.SMEM), pl.BlockSpec(memory_space=pltpu.VMEM)],
        out_specs=[pl.BlockSpec(memory_space=pltpu.VMEM)] * 2,
        out_shape=[out, out],
        name="dest_rows",
    )(padded, ri)


def _dispatch_kernel(tail_ref, d0_ref, d1_ref, x_ref, xs_ref, zero_ref, sem, zsem):
    tm = x_ref.shape[0]

    @pl.when(pl.program_id(0) == 0)
    def _():
        zero_ref[...] = jnp.zeros_like(zero_ref)

        def zero_copy(e):
            start = pl.multiple_of(tail_ref[e], DISPATCH_BLOCK)
            return pltpu.make_async_copy(zero_ref, xs_ref.at[pl.ds(start, DISPATCH_BLOCK)], zsem)

        for e in range(N_EXPERTS):
            pl.when(tail_ref[e] >= 0)(lambda e=e: zero_copy(e).start())
        for e in range(N_EXPERTS):
            pl.when(tail_ref[e] >= 0)(lambda e=e: zero_copy(e).wait())

        def unused_copy(b):
            return pltpu.make_async_copy(zero_ref, xs_ref.at[pl.ds(pl.multiple_of(b * DISPATCH_BLOCK, DISPATCH_BLOCK),
                                                                   DISPATCH_BLOCK)], zsem)

        n_blocks = xs_ref.shape[0] // DISPATCH_BLOCK
        lax.fori_loop(tail_ref[N_EXPERTS], n_blocks, lambda b, c: (unused_copy(b).start(), c)[1], 0)
        lax.fori_loop(tail_ref[N_EXPERTS], n_blocks, lambda b, c: (unused_copy(b).wait(), c)[1], 0)

    def copy(t, k):
        return pltpu.make_async_copy(x_ref.at[pl.ds(t, 1)], xs_ref.at[pl.ds((d0_ref, d1_ref)[k][0, 0, t], 1)], sem)

    def issue(tb, carry):
        for j in range(DMA_UNROLL):
            copy(tb * DMA_UNROLL + j, 0).start(priority=0)
            copy(tb * DMA_UNROLL + j, 1).start(priority=1)
        return carry

    lax.fori_loop(0, tm // DMA_UNROLL, issue, 0)

    def drain(tb, carry):
        for j in range(DMA_UNROLL):
            copy(tb * DMA_UNROLL + j, 0).wait()
            copy(tb * DMA_UNROLL + j, 1).wait()
        return carry

    lax.fori_loop(0, tm // DMA_UNROLL, drain, 0)


def _dispatch(xp, dest, tail, rows):
    n, w = xp.shape
    nt, tm = dest[0].shape
    grid_spec = pltpu.PrefetchScalarGridSpec(
        num_scalar_prefetch=1,
        grid=(nt,),
        in_specs=[
            pl.BlockSpec((1, 1, tm), lambda i, tail: (i, 0, 0), memory_space=pltpu.SMEM),
            pl.BlockSpec((1, 1, tm), lambda i, tail: (i, 0, 0), memory_space=pltpu.SMEM),
            pl.BlockSpec((tm, w), lambda i, tail: (i, 0)),
        ],
        out_specs=pl.BlockSpec(memory_space=pl.ANY),
        scratch_shapes=[pltpu.VMEM((DISPATCH_BLOCK, w), xp.dtype), pltpu.SemaphoreType.DMA(()),
                        pltpu.SemaphoreType.DMA(())],
    )
    return pl.pallas_call(
        _dispatch_kernel,
        grid_spec=grid_spec,
        out_shape=jax.ShapeDtypeStruct((rows, w), xp.dtype),
        compiler_params=_cparams(("arbitrary",)),
        name="dispatch",
    )(tail, dest[0].reshape(nt, 1, tm), dest[1].reshape(nt, 1, tm), xp)


def _experts_kernel(blk_e_ref, blk_x_ref, nused_ref, first_ref, next_e_ref, xs_ref, wgu_hbm, wdn_hbm, y_ref,
                    wgu_f32, wdn_f32, wgu_bf, wdn_bf, sems):
    b = pl.program_id(0)
    half = xs_ref.shape[1]

    def fetch(e):
        return (pltpu.make_async_copy(wgu_hbm.at[e], wgu_f32, sems.at[0]),
                pltpu.make_async_copy(wdn_hbm.at[e], wdn_f32, sems.at[1]))

    @pl.when(b == 0)
    def _():
        for c in fetch(blk_e_ref[0]):
            c.start()

    @pl.when(first_ref[b] == 1)
    def _():
        for c in fetch(blk_e_ref[b]):
            c.wait()
        wgu_bf[...] = wgu_f32[...].astype(BF16)
        wdn_bf[...] = wdn_f32[...].astype(BF16)

        @pl.when(next_e_ref[b] >= 0)
        def _():
            for c in fetch(next_e_ref[b]):
                c.start()

    @pl.when(b < nused_ref[0])
    def _():
        x_lo, x_hi = _unpack_bf16_pair(xs_ref[...])
        gu = (jnp.dot(x_lo.astype(BF16), wgu_bf[0:half, :], preferred_element_type=F32)
              + jnp.dot(x_hi.astype(BF16), wgu_bf[half:, :], preferred_element_type=F32))
        g = gu[:, :D_EXPERT]
        u = gu[:, D_EXPERT:]
        act = (g * jax.nn.sigmoid(g) * u).astype(BF16)
        y = jnp.dot(act, wdn_bf[...], preferred_element_type=F32)
        y_ref[...] = _pack_bf16_pair(y[:, :half], y[:, half:])

    @pl.when(b >= nused_ref[0])
    def _():
        y_ref[...] = jnp.zeros_like(y_ref)


def _experts(xs, w_gu, w_dn, blk_e, blk_x, nused, first, next_e):
    rows, half = xs.shape
    d = 2 * half
    n_blocks = rows // DISPATCH_BLOCK
    grid_spec = pltpu.PrefetchScalarGridSpec(
        num_scalar_prefetch=5,
        grid=(n_blocks,),
        in_specs=[
            pl.BlockSpec((DISPATCH_BLOCK, half), lambda b, be, bx, *_: (bx[b], 0)),
            pl.BlockSpec(memory_space=pl.ANY),
            pl.BlockSpec(memory_space=pl.ANY),
        ],
        out_specs=pl.BlockSpec((DISPATCH_BLOCK, half), lambda b, *_: (b, 0)),
        scratch_shapes=[pltpu.VMEM((d, 2 * D_EXPERT), F32), pltpu.VMEM((D_EXPERT, d), F32),
                        pltpu.VMEM((d, 2 * D_EXPERT), BF16), pltpu.VMEM((D_EXPERT, d), BF16),
                        pltpu.SemaphoreType.DMA((2,))],
    )
    return pl.pallas_call(
        _experts_kernel,
        grid_spec=grid_spec,
        out_shape=jax.ShapeDtypeStruct((rows, half), jnp.uint32),
        compiler_params=_cparams(("arbitrary",)),
        name="experts",
    )(blk_e, blk_x, nused, first, next_e, xs, w_gu, w_dn)


def _combine_kernel(cur0_ref, cur1_ref, next0_ref, next1_ref, h_ref, rf_ref, y_ref, o_ref, ya_ref, yb_ref, sems):
    i = pl.program_id(0)
    nt = pl.num_programs(0)
    tm, d = h_ref.shape
    half = d // 2
    slot = i % 2

    def issue(d0_ref, d1_ref, s):
        def body(tb, carry):
            for j in range(DMA_UNROLL):
                t = tb * DMA_UNROLL + j
                pltpu.make_async_copy(y_ref.at[pl.ds(d0_ref[0, 0, t], 1)], ya_ref.at[s, pl.ds(t, 1)],
                                      sems.at[s]).start(priority=0)
                pltpu.make_async_copy(y_ref.at[pl.ds(d1_ref[0, 0, t], 1)], yb_ref.at[s, pl.ds(t, 1)],
                                      sems.at[s]).start(priority=1)
            return carry

        lax.fori_loop(0, tm // DMA_UNROLL, body, 0)

    pl.when(i == 0)(lambda: issue(cur0_ref, cur1_ref, 0))
    pl.when(i + 1 < nt)(lambda: issue(next0_ref, next1_ref, 1 - slot))
    pltpu.make_async_copy(y_ref.at[pl.ds(0, tm)], ya_ref.at[slot], sems.at[slot]).wait()
    pltpu.make_async_copy(y_ref.at[pl.ds(0, tm)], yb_ref.at[slot], sems.at[slot]).wait()
    g1 = rf_ref[:, 0:1]
    g2 = rf_ref[:, 1:2]
    a_lo, a_hi = _unpack_bf16_pair(ya_ref[slot])
    b_lo, b_hi = _unpack_bf16_pair(yb_ref[slot])
    o_ref[:, :half] = h_ref[:, :half] + (a_lo * g1 + b_lo * g2)
    o_ref[:, half:] = h_ref[:, half:] + (a_hi * g1 + b_hi * g2)


def _combine(h2, rf, dest, ys, tm=256):
    n, d = h2.shape
    nt = n // tm
    half = d // 2
    cur = pl.BlockSpec((1, 1, tm), lambda i: (i, 0, 0), memory_space=pltpu.SMEM)
    nxt = pl.BlockSpec((1, 1, tm), lambda i: (jnp.minimum(i + 1, nt - 1), 0, 0), memory_space=pltpu.SMEM)
    d0 = dest[0].reshape(nt, 1, tm)
    d1 = dest[1].reshape(nt, 1, tm)
    return pl.pallas_call(
        _combine_kernel,
        grid=(nt,),
        in_specs=[
            cur, cur, nxt, nxt,
            pl.BlockSpec((tm, d), lambda i: (i, 0)),
            pl.BlockSpec((tm, LANES), lambda i: (i, 0)),
            pl.BlockSpec(memory_space=pl.ANY),
        ],
        out_specs=pl.BlockSpec((tm, d), lambda i: (i, 0)),
        out_shape=jax.ShapeDtypeStruct((n, d), F32),
        scratch_shapes=[pltpu.VMEM((2, tm, half), jnp.uint32), pltpu.VMEM((2, tm, half), jnp.uint32),
                        pltpu.SemaphoreType.DMA((2,))],
        compiler_params=_cparams(("arbitrary",)),
        name="combine",
    )(d0, d1, d0, d1, h2, rf, ys)


def kernel(x, mem, rel_bias, norm_mix_g, w_in, conv_dw_w, conv_dw_b, conv_ln_g, conv_ln_b, w_conv_out, q_norm_g, k_norm_g, attn_sinks, w_attn_out, w_mix_out, norm_x_g, norm_mem_g, w_xq, w_xkv, xq_norm_g, xk_norm_g, w_xo, norm_moe_g, w_router_group, b_router_group, w_router_expert, b_router_expert, w_expert_gu, w_expert_down):
    batch, seq, d = x.shape
    mem_len = mem.shape[1]
    n = batch * seq
    h = x.reshape(n, d)
    for l in range(norm_mix_g.shape[0]):
        w = w_in[l].astype(BF16)
        q_gain = jnp.tile(q_norm_g[l].astype(F32) * HEAD_DIM ** -0.5, N_Q_HEADS)
        gain = jnp.concatenate([q_gain, jnp.tile(k_norm_g[l].astype(F32), N_KV_HEADS), jnp.ones((KV_WIDTH,), F32)])
        flag = jnp.concatenate([jnp.ones((ATTN_WIDTH + KV_WIDTH,), F32), jnp.zeros((KV_WIDTH,), F32)])
        conv_act, qkv, gates = _in_proj(h, norm_mix_g[l], w, gain, flag, conv_dw_w[l].reshape(CONV_WIDTH, CONV_CH),
                                        conv_dw_b[l], conv_ln_g[l], conv_ln_b[l], seq)
        attn_o = _swa(qkv, rel_bias, attn_sinks[l], batch, seq)
        h = _mix(conv_act, attn_o, gates, h, w_conv_out[l].astype(BF16), w_attn_out[l].astype(BF16),
                 w_mix_out[l].astype(BF16))

        kgain = jnp.concatenate([jnp.tile(xk_norm_g[l].astype(F32), X_HEADS), jnp.ones((X_WIDTH,), F32)])
        kflag = jnp.concatenate([jnp.ones((X_WIDTH,), F32), jnp.zeros((X_WIDTH,), F32)])
        memkv = _mem_proj(mem.reshape(batch * mem_len, d), norm_mem_g[l], w_xkv[l].astype(BF16), kgain, kflag)
        xq_gain = jnp.tile(xq_norm_g[l].astype(F32) * X_HEAD_DIM ** -0.5, X_HEADS).reshape(1, -1)
        w_r = jnp.concatenate([w_router_group[l], w_router_expert[l]], axis=1).astype(F32)
        w_r = jnp.pad(w_r, ((0, 0), (0, LANES - w_r.shape[1])))
        wr_hi = w_r.astype(BF16)
        wr = jnp.concatenate([wr_hi, (w_r - wr_hi.astype(F32)).astype(BF16)], axis=1)
        b_r = jnp.concatenate([b_router_group[l], b_router_expert[l]]).astype(F32)
        b_r = jnp.pad(b_r, (0, LANES - b_r.shape[0])).reshape(1, LANES)
        h2, xp, ri, rf, cnt = _xattn_route(h, memkv, norm_x_g[l], w_xq[l].astype(BF16), xq_gain, w_xo[l].astype(BF16),
                                           norm_moe_g[l], wr, b_r, seq, mem_len)

        counts = cnt[-1, 0, N_GROUPS:N_GROUPS + N_EXPERTS]
        padded = (counts + DISPATCH_BLOCK - 1) // DISPATCH_BLOCK * DISPATCH_BLOCK
        pad_end = jnp.cumsum(padded)
        n_blocks = -(-(2 * n) // DISPATCH_BLOCK) + N_EXPERTS
        rows = n_blocks * DISPATCH_BLOCK
        nused = (pad_end[-1] // DISPATCH_BLOCK).astype(jnp.int32)
        blk = jnp.minimum(jnp.arange(n_blocks, dtype=jnp.int32), nused - 1)
        blk_e = jnp.minimum(jnp.sum(pad_end[None, :] <= (blk * DISPATCH_BLOCK)[:, None], axis=1), N_EXPERTS - 1).astype(jnp.int32)
        dest = _dest_rows(ri, padded.astype(jnp.int32))

        tail = jnp.concatenate([jnp.where(counts > 0, pad_end - DISPATCH_BLOCK, -1), nused.reshape(1)]).astype(jnp.int32)
        xs = _dispatch(xp, dest, tail, rows)
        first = jnp.concatenate([jnp.ones((1,), jnp.int32), (blk_e[1:] != blk_e[:-1]).astype(jnp.int32)])
        eids = jnp.arange(N_EXPERTS, dtype=jnp.int32)
        later = (eids[None, :] > blk_e[:, None]) & (counts[None, :] > 0)
        next_e = jnp.min(jnp.where(later, eids[None, :], N_EXPERTS), axis=1)
        next_e = jnp.where(next_e < N_EXPERTS, next_e, -1).astype(jnp.int32)
        ys = _experts(xs, w_expert_gu[l], w_expert_down[l], blk_e, blk, nused.reshape(1), first, next_e)
        h = _combine(h2, rf, dest, ys)
    return h.reshape(batch, seq, d)
```

```python
import functools
import math

import jax
import jax.numpy as jnp
import numpy as np
from jax import lax
from jax.experimental import pallas as pl
from jax.experimental.pallas import tpu as pltpu

EPS = 1e-6
NEG = -1e30

CONV_CH = 1024
CONV_WIDTH = 31
N_Q_HEADS = 16
N_KV_HEADS = 2
HEAD_DIM = 64
ATTN_WIDTH = N_Q_HEADS * HEAD_DIM
KV_WIDTH = N_KV_HEADS * HEAD_DIM
WINDOW = 128
NUM_BUCKETS = 32
MAX_DISTANCE = 128
X_HEADS = 4
X_HEAD_DIM = 128
X_WIDTH = X_HEADS * X_HEAD_DIM
N_GROUPS = 4
EXPERTS_PER_GROUP = 8
N_EXPERTS = N_GROUPS * EXPERTS_PER_GROUP
D_EXPERT = 512
DISPATCH_BLOCK = 256

LANES = 128
SUBLANES = 8
MXU_K = 256
DMA_UNROLL = 8
CONV_HALO = 32
VMEM_LIMIT = 56 * 1024 * 1024

BF16 = jnp.bfloat16
F32 = jnp.float32


def _cparams(sem):
    return pltpu.CompilerParams(dimension_semantics=sem, vmem_limit_bytes=VMEM_LIMIT)


def _const_spec(shape):
    nd = len(shape)
    return pl.BlockSpec(shape, lambda *_: (0,) * nd, pipeline_mode=pl.Buffered(1))


def _rms_scale(x):
    return lax.rsqrt(jnp.mean(x * x, axis=-1, keepdims=True) + EPS)


def _group_sumsq(y, bd):
    sq = y * y
    hi = sq.astype(BF16)
    lo = (sq - hi.astype(F32)).astype(BF16)
    return (jnp.dot(hi, bd, preferred_element_type=F32) + jnp.dot(lo, bd, preferred_element_type=F32))


def _pack_bf16_pair(lo, hi):
    lo_bits = lax.bitcast_convert_type(lo.astype(BF16).astype(F32), jnp.uint32)
    hi_bits = lax.bitcast_convert_type(hi.astype(BF16).astype(F32), jnp.uint32)
    return (lo_bits >> 16) | hi_bits


def _store_row_tiles(ref, idx, value):
    for s_ in range(SUBLANES):
        ref[idx + (slice(None), s_, slice(None))] = value[:, s_ * LANES:(s_ + 1) * LANES]


def _load_row_tiles(ref, idx):
    return jnp.concatenate([ref[idx + (slice(None), s_, slice(None))] for s_ in range(SUBLANES)], axis=-1)


def _unpack_bf16_pair(w):
    return (lax.bitcast_convert_type(w << 16, F32), lax.bitcast_convert_type(w & jnp.uint32(0xFFFF0000), F32))


def _head_norm_store(y, gain_ref, flag_ref, bd, o_ref, head_dim):
    for c in range(y.shape[1] // LANES):
        sl = slice(c * LANES, (c + 1) * LANES)
        yc = y[:, sl]
        ss = _group_sumsq(yc, bd)
        normed = yc * lax.rsqrt(ss * (1.0 / head_dim) + EPS) * gain_ref[:, sl]
        o_ref[:, sl] = jnp.where(flag_ref[:, sl] > 0.0, normed, yc).astype(o_ref.dtype)


def _opaque_zero(dep):
    bits = lax.shift_right_logical(lax.bitcast_convert_type(dep, jnp.uint32), jnp.uint32(32))
    return lax.bitcast_convert_type(bits, F32)[0:1, :]


def _conv_chunk(c, ext_ref, sh_ref, cv_ref, dww_ref, dwb_ref, row_chunk=128, after=None):
    tm = cv_ref.shape[0]
    off = CONV_HALO - (CONV_WIDTH - 1)
    span = tm + CONV_HALO - SUBLANES
    sl = slice(c * LANES, (c + 1) * LANES)
    for r in range(1, SUBLANES):
        sh_ref[c % 2, r - 1] = ext_ref[r:r + span, sl]
    zero = None if after is None else _opaque_zero(after)
    for rc in range(tm // row_chunk):
        acc = jnp.broadcast_to(dwb_ref[:, sl] if zero is None else dwb_ref[:, sl] + zero, (row_chunk, LANES))
        for j in range(CONV_WIDTH):
            q, r = divmod(off + j, SUBLANES)
            lo = SUBLANES * q + rc * row_chunk
            win = ext_ref[lo:lo + row_chunk, sl] if r == 0 else sh_ref[c % 2, r - 1, lo:lo + row_chunk, :]
            wj = dww_ref[j:j + 1, sl] if zero is None else dww_ref[j:j + 1, sl] + zero
            acc = acc + wj * win
        cv_ref[rc * row_chunk:(rc + 1) * row_chunk, sl] = acc


def _ln_silu(cv_ref, lng_ref, lnb_ref, act_ref):
    cv = cv_ref[...]
    mu = jnp.mean(cv, axis=-1, keepdims=True)
    cen = cv - mu
    var = jnp.mean(cen * cen, axis=-1, keepdims=True)
    ln = cen * lax.rsqrt(var + EPS) * lng_ref[...] + lnb_ref[...]
    act_ref[...] = (ln * jax.nn.sigmoid(ln)).astype(act_ref.dtype)


def _in_proj_kernel(x_ref, g_ref, w_ref, gain_ref, flag_ref, bd_ref, dww_ref, dwb_ref, lng_ref, lnb_ref,
                    act_ref, qkv_ref, gate_ref, xn_ref, ext_ref, sh_ref, cv_ref, *, tn, tiles_per_seq):
    tm = x_ref.shape[0]
    x = x_ref[...]
    xn_ref[...] = (x * _rms_scale(x) * g_ref[...]).astype(BF16)

    def proj(lo, width):
        return jnp.dot(xn_ref[...], w_ref[:, lo:lo + width], preferred_element_type=F32)

    @pl.when(pl.program_id(0) % tiles_per_seq == 0)
    def _():
        ext_ref[0:CONV_HALO, :] = jnp.zeros((CONV_HALO, CONV_CH), F32)

    n_u = act_ref.shape[1]
    n_qkv = qkv_ref.shape[1]
    g0 = 2 * n_u + n_qkv
    n_conv = CONV_CH // LANES
    n_gate = gate_ref.shape[1] // tn
    assert n_u == 2 * tn

    def proj_after(lo, after):
        xn = xn_ref[...]
        if after is not None:
            zb = jnp.broadcast_to(_opaque_zero(after), (SUBLANES, LANES)).astype(BF16)
            zb = jnp.tile(zb, (tm // SUBLANES, MXU_K // LANES))
            xn = jnp.concatenate([xn[:, :MXU_K] + zb, xn[:, MXU_K:]], axis=1)
        return jnp.dot(xn, w_ref[:, lo:lo + tn], preferred_element_type=F32)

    ext_ref[CONV_HALO:, 0:tn] = proj(0, tn) * jax.nn.sigmoid(proj(n_u, tn))
    stages = [("a", tn), ("b", n_u + tn)] + [("gate", g0 + c * tn) for c in range(n_gate)]
    last_rows = (slice(tm - SUBLANES, tm), slice(tn - LANES, tn))
    mxu_done = None
    conv_done = None
    glu_a = None
    for c, (kind, lo) in enumerate(stages):
        if c < n_conv:
            _conv_chunk(c, ext_ref, sh_ref, cv_ref, dww_ref, dwb_ref, after=mxu_done)
        y = proj_after(lo, conv_done if c < n_conv else None)
        mxu_done = y[last_rows]
        if kind == "a":
            glu_a = y
        elif kind == "b":
            ext_ref[CONV_HALO:, tn:2 * tn] = glu_a * jax.nn.sigmoid(y)
        else:
            col = lo - g0
            gate_ref[:, col:col + tn] = jax.nn.sigmoid(y).astype(gate_ref.dtype)
        if c < n_conv:
            conv_done = cv_ref[tm - SUBLANES:tm, c * LANES:(c + 1) * LANES]
    ext_ref[0:CONV_HALO, :] = ext_ref[tm:tm + CONV_HALO, :]
    _ln_silu(cv_ref, lng_ref, lnb_ref, act_ref)
    _head_norm_store(proj(2 * n_u, n_qkv), gain_ref, flag_ref, bd_ref[...], qkv_ref, HEAD_DIM)


def _in_proj(x, g, w, gain, flag, dw_w, dw_b, ln_g, ln_b, seq, tm=256, tn=512):
    m, d = x.shape
    n_qkv = ATTN_WIDTH + 2 * KV_WIDTH
    n_gate = w.shape[1] - 2 * CONV_CH - n_qkv
    assert seq % tm == 0
    row = lambda i: (i, 0)
    vec = _const_spec((1, CONV_CH))
    return pl.pallas_call(
        functools.partial(_in_proj_kernel, tn=tn, tiles_per_seq=seq // tm),
        grid=(m // tm,),
        in_specs=[pl.BlockSpec((tm, d), row), _const_spec((1, d)), _const_spec(w.shape), _const_spec((1, n_qkv)),
                  _const_spec((1, n_qkv)), _const_spec((LANES, LANES)), _const_spec((CONV_WIDTH, CONV_CH)), vec, vec, vec],
        out_specs=[pl.BlockSpec((tm, CONV_CH), row), pl.BlockSpec((tm, n_qkv), row), pl.BlockSpec((tm, n_gate), row)],
        out_shape=[jax.ShapeDtypeStruct((m, CONV_CH), BF16), jax.ShapeDtypeStruct((m, n_qkv), BF16),
                   jax.ShapeDtypeStruct((m, n_gate), BF16)],
        scratch_shapes=[pltpu.VMEM((tm, d), BF16), pltpu.VMEM((tm + CONV_HALO, CONV_CH), F32),
                        pltpu.VMEM((2, SUBLANES - 1, tm + CONV_HALO - SUBLANES, LANES), F32), pltpu.VMEM((tm, CONV_CH), F32)],
        compiler_params=_cparams(("arbitrary",)),
        name="in_proj",
    )(x, g.reshape(1, d), w, gain.reshape(1, -1), flag.reshape(1, -1), _block_diag_ones(HEAD_DIM),
      dw_w, dw_b.reshape(1, -1), ln_g.reshape(1, -1), ln_b.reshape(1, -1))


def _mem_proj_kernel(x_ref, g_ref, w_ref, gain_ref, flag_ref, bd_ref, o_ref):
    x = x_ref[...]
    xn = (x * _rms_scale(x) * g_ref[...]).astype(BF16)
    y = jnp.dot(xn, w_ref[...], preferred_element_type=F32)
    _head_norm_store(y, gain_ref, flag_ref, bd_ref[...], o_ref, X_HEAD_DIM)


def _mem_proj(x, g, w, gain, flag, tm=256):
    m, d = x.shape
    n = w.shape[1]
    return pl.pallas_call(
        _mem_proj_kernel,
        grid=(m // tm,),
        in_specs=[pl.BlockSpec((tm, d), lambda i: (i, 0)), _const_spec((1, d)), _const_spec(w.shape), _const_spec((1, n)),
                  _const_spec((1, n)), _const_spec((LANES, LANES))],
        out_specs=pl.BlockSpec((tm, n), lambda i: (i, 0)),
        out_shape=jax.ShapeDtypeStruct((m, n), BF16),
        compiler_params=_cparams(("parallel",)),
        name="mem_proj",
    )(x, g.reshape(1, d), w, gain.reshape(1, -1), flag.reshape(1, -1), _block_diag_ones(X_HEAD_DIM))


def _block_diag_ones(group):
    r = np.arange(LANES)
    return jnp.asarray((r[:, None] // group) == (r[None, :] // group), dtype=BF16)


def _swa_kernel(sink_ref, q_ref, kp_ref, kc_ref, vp_ref, vc_ref, bias_ref, o_ref, *, n_chunks):
    first = pl.program_id(1) == 0
    qb = q_ref.shape[0]
    lane = lax.broadcasted_iota(jnp.int32, (qb, LANES), 1)
    lane2 = lax.broadcasted_iota(jnp.int32, (2 * qb, LANES), 1)
    row = lax.broadcasted_iota(jnp.int32, (2 * qb, 1), 0)
    col = lax.broadcasted_iota(jnp.int32, (2 * qb, 2 * qb), 1)

    def both_halves(prev_ref, cur_ref):
        t = jnp.concatenate([prev_ref[...], cur_ref[...]], axis=0).astype(F32)
        swapped = pltpu.roll(t, HEAD_DIM, 1)
        return (jnp.where(lane2 < HEAD_DIM, t, swapped).astype(BF16), jnp.where(lane2 < HEAD_DIM, swapped, t).astype(BF16))

    ks = both_halves(kp_ref, kc_ref)
    vs = both_halves(vp_ref, vc_ref)
    no_prev = jnp.where(jnp.logical_and(first, col < qb), NEG, 0.0)
    chunks_per_kv = n_chunks // N_KV_HEADS
    for c in range(n_chunks):
        q2 = q_ref[:, c * LANES:(c + 1) * LANES]
        zero = jnp.zeros_like(q2)
        qs = jnp.concatenate([jnp.where(lane < HEAD_DIM, q2, zero), jnp.where(lane >= HEAD_DIM, q2, zero)], axis=0)
        s = lax.dot_general(qs, ks[c // chunks_per_kv], (((1,), (1,)), ((), ())), preferred_element_type=F32)
        s = s + bias_ref[c] + no_prev
        sink = jnp.where(row < qb, sink_ref[2 * c], sink_ref[2 * c + 1])
        m = jnp.maximum(jnp.max(s, axis=-1, keepdims=True), sink)
        p = jnp.exp(s - m)
        den = jnp.sum(p, axis=-1, keepdims=True) + jnp.exp(sink - m)
        o2 = jnp.dot(p.astype(BF16), vs[c // chunks_per_kv], preferred_element_type=F32) / den
        o = jnp.where(lane < HEAD_DIM, o2[:qb], o2[qb:])
        o_ref[:, c * LANES:(c + 1) * LANES] = o.astype(o_ref.dtype)


def _t5_bucket_np(dist):
    n = np.maximum(dist, 0)
    max_exact = NUM_BUCKETS // 2
    large = max_exact + (np.log(np.maximum(n, 1).astype(np.float32) / max_exact)
                         / math.log(MAX_DISTANCE / max_exact) * (NUM_BUCKETS - max_exact)).astype(np.int32)
    large = np.minimum(large, NUM_BUCKETS - 1)
    return np.where(n < max_exact, n, large)


def _swa(qkv, rel_bias, sinks, batch, seq):
    qb = WINDOW
    nb = seq // qb
    n_chunks = N_Q_HEADS // 2
    qi = np.arange(qb)[:, None]
    kj = np.arange(2 * qb)[None, :]
    dist = qi + qb - kj
    valid = (dist >= 0) & (dist < WINDOW)
    onehot = jnp.asarray(_t5_bucket_np(dist)[:, :, None] == np.arange(NUM_BUCKETS), dtype=F32)
    bias = jnp.einsum('qkb,bh->qkh', onehot, rel_bias.astype(F32), precision=lax.Precision.HIGHEST)
    bias = jnp.where(valid[:, :, None], bias, NEG).transpose(2, 0, 1)
    bias = bias.reshape(n_chunks, 2 * qb, 2 * qb)
    kcol = ATTN_WIDTH // LANES
    vcol = kcol + 1

    def prev(b, n):
        return (b * nb + jnp.maximum(n - 1, 0))

    return pl.pallas_call(
        functools.partial(_swa_kernel, n_chunks=n_chunks),
        grid=(batch, nb),
        in_specs=[
            pl.BlockSpec(memory_space=pltpu.SMEM),
            pl.BlockSpec((qb, ATTN_WIDTH), lambda b, n: (b * nb + n, 0)),
            pl.BlockSpec((qb, LANES), lambda b, n: (prev(b, n), kcol)),
            pl.BlockSpec((qb, LANES), lambda b, n: (b * nb + n, kcol)),
            pl.BlockSpec((qb, LANES), lambda b, n: (prev(b, n), vcol)),
            pl.BlockSpec((qb, LANES), lambda b, n: (b * nb + n, vcol)),
            _const_spec((n_chunks, 2 * qb, 2 * qb)),
        ],
        out_specs=pl.BlockSpec((qb, ATTN_WIDTH), lambda b, n: (b * nb + n, 0)),
        out_shape=jax.ShapeDtypeStruct((batch * seq, ATTN_WIDTH), BF16),
        compiler_params=_cparams(("parallel", "parallel")),
        name="swa",
    )(sinks.astype(F32), qkv, qkv, qkv, qkv, qkv, bias)


def _mix_kernel(c_ref, a_ref, g0_ref, g1_ref, x_ref, wc_ref, wa_ref, wm_ref, o_ref):
    y_conv = jnp.dot(c_ref[...], wc_ref[...], preferred_element_type=F32)
    y_attn = jnp.dot(a_ref[...], wa_ref[...], preferred_element_type=F32)
    merged = g0_ref[...].astype(F32) * y_conv + g1_ref[...].astype(F32) * y_attn
    o_ref[...] = x_ref[...] + jnp.dot(merged.astype(BF16), wm_ref[...], preferred_element_type=F32)


def _mix(conv_act, attn_o, gates, x, w_conv_out, w_attn_out, w_mix_out, tm=256):
    n, d = x.shape
    return pl.pallas_call(
        _mix_kernel,
        grid=(n // tm,),
        in_specs=[
            pl.BlockSpec((tm, CONV_CH), lambda i: (i, 0)),
            pl.BlockSpec((tm, ATTN_WIDTH), lambda i: (i, 0)),
            pl.BlockSpec((tm, d), lambda i: (i, 0)),
            pl.BlockSpec((tm, d), lambda i: (i, 1)),
            pl.BlockSpec((tm, d), lambda i: (i, 0)),
            _const_spec((CONV_CH, d)),
            _const_spec((ATTN_WIDTH, d)),
            _const_spec((d, d)),
        ],
        out_specs=pl.BlockSpec((tm, d), lambda i: (i, 0)),
        out_shape=jax.ShapeDtypeStruct((n, d), F32),
        compiler_params=_cparams(("parallel",)),
        name="mix",
    )(conv_act, attn_o, gates, gates, x, w_conv_out, w_attn_out, w_mix_out)


def _xattn_route_kernel(h_ref, gx_ref, wq_ref, qgain_ref, bd_ref, k_ref, v_ref, wo_ref, gm_ref,
                        wr_ref, br_ref, tri_ref,
                        h2_ref, xp_ref, ri_ref, rf_ref, cnt_ref, run_ref):
    tm, d = h_ref.shape
    half = d // 2

    @pl.when(pl.program_id(0) == 0)
    def _():
        run_ref[...] = jnp.zeros_like(run_ref)

    def attend(rows):
        h = h_ref[rows, :]
        xn = (h * _rms_scale(h) * gx_ref[...]).astype(BF16)
        q = jnp.dot(xn, wq_ref[...], preferred_element_type=F32)
        heads = []
        for hd in range(X_HEADS):
            sl = slice(hd * X_HEAD_DIM, (hd + 1) * X_HEAD_DIM)
            qh = q[:, sl]
            qh = (qh * lax.rsqrt(_group_sumsq(qh, bd_ref[...]) * (1.0 / X_HEAD_DIM) + EPS) * qgain_ref[:, sl]).astype(BF16)
            s = lax.dot_general(qh, k_ref[:, sl], (((1,), (1,)), ((), ())), preferred_element_type=F32)
            p = jnp.exp(s - jnp.max(s, axis=-1, keepdims=True))
            den = jnp.sum(p, axis=-1, keepdims=True)
            heads.append((jnp.dot(p.astype(BF16), v_ref[:, sl], preferred_element_type=F32) / den).astype(BF16))
        o = jnp.concatenate(heads, axis=-1)
        h2 = h + jnp.dot(o, wo_ref[...], preferred_element_type=F32)
        h2_ref[rows, :] = h2

        xn2 = h2 * _rms_scale(h2) * gm_ref[...]
        _store_row_tiles(xp_ref, (), _pack_bf16_pair(xn2[:, :half], xn2[:, half:]))

        x_hi = xn2.astype(BF16)
        x_lo = (xn2 - x_hi.astype(F32)).astype(BF16)
        r = jnp.dot(jnp.concatenate([x_hi, x_lo], axis=0), wr_ref[...], preferred_element_type=F32)
        n_r = xn2.shape[0]
        return (r[:n_r, :LANES] + r[:n_r, LANES:]) + (r[n_r:, :LANES] + r[n_r:, LANES:]) + br_ref[...]

    lg = attend(slice(0, tm))

    lane = lax.broadcasted_iota(jnp.int32, (tm, LANES), 1)
    lane_f = lane.astype(F32)
    big = jnp.float32(LANES)

    def first_argmax(vals, vmax):
        idx = jnp.min(jnp.where(vals == vmax, lane_f, big), axis=-1, keepdims=True)
        return idx.astype(jnp.int32)

    glog = jnp.where(lane < N_GROUPS, lg, -jnp.inf)
    gmax = jnp.max(glog, axis=-1, keepdims=True)
    grp = first_argmax(glog, gmax)
    pg_top = 1.0 / jnp.sum(jnp.exp(glog - gmax), axis=-1, keepdims=True)
    elane = lane - N_GROUPS
    emask = (elane >= 0) & (elane < N_EXPERTS) & ((elane >> 3) == grp)
    elog = jnp.where(emask, lg, -jnp.inf)
    m1 = jnp.max(elog, axis=-1, keepdims=True)
    i1 = first_argmax(elog, m1)
    elog2 = jnp.where(lane == i1, -jnp.inf, elog)
    m2 = jnp.max(elog2, axis=-1, keepdims=True)
    i2 = first_argmax(elog2, m2)
    den = jnp.sum(jnp.exp(elog - m1), axis=-1, keepdims=True)
    p1 = 1.0 / den
    p2 = jnp.exp(m2 - m1) / den
    psum = p1 + p2
    g1 = pg_top * (p1 / psum)
    g2 = pg_top * (p2 / psum)
    e1 = i1 - N_GROUPS
    e2 = i2 - N_GROUPS

    onehot = jnp.where((elane == e1) | (elane == e2), 1.0, 0.0)
    before = jnp.dot(tri_ref[...], onehot.astype(BF16), preferred_element_type=F32) + run_ref[0:1, :]
    r1 = jnp.sum(jnp.where(elane == e1, before, 0.0), axis=-1, keepdims=True).astype(jnp.int32)
    r2 = jnp.sum(jnp.where(elane == e2, before, 0.0), axis=-1, keepdims=True).astype(jnp.int32)
    run = run_ref[...] + jnp.sum(onehot, axis=0, keepdims=True)
    run_ref[...] = run
    cnt_ref[0] = run.astype(jnp.int32)

    ri = jnp.where(lane == 0, e1, jnp.where(lane == 1, e2, jnp.where(lane == 2, r1, jnp.where(lane == 3, r2, 0))))
    ri_ref[0] = jnp.transpose(ri)[0:SUBLANES, :]
    rf_ref[...] = jnp.where(lane == 0, g1, jnp.where(lane == 1, g2, 0.0))


def _xattn_route(h1, memkv, gx, w_xq, q_gain, w_xo, gm, wr, b_r, seq, mem_len, tm=512):
    n, d = h1.shape
    tiles_per_seq = seq // tm
    nt = n // tm
    tri = jnp.asarray(np.tril(np.ones((tm, tm), np.float32), -1), dtype=BF16)
    bd = _block_diag_ones(X_HEAD_DIM)
    return pl.pallas_call(
        _xattn_route_kernel,
        grid=(nt,),
        in_specs=[
            pl.BlockSpec((tm, d), lambda i: (i, 0)),
            _const_spec((1, d)),
            _const_spec((d, X_WIDTH)),
            _const_spec((1, X_WIDTH)),
            _const_spec((LANES, LANES)),
            pl.BlockSpec((mem_len, X_WIDTH), lambda i: (i // tiles_per_seq, 0)),
            pl.BlockSpec((mem_len, X_WIDTH), lambda i: (i // tiles_per_seq, 1)),
            _const_spec((X_WIDTH, d)),
            _const_spec((1, d)),
            _const_spec((d, 2 * LANES)),
            _const_spec((1, LANES)),
            _const_spec((tm, tm)),
        ],
        out_specs=[
            pl.BlockSpec((tm, d), lambda i: (i, 0)),
            pl.BlockSpec((tm, SUBLANES, LANES), lambda i: (i, 0, 0)),
            pl.BlockSpec((1, SUBLANES, tm), lambda i: (i, 0, 0)),
            pl.BlockSpec((tm, LANES), lambda i: (i, 0)),
            pl.BlockSpec((1, 8, LANES), lambda i: (i, 0, 0)),
        ],
        out_shape=[
            jax.ShapeDtypeStruct((n, d), F32),
            jax.ShapeDtypeStruct((n, SUBLANES, LANES), jnp.uint32),
            jax.ShapeDtypeStruct((nt, SUBLANES, tm), jnp.int32),
            jax.ShapeDtypeStruct((n, LANES), F32),
            jax.ShapeDtypeStruct((nt, 8, LANES), jnp.int32),
        ],
        scratch_shapes=[pltpu.VMEM((8, LANES), F32)],
        compiler_params=_cparams(("arbitrary",)),
        name="xattn_route",
    )(h1, gx.reshape(1, d), w_xq, q_gain, bd, memkv, memkv, w_xo, gm.reshape(1, d), wr, b_r, tri)


def _dest_rows_kernel(padded_ref, ri_ref, d0_ref, d1_ref):
    for k, d_ref in enumerate((d0_ref, d1_ref)):
        e = ri_ref[:, k, :]
        row = ri_ref[:, 2 + k, :]
        for j in range(N_EXPERTS - 1):
            row = row + jnp.where(e > j, padded_ref[j], 0)
        d_ref[...] = row


def _dest_rows(ri, padded):
    nt, _, tm = ri.shape
    out = jax.ShapeDtypeStruct((nt, tm), jnp.int32)
    return pl.pallas_call(
        _dest_rows_kernel,
        in_specs=[pl.BlockSpec(memory_space=pltpu.SMEM), pl.BlockSpec(memory_space=pltpu.VMEM)],
        out_specs=[pl.BlockSpec(memory_space=pltpu.VMEM)] * 2,
        out_shape=[out, out],
        name="dest_rows",
    )(padded, ri)


def _dispatch_kernel(tail_ref, d0_ref, d1_ref, x_ref, xs_ref, zero_ref, sem, zsem):
    tm = x_ref.shape[0]

    @pl.when(pl.program_id(0) == 0)
    def _():
        zero_ref[...] = jnp.zeros_like(zero_ref)

        def zero_copy(e):
            start = pl.multiple_of(tail_ref[e], DISPATCH_BLOCK)
            return pltpu.make_async_copy(zero_ref, xs_ref.at[pl.ds(start, DISPATCH_BLOCK)], zsem)

        for e in range(N_EXPERTS):
            pl.when(tail_ref[e] >= 0)(lambda e=e: zero_copy(e).start())
        for e in range(N_EXPERTS):
            pl.when(tail_ref[e] >= 0)(lambda e=e: zero_copy(e).wait())

        def unused_copy(b):
            return pltpu.make_async_copy(zero_ref, xs_ref.at[pl.ds(pl.multiple_of(b * DISPATCH_BLOCK, DISPATCH_BLOCK),
                                                                   DISPATCH_BLOCK)], zsem)

        n_blocks = xs_ref.shape[0] // DISPATCH_BLOCK
        lax.fori_loop(tail_ref[N_EXPERTS], n_blocks, lambda b, c: (unused_copy(b).start(), c)[1], 0)
        lax.fori_loop(tail_ref[N_EXPERTS], n_blocks, lambda b, c: (unused_copy(b).wait(), c)[1], 0)

    def copy(t, k):
        return pltpu.make_async_copy(x_ref.at[t], xs_ref.at[(d0_ref, d1_ref)[k][0, 0, t]], sem)

    def issue(tb, carry):
        for j in range(DMA_UNROLL):
            copy(tb * DMA_UNROLL + j, 0).start()
            copy(tb * DMA_UNROLL + j, 1).start()
        return carry

    lax.fori_loop(0, tm // DMA_UNROLL, issue, 0)

    def drain(tb, carry):
        for j in range(DMA_UNROLL):
            copy(tb * DMA_UNROLL + j, 0).wait()
            copy(tb * DMA_UNROLL + j, 1).wait()
        return carry

    lax.fori_loop(0, tm // DMA_UNROLL, drain, 0)


def _dispatch(xp, dest, tail, rows):
    n = xp.shape[0]
    nt, tm = dest[0].shape
    grid_spec = pltpu.PrefetchScalarGridSpec(
        num_scalar_prefetch=1,
        grid=(nt,),
        in_specs=[
            pl.BlockSpec((1, 1, tm), lambda i, tail: (i, 0, 0), memory_space=pltpu.SMEM),
            pl.BlockSpec((1, 1, tm), lambda i, tail: (i, 0, 0), memory_space=pltpu.SMEM),
            pl.BlockSpec((tm,) + xp.shape[1:], lambda i, tail: (i, 0, 0)),
        ],
        out_specs=pl.BlockSpec(memory_space=pl.ANY),
        scratch_shapes=[pltpu.VMEM((DISPATCH_BLOCK,) + xp.shape[1:], xp.dtype), pltpu.SemaphoreType.DMA(()),
                        pltpu.SemaphoreType.DMA(())],
    )
    return pl.pallas_call(
        _dispatch_kernel,
        grid_spec=grid_spec,
        out_shape=jax.ShapeDtypeStruct((rows,) + xp.shape[1:], xp.dtype),
        compiler_params=_cparams(("arbitrary",)),
        name="dispatch",
    )(tail, dest[0].reshape(nt, 1, tm), dest[1].reshape(nt, 1, tm), xp)


def _experts_kernel(blk_e_ref, blk_x_ref, nused_ref, first_ref, next_e_ref, xs_ref, wgu_hbm, wdn_hbm, y_ref,
                    wgu_f32, wdn_f32, wgu_bf, wdn_bf, sems):
    b = pl.program_id(0)
    half = xs_ref.shape[1] * xs_ref.shape[2]

    def fetch(e):
        return (pltpu.make_async_copy(wgu_hbm.at[e], wgu_f32, sems.at[0]),
                pltpu.make_async_copy(wdn_hbm.at[e], wdn_f32, sems.at[1]))

    @pl.when(b == 0)
    def _():
        for c in fetch(blk_e_ref[0]):
            c.start()

    @pl.when(first_ref[b] == 1)
    def _():
        for c in fetch(blk_e_ref[b]):
            c.wait()
        wgu_bf[...] = wgu_f32[...].astype(BF16)
        wdn_bf[...] = wdn_f32[...].astype(BF16)

        @pl.when(next_e_ref[b] >= 0)
        def _():
            for c in fetch(next_e_ref[b]):
                c.start()

    @pl.when(b < nused_ref[0])
    def _():
        x_lo, x_hi = _unpack_bf16_pair(_load_row_tiles(xs_ref, ()))
        gu = (jnp.dot(x_lo.astype(BF16), wgu_bf[0:half, :], preferred_element_type=F32)
              + jnp.dot(x_hi.astype(BF16), wgu_bf[half:, :], preferred_element_type=F32))
        g = gu[:, :D_EXPERT]
        u = gu[:, D_EXPERT:]
        act = (g * jax.nn.sigmoid(g) * u).astype(BF16)
        y = jnp.dot(act, wdn_bf[...], preferred_element_type=F32)
        _store_row_tiles(y_ref, (), _pack_bf16_pair(y[:, :half], y[:, half:]))

    @pl.when(b >= nused_ref[0])
    def _():
        y_ref[...] = jnp.zeros_like(y_ref)


def _experts(xs, w_gu, w_dn, blk_e, blk_x, nused, first, next_e):
    rows = xs.shape[0]
    tile = xs.shape[1:]
    d = 2 * tile[0] * tile[1]
    n_blocks = rows // DISPATCH_BLOCK
    grid_spec = pltpu.PrefetchScalarGridSpec(
        num_scalar_prefetch=5,
        grid=(n_blocks,),
        in_specs=[
            pl.BlockSpec((DISPATCH_BLOCK,) + tile, lambda b, be, bx, *_: (bx[b], 0, 0)),
            pl.BlockSpec(memory_space=pl.ANY),
            pl.BlockSpec(memory_space=pl.ANY),
        ],
        out_specs=pl.BlockSpec((DISPATCH_BLOCK,) + tile, lambda b, *_: (b, 0, 0)),
        scratch_shapes=[pltpu.VMEM((d, 2 * D_EXPERT), F32), pltpu.VMEM((D_EXPERT, d), F32),
                        pltpu.VMEM((d, 2 * D_EXPERT), BF16), pltpu.VMEM((D_EXPERT, d), BF16),
                        pltpu.SemaphoreType.DMA((2,))],
    )
    return pl.pallas_call(
        _experts_kernel,
        grid_spec=grid_spec,
        out_shape=jax.ShapeDtypeStruct((rows,) + tile, jnp.uint32),
        compiler_params=_cparams(("arbitrary",)),
        name="experts",
    )(blk_e, blk_x, nused, first, next_e, xs, w_gu, w_dn)


def _combine_kernel(cur0_ref, cur1_ref, next0_ref, next1_ref, h_ref, rf_ref, y_ref, o_ref, ya_ref, yb_ref, sems):
    i = pl.program_id(0)
    nt = pl.num_programs(0)
    tm, d = h_ref.shape
    half = d // 2
    slot = i % 2

    def issue(d0_ref, d1_ref, s):
        def body(tb, carry):
            for j in range(DMA_UNROLL):
                t = tb * DMA_UNROLL + j
                pltpu.make_async_copy(y_ref.at[d0_ref[0, 0, t]], ya_ref.at[s, t], sems.at[s]).start()
                pltpu.make_async_copy(y_ref.at[d1_ref[0, 0, t]], yb_ref.at[s, t], sems.at[s]).start()
            return carry

        lax.fori_loop(0, tm // DMA_UNROLL, body, 0)

    pl.when(i == 0)(lambda: issue(cur0_ref, cur1_ref, 0))
    pl.when(i + 1 < nt)(lambda: issue(next0_ref, next1_ref, 1 - slot))
    pltpu.make_async_copy(y_ref.at[pl.ds(0, tm)], ya_ref.at[slot], sems.at[slot]).wait()
    pltpu.make_async_copy(y_ref.at[pl.ds(0, tm)], yb_ref.at[slot], sems.at[slot]).wait()
    g1 = rf_ref[:, 0:1]
    g2 = rf_ref[:, 1:2]
    a_lo, a_hi = _unpack_bf16_pair(_load_row_tiles(ya_ref, (slot,)))
    b_lo, b_hi = _unpack_bf16_pair(_load_row_tiles(yb_ref, (slot,)))
    o_ref[:, :half] = h_ref[:, :half] + (a_lo * g1 + b_lo * g2)
    o_ref[:, half:] = h_ref[:, half:] + (a_hi * g1 + b_hi * g2)


def _combine(h2, rf, dest, ys, tm=256):
    n, d = h2.shape
    nt = n // tm
    half = d // 2
    cur = pl.BlockSpec((1, 1, tm), lambda i: (i, 0, 0), memory_space=pltpu.SMEM)
    nxt = pl.BlockSpec((1, 1, tm), lambda i: (jnp.minimum(i + 1, nt - 1), 0, 0), memory_space=pltpu.SMEM)
    d0 = dest[0].reshape(nt, 1, tm)
    d1 = dest[1].reshape(nt, 1, tm)
    return pl.pallas_call(
        _combine_kernel,
        grid=(nt,),
        in_specs=[
            cur, cur, nxt, nxt,
            pl.BlockSpec((tm, d), lambda i: (i, 0)),
            pl.BlockSpec((tm, LANES), lambda i: (i, 0)),
            pl.BlockSpec(memory_space=pl.ANY),
        ],
        out_specs=pl.BlockSpec((tm, d), lambda i: (i, 0)),
        out_shape=jax.ShapeDtypeStruct((n, d), F32),
        scratch_shapes=[pltpu.VMEM((2, tm) + ys.shape[1:], jnp.uint32), pltpu.VMEM((2, tm) + ys.shape[1:], jnp.uint32),
                        pltpu.SemaphoreType.DMA((2,))],
        compiler_params=_cparams(("arbitrary",)),
        name="combine",
    )(d0, d1, d0, d1, h2, rf, ys)


def kernel(x, mem, rel_bias, norm_mix_g, w_in, conv_dw_w, conv_dw_b, conv_ln_g, conv_ln_b, w_conv_out, q_norm_g, k_norm_g, attn_sinks, w_attn_out, w_mix_out, norm_x_g, norm_mem_g, w_xq, w_xkv, xq_norm_g, xk_norm_g, w_xo, norm_moe_g, w_router_group, b_router_group, w_router_expert, b_router_expert, w_expert_gu, w_expert_down):
    batch, seq, d = x.shape
    mem_len = mem.shape[1]
    n = batch * seq
    h = x.reshape(n, d)
    for l in range(norm_mix_g.shape[0]):
        w = w_in[l].astype(BF16)
        q_gain = jnp.tile(q_norm_g[l].astype(F32) * HEAD_DIM ** -0.5, N_Q_HEADS)
        gain = jnp.concatenate([q_gain, jnp.tile(k_norm_g[l].astype(F32), N_KV_HEADS), jnp.ones((KV_WIDTH,), F32)])
        flag = jnp.concatenate([jnp.ones((ATTN_WIDTH + KV_WIDTH,), F32), jnp.zeros((KV_WIDTH,), F32)])
        conv_act, qkv, gates = _in_proj(h, norm_mix_g[l], w, gain, flag, conv_dw_w[l].reshape(CONV_WIDTH, CONV_CH),
                                        conv_dw_b[l], conv_ln_g[l], conv_ln_b[l], seq)
        attn_o = _swa(qkv, rel_bias, attn_sinks[l], batch, seq)
        h = _mix(conv_act, attn_o, gates, h, w_conv_out[l].astype(BF16), w_attn_out[l].astype(BF16),
                 w_mix_out[l].astype(BF16))

        kgain = jnp.concatenate([jnp.tile(xk_norm_g[l].astype(F32), X_HEADS), jnp.ones((X_WIDTH,), F32)])
        kflag = jnp.concatenate([jnp.ones((X_WIDTH,), F32), jnp.zeros((X_WIDTH,), F32)])
        memkv = _mem_proj(mem.reshape(batch * mem_len, d), norm_mem_g[l], w_xkv[l].astype(BF16), kgain, kflag)
        xq_gain = jnp.tile(xq_norm_g[l].astype(F32) * X_HEAD_DIM ** -0.5, X_HEADS).reshape(1, -1)
        w_r = jnp.concatenate([w_router_group[l], w_router_expert[l]], axis=1).astype(F32)
        w_r = jnp.pad(w_r, ((0, 0), (0, LANES - w_r.shape[1])))
        wr_hi = w_r.astype(BF16)
        wr = jnp.concatenate([wr_hi, (w_r - wr_hi.astype(F32)).astype(BF16)], axis=1)
        b_r = jnp.concatenate([b_router_group[l], b_router_expert[l]]).astype(F32)
        b_r = jnp.pad(b_r, (0, LANES - b_r.shape[0])).reshape(1, LANES)
        h2, xp, ri, rf, cnt = _xattn_route(h, memkv, norm_x_g[l], w_xq[l].astype(BF16), xq_gain, w_xo[l].astype(BF16),
                                           norm_moe_g[l], wr, b_r, seq, mem_len)

        counts = cnt[-1, 0, N_GROUPS:N_GROUPS + N_EXPERTS]
        padded = (counts + DISPATCH_BLOCK - 1) // DISPATCH_BLOCK * DISPATCH_BLOCK
        pad_end = jnp.cumsum(padded)
        n_blocks = -(-(2 * n) // DISPATCH_BLOCK) + N_EXPERTS
        rows = n_blocks * DISPATCH_BLOCK
        nused = (pad_end[-1] // DISPATCH_BLOCK).astype(jnp.int32)
        blk = jnp.minimum(jnp.arange(n_blocks, dtype=jnp.int32), nused - 1)
        blk_e = jnp.minimum(jnp.sum(pad_end[None, :] <= (blk * DISPATCH_BLOCK)[:, None], axis=1), N_EXPERTS - 1).astype(jnp.int32)
        dest = _dest_rows(ri, padded.astype(jnp.int32))

        tail = jnp.concatenate([jnp.where(counts > 0, pad_end - DISPATCH_BLOCK, -1), nused.reshape(1)]).astype(jnp.int32)
        xs = _dispatch(xp, dest, tail, rows)
        first = jnp.concatenate([jnp.ones((1,), jnp.int32), (blk_e[1:] != blk_e[:-1]).astype(jnp.int32)])
        eids = jnp.arange(N_EXPERTS, dtype=jnp.int32)
        later = (eids[None, :] > blk_e[:, None]) & (counts[None, :] > 0)
        next_e = jnp.min(jnp.where(later, eids[None, :], N_EXPERTS), axis=1)
        next_e = jnp.where(next_e < N_EXPERTS, next_e, -1).astype(jnp.int32)
        ys = _experts(xs, w_expert_gu[l], w_expert_down[l], blk_e, blk, nused.reshape(1), first, next_e)
        h = _combine(h2, rf, dest, ys)
    return h.reshape(batch, seq, d)
```

```python
import functools
import math

import jax
import jax.numpy as jnp
import numpy as np
from jax import lax
from jax.experimental import pallas as pl
from jax.experimental.pallas import tpu as pltpu

EPS = 1e-6
NEG = -1e30

CONV_CH = 1024
CONV_WIDTH = 31
N_Q_HEADS = 16
N_KV_HEADS = 2
HEAD_DIM = 64
ATTN_WIDTH = N_Q_HEADS * HEAD_DIM
KV_WIDTH = N_KV_HEADS * HEAD_DIM
WINDOW = 128
NUM_BUCKETS = 32
MAX_DISTANCE = 128
X_HEADS = 4
X_HEAD_DIM = 128
X_WIDTH = X_HEADS * X_HEAD_DIM
N_GROUPS = 4
EXPERTS_PER_GROUP = 8
N_EXPERTS = N_GROUPS * EXPERTS_PER_GROUP
D_EXPERT = 512
DISPATCH_BLOCK = 256

LANES = 128
SUBLANES = 8
MXU_K = 256
DMA_UNROLL = 8
CONV_HALO = 32
VMEM_LIMIT = 56 * 1024 * 1024

BF16 = jnp.bfloat16
F32 = jnp.float32


def _cparams(sem):
    return pltpu.CompilerParams(dimension_semantics=sem, vmem_limit_bytes=VMEM_LIMIT)


def _const_spec(shape):
    nd = len(shape)
    return pl.BlockSpec(shape, lambda *_: (0,) * nd, pipeline_mode=pl.Buffered(1))


def _rms_scale(x):
    return lax.rsqrt(jnp.mean(x * x, axis=-1, keepdims=True) + EPS)


def _group_sumsq(y, bd):
    sq = y * y
    hi = sq.astype(BF16)
    lo = (sq - hi.astype(F32)).astype(BF16)
    return (jnp.dot(hi, bd, preferred_element_type=F32) + jnp.dot(lo, bd, preferred_element_type=F32))


def _pack_bf16_pair(lo, hi):
    lo_bits = lax.bitcast_convert_type(lo.astype(BF16).astype(F32), jnp.uint32)
    hi_bits = lax.bitcast_convert_type(hi.astype(BF16).astype(F32), jnp.uint32)
    return (lo_bits >> 16) | hi_bits


def _unpack_bf16_pair(w):
    return (lax.bitcast_convert_type(w << 16, F32), lax.bitcast_convert_type(w & jnp.uint32(0xFFFF0000), F32))


def _head_norm_store(y, gain_ref, flag_ref, bd, o_ref, head_dim):
    for c in range(y.shape[1] // LANES):
        sl = slice(c * LANES, (c + 1) * LANES)
        yc = y[:, sl]
        ss = _group_sumsq(yc, bd)
        normed = yc * lax.rsqrt(ss * (1.0 / head_dim) + EPS) * gain_ref[:, sl]
        o_ref[:, sl] = jnp.where(flag_ref[:, sl] > 0.0, normed, yc).astype(o_ref.dtype)


def _opaque_zero(dep):
    bits = lax.shift_right_logical(lax.bitcast_convert_type(dep, jnp.uint32), jnp.uint32(32))
    return lax.bitcast_convert_type(bits, F32)[0:1, :]


def _conv_chunk(c, ext_ref, sh_ref, cv_ref, dww_ref, dwb_ref, row_chunk=128, after=None):
    tm = cv_ref.shape[0]
    off = CONV_HALO - (CONV_WIDTH - 1)
    span = tm + CONV_HALO - SUBLANES
    sl = slice(c * LANES, (c + 1) * LANES)
    for r in range(1, SUBLANES):
        sh_ref[c % 2, r - 1] = ext_ref[r:r + span, sl]
    zero = None if after is None else _opaque_zero(after)
    for rc in range(tm // row_chunk):
        acc = jnp.broadcast_to(dwb_ref[:, sl] if zero is None else dwb_ref[:, sl] + zero, (row_chunk, LANES))
        for j in range(CONV_WIDTH):
            q, r = divmod(off + j, SUBLANES)
            lo = SUBLANES * q + rc * row_chunk
            win = ext_ref[lo:lo + row_chunk, sl] if r == 0 else sh_ref[c % 2, r - 1, lo:lo + row_chunk, :]
            wj = dww_ref[j:j + 1, sl] if zero is None else dww_ref[j:j + 1, sl] + zero
            acc = acc + wj * win
        cv_ref[rc * row_chunk:(rc + 1) * row_chunk, sl] = acc


def _ln_silu(cv_ref, lng_ref, lnb_ref, act_ref):
    cv = cv_ref[...]
    mu = jnp.mean(cv, axis=-1, keepdims=True)
    cen = cv - mu
    var = jnp.mean(cen * cen, axis=-1, keepdims=True)
    ln = cen * lax.rsqrt(var + EPS) * lng_ref[...] + lnb_ref[...]
    act_ref[...] = (ln * jax.nn.sigmoid(ln)).astype(act_ref.dtype)


def _in_proj_kernel(x_ref, g_ref, w_ref, gain_ref, flag_ref, bd_ref, dww_ref, dwb_ref, lng_ref, lnb_ref,
                    act_ref, qkv_ref, gate_ref, xn_ref, ext_ref, sh_ref, cv_ref, *, tn, tiles_per_seq):
    tm = x_ref.shape[0]
    x = x_ref[...]
    xn_ref[...] = (x * _rms_scale(x) * g_ref[...]).astype(BF16)

    def proj(lo, width):
        return jnp.dot(xn_ref[...], w_ref[:, lo:lo + width], preferred_element_type=F32)

    @pl.when(pl.program_id(0) % tiles_per_seq == 0)
    def _():
        ext_ref[0:CONV_HALO, :] = jnp.zeros((CONV_HALO, CONV_CH), F32)

    n_u = act_ref.shape[1]
    n_qkv = qkv_ref.shape[1]
    g0 = 2 * n_u + n_qkv
    n_conv = CONV_CH // LANES
    n_gate = gate_ref.shape[1] // tn
    assert n_u == 2 * tn

    def proj_after(lo, after):
        xn = xn_ref[...]
        if after is not None:
            zb = jnp.broadcast_to(_opaque_zero(after), (SUBLANES, LANES)).astype(BF16)
            zb = jnp.tile(zb, (tm // SUBLANES, MXU_K // LANES))
            xn = jnp.concatenate([xn[:, :MXU_K] + zb, xn[:, MXU_K:]], axis=1)
        return jnp.dot(xn, w_ref[:, lo:lo + tn], preferred_element_type=F32)

    ext_ref[CONV_HALO:, 0:tn] = proj(0, tn) * jax.nn.sigmoid(proj(n_u, tn))
    stages = [("a", tn), ("b", n_u + tn)] + [("gate", g0 + c * tn) for c in range(n_gate)]
    last_rows = (slice(tm - SUBLANES, tm), slice(tn - LANES, tn))
    mxu_done = None
    conv_done = None
    glu_a = None
    for c, (kind, lo) in enumerate(stages):
        if c < n_conv:
            _conv_chunk(c, ext_ref, sh_ref, cv_ref, dww_ref, dwb_ref, after=mxu_done)
        y = proj_after(lo, conv_done if c < n_conv else None)
        mxu_done = y[last_rows]
        if kind == "a":
            glu_a = y
        elif kind == "b":
            ext_ref[CONV_HALO:, tn:2 * tn] = glu_a * jax.nn.sigmoid(y)
        else:
            col = lo - g0
            gate_ref[:, col:col + tn] = y.astype(gate_ref.dtype)
        if c < n_conv:
            conv_done = cv_ref[tm - SUBLANES:tm, c * LANES:(c + 1) * LANES]
    ext_ref[0:CONV_HALO, :] = ext_ref[tm:tm + CONV_HALO, :]
    _ln_silu(cv_ref, lng_ref, lnb_ref, act_ref)
    _head_norm_store(proj(2 * n_u, n_qkv), gain_ref, flag_ref, bd_ref[...], qkv_ref, HEAD_DIM)


def _in_proj(x, g, w, gain, flag, dw_w, dw_b, ln_g, ln_b, seq, tm=256, tn=512):
    m, d = x.shape
    n_qkv = ATTN_WIDTH + 2 * KV_WIDTH
    n_gate = w.shape[1] - 2 * CONV_CH - n_qkv
    assert seq % tm == 0
    row = lambda i: (i, 0)
    vec = _const_spec((1, CONV_CH))
    return pl.pallas_call(
        functools.partial(_in_proj_kernel, tn=tn, tiles_per_seq=seq // tm),
        grid=(m // tm,),
        in_specs=[pl.BlockSpec((tm, d), row), _const_spec((1, d)), _const_spec(w.shape), _const_spec((1, n_qkv)),
                  _const_spec((1, n_qkv)), _const_spec((LANES, LANES)), _const_spec((CONV_WIDTH, CONV_CH)), vec, vec, vec],
        out_specs=[pl.BlockSpec((tm, CONV_CH), row), pl.BlockSpec((tm, n_qkv), row), pl.BlockSpec((tm, n_gate), row)],
        out_shape=[jax.ShapeDtypeStruct((m, CONV_CH), BF16), jax.ShapeDtypeStruct((m, n_qkv), BF16),
                   jax.ShapeDtypeStruct((m, n_gate), BF16)],
        scratch_shapes=[pltpu.VMEM((tm, d), BF16), pltpu.VMEM((tm + CONV_HALO, CONV_CH), F32),
                        pltpu.VMEM((2, SUBLANES - 1, tm + CONV_HALO - SUBLANES, LANES), F32), pltpu.VMEM((tm, CONV_CH), F32)],
        compiler_params=_cparams(("arbitrary",)),
        name="in_proj",
    )(x, g.reshape(1, d), w, gain.reshape(1, -1), flag.reshape(1, -1), _block_diag_ones(HEAD_DIM),
      dw_w, dw_b.reshape(1, -1), ln_g.reshape(1, -1), ln_b.reshape(1, -1))


def _mem_proj_kernel(x_ref, g_ref, w_ref, kgain_ref, o_ref):
    x = x_ref[...]
    xn = (x * _rms_scale(x) * g_ref[...]).astype(BF16)
    y = jnp.dot(xn, w_ref[...], preferred_element_type=F32)
    n_k = kgain_ref.shape[1]
    for c in range(y.shape[1] // LANES):
        sl = slice(c * LANES, (c + 1) * LANES)
        yc = y[:, sl]
        if c * LANES < n_k:
            yc = yc * _rms_scale(yc) * kgain_ref[:, sl]
        o_ref[:, sl] = yc.astype(o_ref.dtype)


def _mem_proj(x, g, w, kgain, tm=256):
    m, d = x.shape
    n = w.shape[1]
    assert X_HEAD_DIM == LANES
    return pl.pallas_call(
        _mem_proj_kernel,
        grid=(m // tm,),
        in_specs=[pl.BlockSpec((tm, d), lambda i: (i, 0)), _const_spec((1, d)), _const_spec(w.shape),
                  _const_spec((1, kgain.shape[0]))],
        out_specs=pl.BlockSpec((tm, n), lambda i: (i, 0)),
        out_shape=jax.ShapeDtypeStruct((m, n), BF16),
        compiler_params=_cparams(("parallel",)),
        name="mem_proj",
    )(x, g.reshape(1, d), w, kgain.reshape(1, -1))


def _block_diag_ones(group):
    r = np.arange(LANES)
    return jnp.asarray((r[:, None] // group) == (r[None, :] // group), dtype=BF16)


def _swa_kernel(sink_ref, q_ref, kp_ref, kc_ref, vp_ref, vc_ref, bias_ref, o_ref, *, n_chunks):
    first = pl.program_id(1) == 0
    qb = q_ref.shape[0]
    lane = lax.broadcasted_iota(jnp.int32, (qb, LANES), 1)
    lane2 = lax.broadcasted_iota(jnp.int32, (2 * qb, LANES), 1)
    row = lax.broadcasted_iota(jnp.int32, (2 * qb, 1), 0)
    col = lax.broadcasted_iota(jnp.int32, (2 * qb, 2 * qb), 1)

    def both_halves(prev_ref, cur_ref):
        t = jnp.concatenate([prev_ref[...], cur_ref[...]], axis=0).astype(F32)
        swapped = pltpu.roll(t, HEAD_DIM, 1)
        return (jnp.where(lane2 < HEAD_DIM, t, swapped).astype(BF16), jnp.where(lane2 < HEAD_DIM, swapped, t).astype(BF16))

    ks = both_halves(kp_ref, kc_ref)
    vs = both_halves(vp_ref, vc_ref)
    no_prev = jnp.where(jnp.logical_and(first, col < qb), NEG, 0.0)
    chunks_per_kv = n_chunks // N_KV_HEADS
    for c in range(n_chunks):
        q2 = q_ref[:, c * LANES:(c + 1) * LANES]
        zero = jnp.zeros_like(q2)
        qs = jnp.concatenate([jnp.where(lane < HEAD_DIM, q2, zero), jnp.where(lane >= HEAD_DIM, q2, zero)], axis=0)
        s = lax.dot_general(qs, ks[c // chunks_per_kv], (((1,), (1,)), ((), ())), preferred_element_type=F32)
        s = s + bias_ref[c] + no_prev
        sink = jnp.where(row < qb, sink_ref[2 * c], sink_ref[2 * c + 1])
        m = jnp.maximum(jnp.max(s, axis=-1, keepdims=True), sink)
        p = jnp.exp(s - m)
        den = jnp.sum(p, axis=-1, keepdims=True) + jnp.exp(sink - m)
        o2 = jnp.dot(p.astype(BF16), vs[c // chunks_per_kv], preferred_element_type=F32) / den
        o = jnp.where(lane < HEAD_DIM, o2[:qb], o2[qb:])
        o_ref[:, c * LANES:(c + 1) * LANES] = o.astype(o_ref.dtype)


def _t5_bucket_np(dist):
    n = np.maximum(dist, 0)
    max_exact = NUM_BUCKETS // 2
    large = max_exact + (np.log(np.maximum(n, 1).astype(np.float32) / max_exact)
                         / math.log(MAX_DISTANCE / max_exact) * (NUM_BUCKETS - max_exact)).astype(np.int32)
    large = np.minimum(large, NUM_BUCKETS - 1)
    return np.where(n < max_exact, n, large)


def _swa(qkv, rel_bias, sinks, batch, seq):
    qb = WINDOW
    nb = seq // qb
    n_chunks = N_Q_HEADS // 2
    qi = np.arange(qb)[:, None]
    kj = np.arange(2 * qb)[None, :]
    dist = qi + qb - kj
    valid = (dist >= 0) & (dist < WINDOW)
    onehot = jnp.asarray(_t5_bucket_np(dist)[:, :, None] == np.arange(NUM_BUCKETS), dtype=F32)
    bias = jnp.einsum('qkb,bh->qkh', onehot, rel_bias.astype(F32), precision=lax.Precision.HIGHEST)
    bias = jnp.where(valid[:, :, None], bias, NEG).transpose(2, 0, 1)
    bias = bias.reshape(n_chunks, 2 * qb, 2 * qb)
    kcol = ATTN_WIDTH // LANES
    vcol = kcol + 1

    def prev(b, n):
        return (b * nb + jnp.maximum(n - 1, 0))

    return pl.pallas_call(
        functools.partial(_swa_kernel, n_chunks=n_chunks),
        grid=(batch, nb),
        in_specs=[
            pl.BlockSpec(memory_space=pltpu.SMEM),
            pl.BlockSpec((qb, ATTN_WIDTH), lambda b, n: (b * nb + n, 0)),
            pl.BlockSpec((qb, LANES), lambda b, n: (prev(b, n), kcol)),
            pl.BlockSpec((qb, LANES), lambda b, n: (b * nb + n, kcol)),
            pl.BlockSpec((qb, LANES), lambda b, n: (prev(b, n), vcol)),
            pl.BlockSpec((qb, LANES), lambda b, n: (b * nb + n, vcol)),
            _const_spec((n_chunks, 2 * qb, 2 * qb)),
        ],
        out_specs=pl.BlockSpec((qb, ATTN_WIDTH), lambda b, n: (b * nb + n, 0)),
        out_shape=jax.ShapeDtypeStruct((batch * seq, ATTN_WIDTH), BF16),
        compiler_params=_cparams(("parallel", "parallel")),
        name="swa",
    )(sinks.astype(F32), qkv, qkv, qkv, qkv, qkv, bias)


def _mix_kernel(c_ref, a_ref, g0_ref, g1_ref, x_ref, wc_ref, wa_ref, wm_ref, o_ref):
    y_conv = jnp.dot(c_ref[...], wc_ref[...], preferred_element_type=F32)
    y_attn = jnp.dot(a_ref[...], wa_ref[...], preferred_element_type=F32)
    merged = jax.nn.sigmoid(g0_ref[...].astype(F32)) * y_conv + jax.nn.sigmoid(g1_ref[...].astype(F32)) * y_attn
    o_ref[...] = x_ref[...] + jnp.dot(merged.astype(BF16), wm_ref[...], preferred_element_type=F32)


def _mix(conv_act, attn_o, gates, x, w_conv_out, w_attn_out, w_mix_out, tm=256):
    n, d = x.shape
    return pl.pallas_call(
        _mix_kernel,
        grid=(n // tm,),
        in_specs=[
            pl.BlockSpec((tm, CONV_CH), lambda i: (i, 0)),
            pl.BlockSpec((tm, ATTN_WIDTH), lambda i: (i, 0)),
            pl.BlockSpec((tm, d), lambda i: (i, 0)),
            pl.BlockSpec((tm, d), lambda i: (i, 1)),
            pl.BlockSpec((tm, d), lambda i: (i, 0)),
            _const_spec((CONV_CH, d)),
            _const_spec((ATTN_WIDTH, d)),
            _const_spec((d, d)),
        ],
        out_specs=pl.BlockSpec((tm, d), lambda i: (i, 0)),
        out_shape=jax.ShapeDtypeStruct((n, d), F32),
        compiler_params=_cparams(("parallel",)),
        name="mix",
    )(conv_act, attn_o, gates, gates, x, w_conv_out, w_attn_out, w_mix_out)


def _xattn_route_kernel(h_ref, gx_ref, wq_ref, qgain_ref, k_ref, v_ref, wo_ref, gm_ref,
                        wr_ref, br_ref, tri_ref,
                        h2_ref, xp_ref, ri_ref, rf_ref, cnt_ref, run_ref):
    tm, d = h_ref.shape
    half = d // 2

    @pl.when(pl.program_id(0) == 0)
    def _():
        run_ref[...] = jnp.zeros_like(run_ref)

    h = h_ref[...]
    xn = (h * _rms_scale(h) * gx_ref[...]).astype(BF16)
    q = jnp.dot(xn, wq_ref[...], preferred_element_type=F32)
    heads = []
    for hd in range(X_HEADS):
        sl = slice(hd * X_HEAD_DIM, (hd + 1) * X_HEAD_DIM)
        qh = q[:, sl]
        qh = (qh * _rms_scale(qh) * qgain_ref[:, sl]).astype(BF16)
        s = lax.dot_general(qh, k_ref[:, sl], (((1,), (1,)), ((), ())), preferred_element_type=F32)
        p = jnp.exp(s - jnp.max(s, axis=-1, keepdims=True))
        den = jnp.sum(p, axis=-1, keepdims=True)
        heads.append((jnp.dot(p.astype(BF16), v_ref[:, sl], preferred_element_type=F32) / den).astype(BF16))
    o = jnp.concatenate(heads, axis=-1)
    h2 = h + jnp.dot(o, wo_ref[...], preferred_element_type=F32)
    h2_ref[...] = h2

    xn2 = h2 * _rms_scale(h2) * gm_ref[...]
    xp_ref[...] = _pack_bf16_pair(xn2[:, :half], xn2[:, half:])

    x_hi = xn2.astype(BF16)
    x_lo = (xn2 - x_hi.astype(F32)).astype(BF16)
    r = jnp.dot(jnp.concatenate([x_hi, x_lo], axis=0), wr_ref[...], preferred_element_type=F32)
    lg = (r[:tm, :LANES] + r[:tm, LANES:]) + (r[tm:, :LANES] + r[tm:, LANES:]) + br_ref[...]

    lane = lax.broadcasted_iota(jnp.int32, (tm, LANES), 1)
    lane_f = lane.astype(F32)
    big = jnp.float32(LANES)

    def first_argmax(vals, vmax):
        idx = jnp.min(jnp.where(vals == vmax, lane_f, big), axis=-1, keepdims=True)
        return idx.astype(jnp.int32)

    glog = jnp.where(lane < N_GROUPS, lg, -jnp.inf)
    gmax = jnp.max(glog, axis=-1, keepdims=True)
    grp = first_argmax(glog, gmax)
    pg_top = 1.0 / jnp.sum(jnp.exp(glog - gmax), axis=-1, keepdims=True)
    elane = lane - N_GROUPS
    emask = (elane >= 0) & (elane < N_EXPERTS) & ((elane >> 3) == grp)
    elog = jnp.where(emask, lg, -jnp.inf)
    m1 = jnp.max(elog, axis=-1, keepdims=True)
    i1 = first_argmax(elog, m1)
    elog2 = jnp.where(lane == i1, -jnp.inf, elog)
    m2 = jnp.max(elog2, axis=-1, keepdims=True)
    i2 = first_argmax(elog2, m2)
    den = jnp.sum(jnp.exp(elog - m1), axis=-1, keepdims=True)
    p1 = 1.0 / den
    p2 = jnp.exp(m2 - m1) / den
    psum = p1 + p2
    g1 = pg_top * (p1 / psum)
    g2 = pg_top * (p2 / psum)
    e1 = i1 - N_GROUPS
    e2 = i2 - N_GROUPS

    onehot = jnp.where((elane == e1) | (elane == e2), 1.0, 0.0)
    before = jnp.dot(tri_ref[...], onehot.astype(BF16), preferred_element_type=F32) + run_ref[0:1, :]
    r1 = jnp.sum(jnp.where(elane == e1, before, 0.0), axis=-1, keepdims=True).astype(jnp.int32)
    r2 = jnp.sum(jnp.where(elane == e2, before, 0.0), axis=-1, keepdims=True).astype(jnp.int32)
    run = run_ref[...] + jnp.sum(onehot, axis=0, keepdims=True)
    run_ref[...] = run
    cnt_ref[0] = run.astype(jnp.int32)

    ri = jnp.where(lane == 0, e1, jnp.where(lane == 1, e2, jnp.where(lane == 2, r1, jnp.where(lane == 3, r2, 0))))
    ri_ref[0] = jnp.transpose(ri)[0:SUBLANES, :]
    rf_ref[...] = jnp.where(lane == 0, g1, jnp.where(lane == 1, g2, 0.0))


def _xattn_route(h1, memkv, gx, w_xq, q_gain, w_xo, gm, wr, b_r, seq, mem_len, tm=512):
    n, d = h1.shape
    tiles_per_seq = seq // tm
    nt = n // tm
    assert X_HEAD_DIM == LANES
    tri = jnp.asarray(np.tril(np.ones((tm, tm), np.float32), -1), dtype=BF16)
    return pl.pallas_call(
        _xattn_route_kernel,
        grid=(nt,),
        in_specs=[
            pl.BlockSpec((tm, d), lambda i: (i, 0)),
            _const_spec((1, d)),
            _const_spec((d, X_WIDTH)),
            _const_spec((1, X_WIDTH)),
            pl.BlockSpec((mem_len, X_WIDTH), lambda i: (i // tiles_per_seq, 0)),
            pl.BlockSpec((mem_len, X_WIDTH), lambda i: (i // tiles_per_seq, 1)),
            _const_spec((X_WIDTH, d)),
            _const_spec((1, d)),
            _const_spec((d, 2 * LANES)),
            _const_spec((1, LANES)),
            _const_spec((tm, tm)),
        ],
        out_specs=[
            pl.BlockSpec((tm, d), lambda i: (i, 0)),
            pl.BlockSpec((tm, d // 2), lambda i: (i, 0)),
            pl.BlockSpec((1, SUBLANES, tm), lambda i: (i, 0, 0)),
            pl.BlockSpec((tm, LANES), lambda i: (i, 0)),
            pl.BlockSpec((1, 8, LANES), lambda i: (i, 0, 0)),
        ],
        out_shape=[
            jax.ShapeDtypeStruct((n, d), F32),
            jax.ShapeDtypeStruct((n, d // 2), jnp.uint32),
            jax.ShapeDtypeStruct((nt, SUBLANES, tm), jnp.int32),
            jax.ShapeDtypeStruct((n, LANES), F32),
            jax.ShapeDtypeStruct((nt, 8, LANES), jnp.int32),
        ],
        scratch_shapes=[pltpu.VMEM((8, LANES), F32)],
        compiler_params=_cparams(("arbitrary",)),
        name="xattn_route",
    )(h1, gx.reshape(1, d), w_xq, q_gain, memkv, memkv, w_xo, gm.reshape(1, d), wr, b_r, tri)


def _dest_rows_kernel(padded_ref, ri_ref, d0_ref, d1_ref):
    for k, d_ref in enumerate((d0_ref, d1_ref)):
        e = ri_ref[:, k, :]
        row = ri_ref[:, 2 + k, :]
        for j in range(N_EXPERTS - 1):
            row = row + jnp.where(e > j, padded_ref[j], 0)
        d_ref[...] = row


def _dest_rows(ri, padded):
    nt, _, tm = ri.shape
    out = jax.ShapeDtypeStruct((nt, tm), jnp.int32)
    return pl.pallas_call(
        _dest_rows_kernel,
        in_specs=[pl.BlockSpec(memory_space=pltpu.SMEM), pl.BlockSpec(memory_space=pltpu.VMEM)],
        out_specs=[pl.BlockSpec(memory_space=pltpu.VMEM)] * 2,
        out_shape=[out, out],
        name="dest_rows",
    )(padded, ri)


def _dispatch_kernel(tail_ref, d0_ref, d1_ref, x_ref, xs_ref, zero_ref, sem, zsem):
    tm = x_ref.shape[0]

    @pl.when(pl.program_id(0) == 0)
    def _():
        zero_ref[...] = jnp.zeros_like(zero_ref)

        def zero_copy(e):
            start = pl.multiple_of(tail_ref[e], DISPATCH_BLOCK)
            return pltpu.make_async_copy(zero_ref, xs_ref.at[pl.ds(start, DISPATCH_BLOCK)], zsem)

        for e in range(N_EXPERTS):
            pl.when(tail_ref[e] >= 0)(lambda e=e: zero_copy(e).start())
        for e in range(N_EXPERTS):
            pl.when(tail_ref[e] >= 0)(lambda e=e: zero_copy(e).wait())

        def unused_copy(b):
            return pltpu.make_async_copy(zero_ref, xs_ref.at[pl.ds(pl.multiple_of(b * DISPATCH_BLOCK, DISPATCH_BLOCK),
                                                                   DISPATCH_BLOCK)], zsem)

        n_blocks = xs_ref.shape[0] // DISPATCH_BLOCK
        lax.fori_loop(tail_ref[N_EXPERTS], n_blocks, lambda b, c: (unused_copy(b).start(), c)[1], 0)
        lax.fori_loop(tail_ref[N_EXPERTS], n_blocks, lambda b, c: (unused_copy(b).wait(), c)[1], 0)

    def copy(t, k):
        return pltpu.make_async_copy(x_ref.at[pl.ds(t, 1)], xs_ref.at[pl.ds((d0_ref, d1_ref)[k][0, 0, t], 1)], sem)

    def issue(tb, carry):
        for j in range(DMA_UNROLL):
            copy(tb * DMA_UNROLL + j, 0).start()
            copy(tb * DMA_UNROLL + j, 1).start()
        return carry

    lax.fori_loop(0, tm // DMA_UNROLL, issue, 0)

    def drain(tb, carry):
        for j in range(DMA_UNROLL):
            copy(tb * DMA_UNROLL + j, 0).wait()
            copy(tb * DMA_UNROLL + j, 1).wait()
        return carry

    lax.fori_loop(0, tm // DMA_UNROLL, drain, 0)


def _dispatch(xp, dest, tail, rows):
    n, w = xp.shape
    nt, tm = dest[0].shape
    grid_spec = pltpu.PrefetchScalarGridSpec(
        num_scalar_prefetch=1,
        grid=(nt,),
        in_specs=[
            pl.BlockSpec((1, 1, tm), lambda i, tail: (i, 0, 0), memory_space=pltpu.SMEM),
            pl.BlockSpec((1, 1, tm), lambda i, tail: (i, 0, 0), memory_space=pltpu.SMEM),
            pl.BlockSpec((tm, w), lambda i, tail: (i, 0)),
        ],
        out_specs=pl.BlockSpec(memory_space=pl.ANY),
        scratch_shapes=[pltpu.VMEM((DISPATCH_BLOCK, w), xp.dtype), pltpu.SemaphoreType.DMA(()),
                        pltpu.SemaphoreType.DMA(())],
    )
    return pl.pallas_call(
        _dispatch_kernel,
        grid_spec=grid_spec,
        out_shape=jax.ShapeDtypeStruct((rows, w), xp.dtype),
        compiler_params=_cparams(("arbitrary",)),
        name="dispatch",
    )(tail, dest[0].reshape(nt, 1, tm), dest[1].reshape(nt, 1, tm), xp)


def _experts_kernel(blk_e_ref, blk_x_ref, nused_ref, first_ref, next_e_ref, xs_ref, wgu_hbm, wdn_hbm, y_ref,
                    wgu_f32, wdn_f32, wgu_bf, wdn_bf, sems):
    b = pl.program_id(0)
    half = xs_ref.shape[1]

    def fetch(e):
        return (pltpu.make_async_copy(wgu_hbm.at[e], wgu_f32, sems.at[0]),
                pltpu.make_async_copy(wdn_hbm.at[e], wdn_f32, sems.at[1]))

    @pl.when(b == 0)
    def _():
        for c in fetch(blk_e_ref[0]):
            c.start()

    @pl.when(first_ref[b] == 1)
    def _():
        for c in fetch(blk_e_ref[b]):
            c.wait()
        wgu_bf[...] = wgu_f32[...].astype(BF16)
        wdn_bf[...] = wdn_f32[...].astype(BF16)

        @pl.when(next_e_ref[b] >= 0)
        def _():
            for c in fetch(next_e_ref[b]):
                c.start()

    @pl.when(b < nused_ref[0])
    def _():
        x_lo, x_hi = _unpack_bf16_pair(xs_ref[...])
        gu = (jnp.dot(x_lo.astype(BF16), wgu_bf[0:half, :], preferred_element_type=F32)
              + jnp.dot(x_hi.astype(BF16), wgu_bf[half:, :], preferred_element_type=F32))
        g = gu[:, :D_EXPERT]
        u = gu[:, D_EXPERT:]
        act = (g * jax.nn.sigmoid(g) * u).astype(BF16)
        y = jnp.dot(act, wdn_bf[...], preferred_element_type=F32)
        y_ref[...] = _pack_bf16_pair(y[:, :half], y[:, half:])

    @pl.when(b >= nused_ref[0])
    def _():
        y_ref[...] = jnp.zeros_like(y_ref)


def _experts(xs, w_gu, w_dn, blk_e, blk_x, nused, first, next_e):
    rows, half = xs.shape
    d = 2 * half
    n_blocks = rows // DISPATCH_BLOCK
    grid_spec = pltpu.PrefetchScalarGridSpec(
        num_scalar_prefetch=5,
        grid=(n_blocks,),
        in_specs=[
            pl.BlockSpec((DISPATCH_BLOCK, half), lambda b, be, bx, *_: (bx[b], 0)),
            pl.BlockSpec(memory_space=pl.ANY),
            pl.BlockSpec(memory_space=pl.ANY),
        ],
        out_specs=pl.BlockSpec((DISPATCH_BLOCK, half), lambda b, *_: (b, 0)),
        scratch_shapes=[pltpu.VMEM((d, 2 * D_EXPERT), F32), pltpu.VMEM((D_EXPERT, d), F32),
                        pltpu.VMEM((d, 2 * D_EXPERT), BF16), pltpu.VMEM((D_EXPERT, d), BF16),
                        pltpu.SemaphoreType.DMA((2,))],
    )
    return pl.pallas_call(
        _experts_kernel,
        grid_spec=grid_spec,
        out_shape=jax.ShapeDtypeStruct((rows, half), jnp.uint32),
        compiler_params=_cparams(("arbitrary",)),
        name="experts",
    )(blk_e, blk_x, nused, first, next_e, xs, w_gu, w_dn)


def _combine_kernel(cur0_ref, cur1_ref, next0_ref, next1_ref, h_ref, rf_ref, y_ref, o_ref, ya_ref, yb_ref, sems):
    i = pl.program_id(0)
    nt = pl.num_programs(0)
    tm, d = h_ref.shape
    half = d // 2
    slot = i % 2

    def issue(d0_ref, d1_ref, s):
        def body(tb, carry):
            for j in range(DMA_UNROLL):
                t = tb * DMA_UNROLL + j
                pltpu.make_async_copy(y_ref.at[pl.ds(d0_ref[0, 0, t], 1)], ya_ref.at[s, pl.ds(t, 1)], sems.at[s]).start()
                pltpu.make_async_copy(y_ref.at[pl.ds(d1_ref[0, 0, t], 1)], yb_ref.at[s, pl.ds(t, 1)], sems.at[s]).start()
            return carry

        lax.fori_loop(0, tm // DMA_UNROLL, body, 0)

    pl.when(i == 0)(lambda: issue(cur0_ref, cur1_ref, 0))
    pl.when(i + 1 < nt)(lambda: issue(next0_ref, next1_ref, 1 - slot))
    pltpu.make_async_copy(y_ref.at[pl.ds(0, tm)], ya_ref.at[slot], sems.at[slot]).wait()
    pltpu.make_async_copy(y_ref.at[pl.ds(0, tm)], yb_ref.at[slot], sems.at[slot]).wait()
    g1 = rf_ref[:, 0:1]
    g2 = rf_ref[:, 1:2]
    a_lo, a_hi = _unpack_bf16_pair(ya_ref[slot])
    b_lo, b_hi = _unpack_bf16_pair(yb_ref[slot])
    o_ref[:, :half] = h_ref[:, :half] + (a_lo * g1 + b_lo * g2)
    o_ref[:, half:] = h_ref[:, half:] + (a_hi * g1 + b_hi * g2)


def _combine(h2, rf, dest, ys, tm=256):
    n, d = h2.shape
    nt = n // tm
    half = d // 2
    cur = pl.BlockSpec((1, 1, tm), lambda i: (i, 0, 0), memory_space=pltpu.SMEM)
    nxt = pl.BlockSpec((1, 1, tm), lambda i: (jnp.minimum(i + 1, nt - 1), 0, 0), memory_space=pltpu.SMEM)
    d0 = dest[0].reshape(nt, 1, tm)
    d1 = dest[1].reshape(nt, 1, tm)
    return pl.pallas_call(
        _combine_kernel,
        grid=(nt,),
        in_specs=[
            cur, cur, nxt, nxt,
            pl.BlockSpec((tm, d), lambda i: (i, 0)),
            pl.BlockSpec((tm, LANES), lambda i: (i, 0)),
            pl.BlockSpec(memory_space=pl.ANY),
        ],
        out_specs=pl.BlockSpec((tm, d), lambda i: (i, 0)),
        out_shape=jax.ShapeDtypeStruct((n, d), F32),
        scratch_shapes=[pltpu.VMEM((2, tm, half), jnp.uint32), pltpu.VMEM((2, tm, half), jnp.uint32),
                        pltpu.SemaphoreType.DMA((2,))],
        compiler_params=_cparams(("arbitrary",)),
        name="combine",
    )(d0, d1, d0, d1, h2, rf, ys)


def kernel(x, mem, rel_bias, norm_mix_g, w_in, conv_dw_w, conv_dw_b, conv_ln_g, conv_ln_b, w_conv_out, q_norm_g, k_norm_g, attn_sinks, w_attn_out, w_mix_out, norm_x_g, norm_mem_g, w_xq, w_xkv, xq_norm_g, xk_norm_g, w_xo, norm_moe_g, w_router_group, b_router_group, w_router_expert, b_router_expert, w_expert_gu, w_expert_down):
    batch, seq, d = x.shape
    mem_len = mem.shape[1]
    n = batch * seq
    h = x.reshape(n, d)
    for l in range(norm_mix_g.shape[0]):
        w = w_in[l].astype(BF16)
        q_gain = jnp.tile(q_norm_g[l].astype(F32) * HEAD_DIM ** -0.5, N_Q_HEADS)
        gain = jnp.concatenate([q_gain, jnp.tile(k_norm_g[l].astype(F32), N_KV_HEADS), jnp.ones((KV_WIDTH,), F32)])
        flag = jnp.concatenate([jnp.ones((ATTN_WIDTH + KV_WIDTH,), F32), jnp.zeros((KV_WIDTH,), F32)])
        conv_act, qkv, gates = _in_proj(h, norm_mix_g[l], w, gain, flag, conv_dw_w[l].reshape(CONV_WIDTH, CONV_CH),
                                        conv_dw_b[l], conv_ln_g[l], conv_ln_b[l], seq)
        attn_o = _swa(qkv, rel_bias, attn_sinks[l], batch, seq)
        h = _mix(conv_act, attn_o, gates, h, w_conv_out[l].astype(BF16), w_attn_out[l].astype(BF16),
                 w_mix_out[l].astype(BF16))

        memkv = _mem_proj(mem.reshape(batch * mem_len, d), norm_mem_g[l], w_xkv[l].astype(BF16),
                          jnp.tile(xk_norm_g[l].astype(F32), X_HEADS))
        xq_gain = jnp.tile(xq_norm_g[l].astype(F32) * X_HEAD_DIM ** -0.5, X_HEADS).reshape(1, -1)
        w_r = jnp.concatenate([w_router_group[l], w_router_expert[l]], axis=1).astype(F32)
        w_r = jnp.pad(w_r, ((0, 0), (0, LANES - w_r.shape[1])))
        wr_hi = w_r.astype(BF16)
        wr = jnp.concatenate([wr_hi, (w_r - wr_hi.astype(F32)).astype(BF16)], axis=1)
        b_r = jnp.concatenate([b_router_group[l], b_router_expert[l]]).astype(F32)
        b_r = jnp.pad(b_r, (0, LANES - b_r.shape[0])).reshape(1, LANES)
        h2, xp, ri, rf, cnt = _xattn_route(h, memkv, norm_x_g[l], w_xq[l].astype(BF16), xq_gain, w_xo[l].astype(BF16),
                                           norm_moe_g[l], wr, b_r, seq, mem_len)

        counts = cnt[-1, 0, N_GROUPS:N_GROUPS + N_EXPERTS]
        padded = (counts + DISPATCH_BLOCK - 1) // DISPATCH_BLOCK * DISPATCH_BLOCK
        pad_end = jnp.cumsum(padded)
        n_blocks = -(-(2 * n) // DISPATCH_BLOCK) + N_EXPERTS
        rows = n_blocks * DISPATCH_BLOCK
        nused = (pad_end[-1] // DISPATCH_BLOCK).astype(jnp.int32)
        blk = jnp.minimum(jnp.arange(n_blocks, dtype=jnp.int32), nused - 1)
        blk_e = jnp.minimum(jnp.sum(pad_end[None, :] <= (blk * DISPATCH_BLOCK)[:, None], axis=1), N_EXPERTS - 1).astype(jnp.int32)
        dest = _dest_rows(ri, padded.astype(jnp.int32))

        tail = jnp.concatenate([jnp.where(counts > 0, pad_end - DISPATCH_BLOCK, -1), nused.reshape(1)]).astype(jnp.int32)
        xs = _dispatch(xp, dest, tail, rows)
        first = jnp.concatenate([jnp.ones((1,), jnp.int32), (blk_e[1:] != blk_e[:-1]).astype(jnp.int32)])
        eids = jnp.arange(N_EXPERTS, dtype=jnp.int32)
        later = (eids[None, :] > blk_e[:, None]) & (counts[None, :] > 0)
        next_e = jnp.min(jnp.where(later, eids[None, :], N_EXPERTS), axis=1)
        next_e = jnp.where(next_e < N_EXPERTS, next_e, -1).astype(jnp.int32)
        ys = _experts(xs, w_expert_gu[l], w_expert_down[l], blk_e, blk, nused.reshape(1), first, next_e)
        h = _combine(h2, rf, dest, ys)
    return h.reshape(batch, seq, d)
```

```python
import functools
import math

import jax
import jax.numpy as jnp
import numpy as np
from jax import lax
from jax.experimental import pallas as pl
from jax.experimental.pallas import tpu as pltpu

EPS = 1e-6
NEG = -1e30

CONV_CH = 1024
CONV_WIDTH = 31
N_Q_HEADS = 16
N_KV_HEADS = 2
HEAD_DIM = 64
ATTN_WIDTH = N_Q_HEADS * HEAD_DIM
KV_WIDTH = N_KV_HEADS * HEAD_DIM
WINDOW = 128
NUM_BUCKETS = 32
MAX_DISTANCE = 128
X_HEADS = 4
X_HEAD_DIM = 128
X_WIDTH = X_HEADS * X_HEAD_DIM
N_GROUPS = 4
EXPERTS_PER_GROUP = 8
N_EXPERTS = N_GROUPS * EXPERTS_PER_GROUP
GROUP_SHIFT = EXPERTS_PER_GROUP.bit_length() - 1
assert 1 << GROUP_SHIFT == EXPERTS_PER_GROUP
D_EXPERT = 512
DISPATCH_BLOCK = 256

LANES = 128
SUBLANES = 8
MXU_K = 256
SWA_BLOCKS_PER_STEP = 8
DMA_UNROLL = 8
CONV_HALO = 32
VMEM_LIMIT = 56 * 1024 * 1024

BF16 = jnp.bfloat16
F32 = jnp.float32


def _cparams(sem):
    return pltpu.CompilerParams(dimension_semantics=sem, vmem_limit_bytes=VMEM_LIMIT)


def _const_spec(shape):
    nd = len(shape)
    return pl.BlockSpec(shape, lambda *_: (0,) * nd, pipeline_mode=pl.Buffered(1))


def _rms_scale(x):
    return lax.rsqrt(jnp.mean(x * x, axis=-1, keepdims=True) + EPS)


def _group_sumsq(y, bd):
    sq = y * y
    hi = sq.astype(BF16)
    lo = (sq - hi.astype(F32)).astype(BF16)
    return (jnp.dot(hi, bd, preferred_element_type=F32) + jnp.dot(lo, bd, preferred_element_type=F32))


def _pack_bf16_pair(lo, hi):
    lo_bits = lax.bitcast_convert_type(lo.astype(BF16).astype(F32), jnp.uint32)
    hi_bits = lax.bitcast_convert_type(hi.astype(BF16).astype(F32), jnp.uint32)
    return (lo_bits >> 16) | hi_bits


def _unpack_bf16_pair(w):
    return (lax.bitcast_convert_type(w << 16, F32), lax.bitcast_convert_type(w & jnp.uint32(0xFFFF0000), F32))


def _head_norm_store(y, gain_ref, flag_ref, bd, o_ref, head_dim):
    for c in range(y.shape[1] // LANES):
        sl = slice(c * LANES, (c + 1) * LANES)
        yc = y[:, sl]
        ss = _group_sumsq(yc, bd)
        normed = yc * lax.rsqrt(ss * (1.0 / head_dim) + EPS) * gain_ref[:, sl]
        o_ref[:, sl] = jnp.where(flag_ref[:, sl] > 0.0, normed, yc).astype(o_ref.dtype)


def _opaque_zero(dep):
    bits = lax.shift_right_logical(lax.bitcast_convert_type(dep, jnp.uint32), jnp.uint32(32))
    return lax.bitcast_convert_type(bits, F32)[0:1, :]


def _conv_chunk(c, ext_ref, sh_ref, cv_ref, dww_ref, dwb_ref, row_chunk=128, after=None):
    tm = cv_ref.shape[0]
    off = CONV_HALO - (CONV_WIDTH - 1)
    span = tm + CONV_HALO - SUBLANES
    sl = slice(c * LANES, (c + 1) * LANES)
    for r in range(1, SUBLANES):
        sh_ref[c % 2, r - 1] = ext_ref[r:r + span, sl]
    zero = None if after is None else _opaque_zero(after)
    for rc in range(tm // row_chunk):
        acc = jnp.broadcast_to(dwb_ref[:, sl] if zero is None else dwb_ref[:, sl] + zero, (row_chunk, LANES))
        for j in range(CONV_WIDTH):
            q, r = divmod(off + j, SUBLANES)
            lo = SUBLANES * q + rc * row_chunk
            win = ext_ref[lo:lo + row_chunk, sl] if r == 0 else sh_ref[c % 2, r - 1, lo:lo + row_chunk, :]
            wj = dww_ref[j:j + 1, sl] if zero is None else dww_ref[j:j + 1, sl] + zero
            acc = acc + wj * win
        cv_ref[rc * row_chunk:(rc + 1) * row_chunk, sl] = acc


def _ln_silu(cv_ref, lng_ref, lnb_ref, act_ref):
    cv = cv_ref[...]
    mu = jnp.mean(cv, axis=-1, keepdims=True)
    cen = cv - mu
    var = jnp.mean(cen * cen, axis=-1, keepdims=True)
    ln = cen * lax.rsqrt(var + EPS) * lng_ref[...] + lnb_ref[...]
    act_ref[...] = (ln * jax.nn.sigmoid(ln)).astype(act_ref.dtype)


def _in_proj_kernel(x_ref, g_ref, w_ref, gain_ref, flag_ref, bd_ref, dww_ref, dwb_ref, lng_ref, lnb_ref,
                    act_ref, qkv_ref, gate_ref, xn_ref, ext_ref, sh_ref, cv_ref, *, tn, tiles_per_seq):
    tm = x_ref.shape[0]
    x = x_ref[...]
    xn_ref[...] = (x * _rms_scale(x) * g_ref[...]).astype(BF16)

    def proj(lo, width):
        return jnp.dot(xn_ref[...], w_ref[:, lo:lo + width], preferred_element_type=F32)

    @pl.when(pl.program_id(0) % tiles_per_seq == 0)
    def _():
        ext_ref[0:CONV_HALO, :] = jnp.zeros((CONV_HALO, CONV_CH), F32)

    n_u = act_ref.shape[1]
    n_qkv = qkv_ref.shape[1]
    g0 = 2 * n_u + n_qkv
    n_conv = CONV_CH // LANES
    n_gate = gate_ref.shape[1] // tn
    assert n_u == 2 * tn

    def proj_after(lo, after):
        xn = xn_ref[...]
        if after is not None:
            zb = jnp.broadcast_to(_opaque_zero(after), (SUBLANES, LANES)).astype(BF16)
            zb = jnp.tile(zb, (tm // SUBLANES, MXU_K // LANES))
            xn = jnp.concatenate([xn[:, :MXU_K] + zb, xn[:, MXU_K:]], axis=1)
        return jnp.dot(xn, w_ref[:, lo:lo + tn], preferred_element_type=F32)

    ext_ref[CONV_HALO:, 0:tn] = proj(0, tn) * jax.nn.sigmoid(proj(n_u, tn))
    stages = [("a", tn), ("b", n_u + tn)] + [("gate", g0 + c * tn) for c in range(n_gate)]
    last_rows = (slice(tm - SUBLANES, tm), slice(tn - LANES, tn))
    mxu_done = None
    conv_done = None
    glu_a = None
    for c, (kind, lo) in enumerate(stages):
        if c < n_conv:
            _conv_chunk(c, ext_ref, sh_ref, cv_ref, dww_ref, dwb_ref, after=mxu_done)
        y = proj_after(lo, conv_done if c < n_conv else None)
        mxu_done = y[last_rows]
        if kind == "a":
            glu_a = y
        elif kind == "b":
            ext_ref[CONV_HALO:, tn:2 * tn] = glu_a * jax.nn.sigmoid(y)
        else:
            col = lo - g0
            gate_ref[:, col:col + tn] = y.astype(gate_ref.dtype)
        if c < n_conv:
            conv_done = cv_ref[tm - SUBLANES:tm, c * LANES:(c + 1) * LANES]
    ext_ref[0:CONV_HALO, :] = ext_ref[tm:tm + CONV_HALO, :]
    _ln_silu(cv_ref, lng_ref, lnb_ref, act_ref)
    _head_norm_store(proj(2 * n_u, n_qkv), gain_ref, flag_ref, bd_ref[...], qkv_ref, HEAD_DIM)


def _in_proj(x, g, w, gain, flag, dw_w, dw_b, ln_g, ln_b, seq, tm=256, tn=512):
    m, d = x.shape
    n_qkv = ATTN_WIDTH + 2 * KV_WIDTH
    n_gate = w.shape[1] - 2 * CONV_CH - n_qkv
    assert seq % tm == 0
    row = lambda i: (i, 0)
    vec = _const_spec((1, CONV_CH))
    return pl.pallas_call(
        functools.partial(_in_proj_kernel, tn=tn, tiles_per_seq=seq // tm),
        grid=(m // tm,),
        in_specs=[pl.BlockSpec((tm, d), row), _const_spec((1, d)), _const_spec(w.shape), _const_spec((1, n_qkv)),
                  _const_spec((1, n_qkv)), _const_spec((LANES, LANES)), _const_spec((CONV_WIDTH, CONV_CH)), vec, vec, vec],
        out_specs=[pl.BlockSpec((tm, CONV_CH), row), pl.BlockSpec((tm, n_qkv), row), pl.BlockSpec((tm, n_gate), row)],
        out_shape=[jax.ShapeDtypeStruct((m, CONV_CH), BF16), jax.ShapeDtypeStruct((m, n_qkv), BF16),
                   jax.ShapeDtypeStruct((m, n_gate), BF16)],
        scratch_shapes=[pltpu.VMEM((tm, d), BF16), pltpu.VMEM((tm + CONV_HALO, CONV_CH), F32),
                        pltpu.VMEM((2, SUBLANES - 1, tm + CONV_HALO - SUBLANES, LANES), F32), pltpu.VMEM((tm, CONV_CH), F32)],
        compiler_params=_cparams(("arbitrary",)),
        name="in_proj",
    )(x, g.reshape(1, d), w, gain.reshape(1, -1), flag.reshape(1, -1), _block_diag_ones(HEAD_DIM),
      dw_w, dw_b.reshape(1, -1), ln_g.reshape(1, -1), ln_b.reshape(1, -1))


def _mem_proj_kernel(x_ref, g_ref, w_ref, kgain_ref, o_ref):
    x = x_ref[...]
    xn = (x * _rms_scale(x) * g_ref[...]).astype(BF16)
    y = jnp.dot(xn, w_ref[...], preferred_element_type=F32)
    n_k = kgain_ref.shape[1]
    for c in range(y.shape[1] // LANES):
        sl = slice(c * LANES, (c + 1) * LANES)
        yc = y[:, sl]
        if c * LANES < n_k:
            yc = yc * _rms_scale(yc) * kgain_ref[:, sl]
        o_ref[:, sl] = yc.astype(o_ref.dtype)


def _mem_proj(x, g, w, kgain, tm=256):
    m, d = x.shape
    n = w.shape[1]
    assert X_HEAD_DIM == LANES
    return pl.pallas_call(
        _mem_proj_kernel,
        grid=(m // tm,),
        in_specs=[pl.BlockSpec((tm, d), lambda i: (i, 0)), _const_spec((1, d)), _const_spec(w.shape),
                  _const_spec((1, kgain.shape[0]))],
        out_specs=pl.BlockSpec((tm, n), lambda i: (i, 0)),
        out_shape=jax.ShapeDtypeStruct((m, n), BF16),
        compiler_params=_cparams(("parallel",)),
        name="mem_proj",
    )(x, g.reshape(1, d), w, kgain.reshape(1, -1))


def _block_diag_ones(group):
    r = np.arange(LANES)
    return jnp.asarray((r[:, None] // group) == (r[None, :] // group), dtype=BF16)


def _swa_kernel(q_ref, kp_ref, kc_ref, vp_ref, vc_ref, bias_ref, o_ref, *, n_chunks, qb):
    lane = lax.broadcasted_iota(jnp.int32, (qb, LANES), 1)
    lane2 = lax.broadcasted_iota(jnp.int32, (2 * qb, LANES), 1)
    row2 = lax.broadcasted_iota(jnp.int32, (2 * qb, LANES), 0)
    chunks_per_kv = n_chunks // N_KV_HEADS

    def both_halves(prev, cur):
        t = jnp.concatenate([prev, cur], axis=0).astype(F32)
        t = jnp.where(row2 == 0, 0.0, t)
        swapped = pltpu.roll(t, HEAD_DIM, 1)
        return (jnp.where(lane2 < HEAD_DIM, t, swapped).astype(BF16), jnp.where(lane2 < HEAD_DIM, swapped, t).astype(BF16))

    for sub in range(q_ref.shape[0] // qb):
        rows = slice(sub * qb, (sub + 1) * qb)
        before = slice((sub - 1) * qb, sub * qb)
        ks = both_halves(kp_ref[...] if sub == 0 else kc_ref[before, :], kc_ref[rows, :])
        vs = both_halves(vp_ref[...] if sub == 0 else vc_ref[before, :], vc_ref[rows, :])
        table = jnp.where(pl.program_id(1) == 0, 1, 0) if sub == 0 else 0
        for c in range(n_chunks):
            q2 = q_ref[rows, c * LANES:(c + 1) * LANES]
            zero = jnp.zeros_like(q2)
            qs = jnp.concatenate([jnp.where(lane < HEAD_DIM, q2, zero), jnp.where(lane >= HEAD_DIM, q2, zero)], axis=0)
            s = lax.dot_general(qs, ks[c // chunks_per_kv], (((1,), (1,)), ((), ())), preferred_element_type=F32)
            s = s + bias_ref[table, c]
            p = jnp.exp(s - jnp.max(s, axis=-1, keepdims=True))
            den = jnp.sum(p, axis=-1, keepdims=True)
            o2 = jnp.dot(p.astype(BF16), vs[c // chunks_per_kv], preferred_element_type=F32) / den
            o = jnp.where(lane < HEAD_DIM, o2[:qb], o2[qb:])
            o_ref[rows, c * LANES:(c + 1) * LANES] = o.astype(o_ref.dtype)


def _t5_bucket_np(dist):
    n = np.maximum(dist, 0)
    max_exact = NUM_BUCKETS // 2
    large = max_exact + (np.log(np.maximum(n, 1).astype(np.float32) / max_exact)
                         / math.log(MAX_DISTANCE / max_exact) * (NUM_BUCKETS - max_exact)).astype(np.int32)
    large = np.minimum(large, NUM_BUCKETS - 1)
    return np.where(n < max_exact, n, large)


def _swa(qkv, rel_bias, sinks, batch, seq):
    qb = WINDOW
    nb = seq // qb
    n_chunks = N_Q_HEADS // 2
    qi = np.arange(qb)[:, None]
    kj = np.arange(2 * qb)[None, :]
    dist = qi + qb - kj
    valid = (dist >= 0) & (dist < WINDOW)
    onehot = jnp.asarray(_t5_bucket_np(dist)[:, :, None] == np.arange(NUM_BUCKETS), dtype=F32)
    bias = jnp.einsum('qkb,bh->qkh', onehot, rel_bias.astype(F32), precision=lax.Precision.HIGHEST)
    bias = jnp.where(valid[:, :, None], bias, NEG).transpose(2, 0, 1)
    bias = bias.reshape(n_chunks, 2 * qb, 2 * qb)
    sink_rows = jnp.repeat(sinks.astype(F32).reshape(n_chunks, 2), qb, axis=1)
    bias = bias.at[:, :, 0].set(sink_rows)
    bias = jnp.stack([bias, jnp.where((kj < qb) & (kj > 0), NEG, bias)])
    kcol = ATTN_WIDTH // LANES
    vcol = kcol + 1
    sub = math.gcd(SWA_BLOCKS_PER_STEP, nb)
    steps = nb // sub

    def prev(b, n):
        return (b * nb + jnp.maximum(n * sub - 1, 0))

    return pl.pallas_call(
        functools.partial(_swa_kernel, n_chunks=n_chunks, qb=qb),
        grid=(batch, steps),
        in_specs=[
            pl.BlockSpec((sub * qb, ATTN_WIDTH), lambda b, n: (b * steps + n, 0)),
            pl.BlockSpec((qb, LANES), lambda b, n: (prev(b, n), kcol)),
            pl.BlockSpec((sub * qb, LANES), lambda b, n: (b * steps + n, kcol)),
            pl.BlockSpec((qb, LANES), lambda b, n: (prev(b, n), vcol)),
            pl.BlockSpec((sub * qb, LANES), lambda b, n: (b * steps + n, vcol)),
            _const_spec((2, n_chunks, 2 * qb, 2 * qb)),
        ],
        out_specs=pl.BlockSpec((sub * qb, ATTN_WIDTH), lambda b, n: (b * steps + n, 0)),
        out_shape=jax.ShapeDtypeStruct((batch * seq, ATTN_WIDTH), BF16),
        compiler_params=_cparams(("parallel", "parallel")),
        name="swa",
    )(qkv, qkv, qkv, qkv, qkv, bias)


def _mix_kernel(c_ref, a_ref, g0_ref, g1_ref, x_ref, wc_ref, wa_ref, wm_ref, o_ref):
    y_conv = jnp.dot(c_ref[...], wc_ref[...], preferred_element_type=F32)
    y_attn = jnp.dot(a_ref[...], wa_ref[...], preferred_element_type=F32)
    merged = jax.nn.sigmoid(g0_ref[...].astype(F32)) * y_conv + jax.nn.sigmoid(g1_ref[...].astype(F32)) * y_attn
    o_ref[...] = x_ref[...] + jnp.dot(merged.astype(BF16), wm_ref[...], preferred_element_type=F32)


def _mix(conv_act, attn_o, gates, x, w_conv_out, w_attn_out, w_mix_out, tm=256):
    n, d = x.shape
    return pl.pallas_call(
        _mix_kernel,
        grid=(n // tm,),
        in_specs=[
            pl.BlockSpec((tm, CONV_CH), lambda i: (i, 0)),
            pl.BlockSpec((tm, ATTN_WIDTH), lambda i: (i, 0)),
            pl.BlockSpec((tm, d), lambda i: (i, 0)),
            pl.BlockSpec((tm, d), lambda i: (i, 1)),
            pl.BlockSpec((tm, d), lambda i: (i, 0)),
            _const_spec((CONV_CH, d)),
            _const_spec((ATTN_WIDTH, d)),
            _const_spec((d, d)),
        ],
        out_specs=pl.BlockSpec((tm, d), lambda i: (i, 0)),
        out_shape=jax.ShapeDtypeStruct((n, d), F32),
        compiler_params=_cparams(("parallel",)),
        name="mix",
    )(conv_act, attn_o, gates, gates, x, w_conv_out, w_attn_out, w_mix_out)


def _xattn_route_kernel(h_ref, gx_ref, wq_ref, qgain_ref, k_ref, v_ref, wo_ref, gm_ref,
                        wr_ref, br_ref, tri_ref,
                        h2_ref, xp_ref, ri_ref, rf_ref, cnt_ref, run_ref):
    tm, d = h_ref.shape
    half = d // 2

    @pl.when(pl.program_id(0) == 0)
    def _():
        run_ref[...] = jnp.zeros_like(run_ref)

    h = h_ref[...]
    xn = (h * _rms_scale(h) * gx_ref[...]).astype(BF16)
    q = jnp.dot(xn, wq_ref[...], preferred_element_type=F32)
    heads = []
    for hd in range(X_HEADS):
        sl = slice(hd * X_HEAD_DIM, (hd + 1) * X_HEAD_DIM)
        qh = q[:, sl]
        qh = (qh * _rms_scale(qh) * qgain_ref[:, sl]).astype(BF16)
        s = lax.dot_general(qh, k_ref[:, sl], (((1,), (1,)), ((), ())), preferred_element_type=F32)
        p = jnp.exp(s - jnp.max(s, axis=-1, keepdims=True))
        den = jnp.sum(p, axis=-1, keepdims=True)
        heads.append((jnp.dot(p.astype(BF16), v_ref[:, sl], preferred_element_type=F32) / den).astype(BF16))
    o = jnp.concatenate(heads, axis=-1)
    h2 = h + jnp.dot(o, wo_ref[...], preferred_element_type=F32)
    h2_ref[...] = h2

    xn2 = h2 * _rms_scale(h2) * gm_ref[...]
    xp_ref[...] = _pack_bf16_pair(xn2[:, :half], xn2[:, half:])

    x_hi = xn2.astype(BF16)
    x_lo = (xn2 - x_hi.astype(F32)).astype(BF16)
    r = jnp.dot(jnp.concatenate([x_hi, x_lo], axis=0), wr_ref[...], preferred_element_type=F32)
    lg = (r[:tm, :LANES] + r[:tm, LANES:]) + (r[tm:, :LANES] + r[tm:, LANES:]) + br_ref[...]

    lane = lax.broadcasted_iota(jnp.int32, (tm, LANES), 1)
    lane_f = lane.astype(F32)
    big = jnp.float32(LANES)

    def first_argmax(vals, vmax):
        idx = jnp.min(jnp.where(vals == vmax, lane_f, big), axis=-1, keepdims=True)
        return idx.astype(jnp.int32)

    glog = jnp.where(lane < N_GROUPS, lg, -jnp.inf)
    gmax = jnp.max(glog, axis=-1, keepdims=True)
    grp = first_argmax(glog, gmax)
    pg_top = 1.0 / jnp.sum(jnp.exp(glog - gmax), axis=-1, keepdims=True)
    elane = lane - N_GROUPS
    emask = (elane >= 0) & (elane < N_EXPERTS) & ((elane >> GROUP_SHIFT) == grp)
    elog = jnp.where(emask, lg, -jnp.inf)
    m1 = jnp.max(elog, axis=-1, keepdims=True)
    i1 = first_argmax(elog, m1)
    elog2 = jnp.where(lane == i1, -jnp.inf, elog)
    m2 = jnp.max(elog2, axis=-1, keepdims=True)
    i2 = first_argmax(elog2, m2)
    den = jnp.sum(jnp.exp(elog - m1), axis=-1, keepdims=True)
    p1 = 1.0 / den
    p2 = jnp.exp(m2 - m1) / den
    psum = p1 + p2
    g1 = pg_top * (p1 / psum)
    g2 = pg_top * (p2 / psum)
    e1 = i1 - N_GROUPS
    e2 = i2 - N_GROUPS

    onehot = jnp.where((elane == e1) | (elane == e2), 1.0, 0.0)
    before = jnp.dot(tri_ref[...], onehot.astype(BF16), preferred_element_type=F32) + run_ref[0:1, :]
    r1 = jnp.sum(jnp.where(elane == e1, before, 0.0), axis=-1, keepdims=True).astype(jnp.int32)
    r2 = jnp.sum(jnp.where(elane == e2, before, 0.0), axis=-1, keepdims=True).astype(jnp.int32)
    run = run_ref[...] + jnp.sum(onehot, axis=0, keepdims=True)
    run_ref[...] = run
    cnt_ref[0] = run.astype(jnp.int32)

    ri = jnp.where(lane == 0, e1, jnp.where(lane == 1, e2, jnp.where(lane == 2, r1, jnp.where(lane == 3, r2, 0))))
    ri_ref[0] = jnp.transpose(ri)[0:SUBLANES, :]
    rf_ref[...] = jnp.where(lane == 0, g1, jnp.where(lane == 1, g2, 0.0))


def _xattn_route(h1, memkv, gx, w_xq, q_gain, w_xo, gm, wr, b_r, seq, mem_len, tm=512):
    n, d = h1.shape
    tiles_per_seq = seq // tm
    nt = n // tm
    assert X_HEAD_DIM == LANES
    tri = jnp.asarray(np.tril(np.ones((tm, tm), np.float32), -1), dtype=BF16)
    return pl.pallas_call(
        _xattn_route_kernel,
        grid=(nt,),
        in_specs=[
            pl.BlockSpec((tm, d), lambda i: (i, 0)),
            _const_spec((1, d)),
            _const_spec((d, X_WIDTH)),
            _const_spec((1, X_WIDTH)),
            pl.BlockSpec((mem_len, X_WIDTH), lambda i: (i // tiles_per_seq, 0)),
            pl.BlockSpec((mem_len, X_WIDTH), lambda i: (i // tiles_per_seq, 1)),
            _const_spec((X_WIDTH, d)),
            _const_spec((1, d)),
            _const_spec((d, 2 * LANES)),
            _const_spec((1, LANES)),
            _const_spec((tm, tm)),
        ],
        out_specs=[
            pl.BlockSpec((tm, d), lambda i: (i, 0)),
            pl.BlockSpec((tm, d // 2), lambda i: (i, 0)),
            pl.BlockSpec((1, SUBLANES, tm), lambda i: (i, 0, 0)),
            pl.BlockSpec((tm, LANES), lambda i: (i, 0)),
            pl.BlockSpec((1, SUBLANES, LANES), lambda i: (i, 0, 0)),
        ],
        out_shape=[
            jax.ShapeDtypeStruct((n, d), F32),
            jax.ShapeDtypeStruct((n, d // 2), jnp.uint32),
            jax.ShapeDtypeStruct((nt, SUBLANES, tm), jnp.int32),
            jax.ShapeDtypeStruct((n, LANES), F32),
            jax.ShapeDtypeStruct((nt, SUBLANES, LANES), jnp.int32),
        ],
        scratch_shapes=[pltpu.VMEM((SUBLANES, LANES), F32)],
        compiler_params=_cparams(("arbitrary",)),
        name="xattn_route",
    )(h1, gx.reshape(1, d), w_xq, q_gain, memkv, memkv, w_xo, gm.reshape(1, d), wr, b_r, tri)


def _dest_rows_kernel(padded_ref, ri_ref, d0_ref, d1_ref):
    for k, d_ref in enumerate((d0_ref, d1_ref)):
        e = ri_ref[:, k, :]
        row = ri_ref[:, 2 + k, :]
        for j in range(N_EXPERTS - 1):
            row = row + jnp.where(e > j, padded_ref[j], 0)
        d_ref[...] = row


def _dest_rows(ri, padded):
    nt, _, tm = ri.shape
    out = jax.ShapeDtypeStruct((nt, tm), jnp.int32)
    return pl.pallas_call(
        _dest_rows_kernel,
        in_specs=[pl.BlockSpec(memory_space=pltpu.SMEM), pl.BlockSpec(memory_space=pltpu.VMEM)],
        out_specs=[pl.BlockSpec(memory_space=pltpu.VMEM)] * 2,
        out_shape=[out, out],
        name="dest_rows",
    )(padded, ri)


def _dispatch_kernel(tail_ref, d0_ref, d1_ref, x_ref, xs_ref, zero_ref, sem, zsem):
    tm = x_ref.shape[0]

    @pl.when(pl.program_id(0) == 0)
    def _():
        zero_ref[...] = jnp.zeros_like(zero_ref)

        def zero_copy(e):
            start = pl.multiple_of(tail_ref[e], DISPATCH_BLOCK)
            return pltpu.make_async_copy(zero_ref, xs_ref.at[pl.ds(start, DISPATCH_BLOCK)], zsem)

        for e in range(N_EXPERTS):
            pl.when(tail_ref[e] >= 0)(lambda e=e: zero_copy(e).start())
        for e in range(N_EXPERTS):
            pl.when(tail_ref[e] >= 0)(lambda e=e: zero_copy(e).wait())

        def unused_copy(b):
            return pltpu.make_async_copy(zero_ref, xs_ref.at[pl.ds(pl.multiple_of(b * DISPATCH_BLOCK, DISPATCH_BLOCK),
                                                                   DISPATCH_BLOCK)], zsem)

        n_blocks = xs_ref.shape[0] // DISPATCH_BLOCK
        lax.fori_loop(tail_ref[N_EXPERTS], n_blocks, lambda b, c: (unused_copy(b).start(), c)[1], 0)
        lax.fori_loop(tail_ref[N_EXPERTS], n_blocks, lambda b, c: (unused_copy(b).wait(), c)[1], 0)

    def copy(t, k):
        return pltpu.make_async_copy(x_ref.at[pl.ds(t, 1)], xs_ref.at[pl.ds((d0_ref, d1_ref)[k][0, 0, t], 1)], sem)

    def issue(tb, carry):
        for j in range(DMA_UNROLL):
            copy(tb * DMA_UNROLL + j, 0).start()
            copy(tb * DMA_UNROLL + j, 1).start()
        return carry

    lax.fori_loop(0, tm // DMA_UNROLL, issue, 0)

    def drain(tb, carry):
        for j in range(DMA_UNROLL):
            copy(tb * DMA_UNROLL + j, 0).wait()
            copy(tb * DMA_UNROLL + j, 1).wait()
        return carry

    lax.fori_loop(0, tm // DMA_UNROLL, drain, 0)


def _dispatch(xp, dest, tail, rows):
    n, w = xp.shape
    nt, tm = dest[0].shape
    grid_spec = pltpu.PrefetchScalarGridSpec(
        num_scalar_prefetch=1,
        grid=(nt,),
        in_specs=[
            pl.BlockSpec((1, 1, tm), lambda i, tail: (i, 0, 0), memory_space=pltpu.SMEM),
            pl.BlockSpec((1, 1, tm), lambda i, tail: (i, 0, 0), memory_space=pltpu.SMEM),
            pl.BlockSpec((tm, w), lambda i, tail: (i, 0)),
        ],
        out_specs=pl.BlockSpec(memory_space=pl.ANY),
        scratch_shapes=[pltpu.VMEM((DISPATCH_BLOCK, w), xp.dtype), pltpu.SemaphoreType.DMA(()),
                        pltpu.SemaphoreType.DMA(())],
    )
    return pl.pallas_call(
        _dispatch_kernel,
        grid_spec=grid_spec,
        out_shape=jax.ShapeDtypeStruct((rows, w), xp.dtype),
        compiler_params=_cparams(("arbitrary",)),
        name="dispatch",
    )(tail, dest[0].reshape(nt, 1, tm), dest[1].reshape(nt, 1, tm), xp)


def _experts_kernel(blk_e_ref, blk_x_ref, nused_ref, first_ref, next_e_ref, xs_ref, wgu_hbm, wdn_hbm, y_ref,
                    wgu_f32, wdn_f32, wgu_bf, wdn_bf, sems):
    b = pl.program_id(0)
    half = xs_ref.shape[1]

    def fetch(e):
        return (pltpu.make_async_copy(wgu_hbm.at[e], wgu_f32, sems.at[0]),
                pltpu.make_async_copy(wdn_hbm.at[e], wdn_f32, sems.at[1]))

    @pl.when(b == 0)
    def _():
        for c in fetch(blk_e_ref[0]):
            c.start()

    @pl.when(first_ref[b] == 1)
    def _():
        for c in fetch(blk_e_ref[b]):
            c.wait()
        wgu_bf[...] = wgu_f32[...].astype(BF16)
        wdn_bf[...] = wdn_f32[...].astype(BF16)

        @pl.when(next_e_ref[b] >= 0)
        def _():
            for c in fetch(next_e_ref[b]):
                c.start()

    @pl.when(b < nused_ref[0])
    def _():
        x_lo, x_hi = _unpack_bf16_pair(xs_ref[...])
        gu = (jnp.dot(x_lo.astype(BF16), wgu_bf[0:half, :], preferred_element_type=F32)
              + jnp.dot(x_hi.astype(BF16), wgu_bf[half:, :], preferred_element_type=F32))
        g = gu[:, :D_EXPERT]
        u = gu[:, D_EXPERT:]
        act = (g * jax.nn.sigmoid(g) * u).astype(BF16)
        y = jnp.dot(act, wdn_bf[...], preferred_element_type=F32)
        y_ref[...] = _pack_bf16_pair(y[:, :half], y[:, half:])

    @pl.when(b >= nused_ref[0])
    def _():
        y_ref[...] = jnp.zeros_like(y_ref)


def _experts(xs, w_gu, w_dn, blk_e, blk_x, nused, first, next_e):
    rows, half = xs.shape
    d = 2 * half
    n_blocks = rows // DISPATCH_BLOCK
    grid_spec = pltpu.PrefetchScalarGridSpec(
        num_scalar_prefetch=5,
        grid=(n_blocks,),
        in_specs=[
            pl.BlockSpec((DISPATCH_BLOCK, half), lambda b, be, bx, *_: (bx[b], 0)),
            pl.BlockSpec(memory_space=pl.ANY),
            pl.BlockSpec(memory_space=pl.ANY),
        ],
        out_specs=pl.BlockSpec((DISPATCH_BLOCK, half), lambda b, *_: (b, 0)),
        scratch_shapes=[pltpu.VMEM((d, 2 * D_EXPERT), F32), pltpu.VMEM((D_EXPERT, d), F32),
                        pltpu.VMEM((d, 2 * D_EXPERT), BF16), pltpu.VMEM((D_EXPERT, d), BF16),
                        pltpu.SemaphoreType.DMA((2,))],
    )
    return pl.pallas_call(
        _experts_kernel,
        grid_spec=grid_spec,
        out_shape=jax.ShapeDtypeStruct((rows, half), jnp.uint32),
        compiler_params=_cparams(("arbitrary",)),
        name="experts",
    )(blk_e, blk_x, nused, first, next_e, xs, w_gu, w_dn)


def _combine_kernel(cur0_ref, cur1_ref, next0_ref, next1_ref, h_ref, rf_ref, y_ref, o_ref, ya_ref, yb_ref, sems):
    i = pl.program_id(0)
    nt = pl.num_programs(0)
    tm, d = h_ref.shape
    half = d // 2
    slot = i % 2

    def issue(d0_ref, d1_ref, s):
        def body(tb, carry):
            for j in range(DMA_UNROLL):
                t = tb * DMA_UNROLL + j
                pltpu.make_async_copy(y_ref.at[pl.ds(d0_ref[0, 0, t], 1)], ya_ref.at[s, pl.ds(t, 1)], sems.at[s]).start()
                pltpu.make_async_copy(y_ref.at[pl.ds(d1_ref[0, 0, t], 1)], yb_ref.at[s, pl.ds(t, 1)], sems.at[s]).start()
            return carry

        lax.fori_loop(0, tm // DMA_UNROLL, body, 0)

    pl.when(i == 0)(lambda: issue(cur0_ref, cur1_ref, 0))
    pl.when(i + 1 < nt)(lambda: issue(next0_ref, next1_ref, 1 - slot))
    pltpu.make_async_copy(y_ref.at[pl.ds(0, tm)], ya_ref.at[slot], sems.at[slot]).wait()
    pltpu.make_async_copy(y_ref.at[pl.ds(0, tm)], yb_ref.at[slot], sems.at[slot]).wait()
    g1 = rf_ref[:, 0:1]
    g2 = rf_ref[:, 1:2]
    a_lo, a_hi = _unpack_bf16_pair(ya_ref[slot])
    b_lo, b_hi = _unpack_bf16_pair(yb_ref[slot])
    o_ref[:, :half] = h_ref[:, :half] + (a_lo * g1 + b_lo * g2)
    o_ref[:, half:] = h_ref[:, half:] + (a_hi * g1 + b_hi * g2)


def _combine(h2, rf, dest, ys, tm=256):
    n, d = h2.shape
    nt = n // tm
    half = d // 2
    cur = pl.BlockSpec((1, 1, tm), lambda i: (i, 0, 0), memory_space=pltpu.SMEM)
    nxt = pl.BlockSpec((1, 1, tm), lambda i: (jnp.minimum(i + 1, nt - 1), 0, 0), memory_space=pltpu.SMEM)
    d0 = dest[0].reshape(nt, 1, tm)
    d1 = dest[1].reshape(nt, 1, tm)
    return pl.pallas_call(
        _combine_kernel,
        grid=(nt,),
        in_specs=[
            cur, cur, nxt, nxt,
            pl.BlockSpec((tm, d), lambda i: (i, 0)),
            pl.BlockSpec((tm, LANES), lambda i: (i, 0)),
            pl.BlockSpec(memory_space=pl.ANY),
        ],
        out_specs=pl.BlockSpec((tm, d), lambda i: (i, 0)),
        out_shape=jax.ShapeDtypeStruct((n, d), F32),
        scratch_shapes=[pltpu.VMEM((2, tm, half), jnp.uint32), pltpu.VMEM((2, tm, half), jnp.uint32),
                        pltpu.SemaphoreType.DMA((2,))],
        compiler_params=_cparams(("arbitrary",)),
        name="combine",
    )(d0, d1, d0, d1, h2, rf, ys)


def kernel(x, mem, rel_bias, norm_mix_g, w_in, conv_dw_w, conv_dw_b, conv_ln_g, conv_ln_b, w_conv_out, q_norm_g, k_norm_g, attn_sinks, w_attn_out, w_mix_out, norm_x_g, norm_mem_g, w_xq, w_xkv, xq_norm_g, xk_norm_g, w_xo, norm_moe_g, w_router_group, b_router_group, w_router_expert, b_router_expert, w_expert_gu, w_expert_down):
    batch, seq, d = x.shape
    mem_len = mem.shape[1]
    n = batch * seq
    h = x.reshape(n, d)
    for l in range(norm_mix_g.shape[0]):
        w = w_in[l].astype(BF16)
        q_gain = jnp.tile(q_norm_g[l].astype(F32) * HEAD_DIM ** -0.5, N_Q_HEADS)
        gain = jnp.concatenate([q_gain, jnp.tile(k_norm_g[l].astype(F32), N_KV_HEADS), jnp.ones((KV_WIDTH,), F32)])
        flag = jnp.concatenate([jnp.ones((ATTN_WIDTH + KV_WIDTH,), F32), jnp.zeros((KV_WIDTH,), F32)])
        conv_act, qkv, gates = _in_proj(h, norm_mix_g[l], w, gain, flag, conv_dw_w[l].reshape(CONV_WIDTH, CONV_CH),
                                        conv_dw_b[l], conv_ln_g[l], conv_ln_b[l], seq)
        attn_o = _swa(qkv, rel_bias, attn_sinks[l], batch, seq)
        h = _mix(conv_act, attn_o, gates, h, w_conv_out[l].astype(BF16), w_attn_out[l].astype(BF16),
                 w_mix_out[l].astype(BF16))

        memkv = _mem_proj(mem.reshape(batch * mem_len, d), norm_mem_g[l], w_xkv[l].astype(BF16),
                          jnp.tile(xk_norm_g[l].astype(F32), X_HEADS))
        xq_gain = jnp.tile(xq_norm_g[l].astype(F32) * X_HEAD_DIM ** -0.5, X_HEADS).reshape(1, -1)
        w_r = jnp.concatenate([w_router_group[l], w_router_expert[l]], axis=1).astype(F32)
        w_r = jnp.pad(w_r, ((0, 0), (0, LANES - w_r.shape[1])))
        wr_hi = w_r.astype(BF16)
        wr = jnp.concatenate([wr_hi, (w_r - wr_hi.astype(F32)).astype(BF16)], axis=1)
        b_r = jnp.concatenate([b_router_group[l], b_router_expert[l]]).astype(F32)
        b_r = jnp.pad(b_r, (0, LANES - b_r.shape[0])).reshape(1, LANES)
        h2, xp, ri, rf, cnt = _xattn_route(h, memkv, norm_x_g[l], w_xq[l].astype(BF16), xq_gain, w_xo[l].astype(BF16),
                                           norm_moe_g[l], wr, b_r, seq, mem_len)

        counts = cnt[-1, 0, N_GROUPS:N_GROUPS + N_EXPERTS]
        padded = (counts + DISPATCH_BLOCK - 1) // DISPATCH_BLOCK * DISPATCH_BLOCK
        pad_end = jnp.cumsum(padded)
        n_blocks = -(-(2 * n) // DISPATCH_BLOCK) + N_EXPERTS
        rows = n_blocks * DISPATCH_BLOCK
        nused = (pad_end[-1] // DISPATCH_BLOCK).astype(jnp.int32)
        blk = jnp.minimum(jnp.arange(n_blocks, dtype=jnp.int32), nused - 1)
        blk_e = jnp.minimum(jnp.sum(pad_end[None, :] <= (blk * DISPATCH_BLOCK)[:, None], axis=1), N_EXPERTS - 1).astype(jnp.int32)
        dest = _dest_rows(ri, padded.astype(jnp.int32))

        tail = jnp.concatenate([jnp.where(counts > 0, pad_end - DISPATCH_BLOCK, -1), nused.reshape(1)]).astype(jnp.int32)
        xs = _dispatch(xp, dest, tail, rows)
        first = jnp.concatenate([jnp.ones((1,), jnp.int32), (blk_e[1:] != blk_e[:-1]).astype(jnp.int32)])
        eids = jnp.arange(N_EXPERTS, dtype=jnp.int32)
        later = (eids[None, :] > blk_e[:, None]) & (counts[None, :] > 0)
        next_e = jnp.min(jnp.where(later, eids[None, :], N_EXPERTS), axis=1)
        next_e = jnp.where(next_e < N_EXPERTS, next_e, -1).astype(jnp.int32)
        ys = _experts(xs, w_expert_gu[l], w_expert_down[l], blk_e, blk, nused.reshape(1), first, next_e)
        h = _combine(h2, rf, dest, ys)
    return h.reshape(batch, seq, d)
```

```python
import functools
import math

import jax
import jax.numpy as jnp
import numpy as np
from jax import lax
from jax.experimental import pallas as pl
from jax.experimental.pallas import tpu as pltpu

EPS = 1e-6
NEG = -1e30

CONV_CH = 1024
CONV_WIDTH = 31
N_Q_HEADS = 16
N_KV_HEADS = 2
HEAD_DIM = 64
ATTN_WIDTH = N_Q_HEADS * HEAD_DIM
KV_WIDTH = N_KV_HEADS * HEAD_DIM
WINDOW = 128
NUM_BUCKETS = 32
MAX_DISTANCE = 128
X_HEADS = 4
X_HEAD_DIM = 128
X_WIDTH = X_HEADS * X_HEAD_DIM
N_GROUPS = 4
EXPERTS_PER_GROUP = 8
N_EXPERTS = N_GROUPS * EXPERTS_PER_GROUP
GROUP_SHIFT = EXPERTS_PER_GROUP.bit_length() - 1
assert 1 << GROUP_SHIFT == EXPERTS_PER_GROUP
D_EXPERT = 512
DISPATCH_BLOCK = 256

LANES = 128
SUBLANES = 8
MXU_K = 256
W_STAGE_ROWS = 64
SWA_BLOCKS_PER_STEP = 8
DMA_UNROLL = 8
CONV_HALO = 32
VMEM_LIMIT = 56 * 1024 * 1024

BF16 = jnp.bfloat16
F32 = jnp.float32


def _cparams(sem):
    return pltpu.CompilerParams(dimension_semantics=sem, vmem_limit_bytes=VMEM_LIMIT)


def _const_spec(shape):
    nd = len(shape)
    return pl.BlockSpec(shape, lambda *_: (0,) * nd, pipeline_mode=pl.Buffered(1))


def _rms_scale(x):
    return lax.rsqrt(jnp.mean(x * x, axis=-1, keepdims=True) + EPS)


def _group_sumsq(y, bd):
    sq = y * y
    hi = sq.astype(BF16)
    lo = (sq - hi.astype(F32)).astype(BF16)
    return (jnp.dot(hi, bd, preferred_element_type=F32) + jnp.dot(lo, bd, preferred_element_type=F32))


def _pack_bf16_pair(lo, hi):
    lo_bits = lax.bitcast_convert_type(lo.astype(BF16).astype(F32), jnp.uint32)
    hi_bits = lax.bitcast_convert_type(hi.astype(BF16).astype(F32), jnp.uint32)
    return (lo_bits >> 16) | hi_bits


def _unpack_bf16_pair(w):
    return (lax.bitcast_convert_type(w << 16, F32), lax.bitcast_convert_type(w & jnp.uint32(0xFFFF0000), F32))


def _head_norm_store(y, gain_ref, flag_ref, bd, o_ref, head_dim):
    for c in range(y.shape[1] // LANES):
        sl = slice(c * LANES, (c + 1) * LANES)
        yc = y[:, sl]
        ss = _group_sumsq(yc, bd)
        normed = yc * lax.rsqrt(ss * (1.0 / head_dim) + EPS) * gain_ref[:, sl]
        o_ref[:, sl] = jnp.where(flag_ref[:, sl] > 0.0, normed, yc).astype(o_ref.dtype)


def _opaque_zero(dep):
    bits = lax.shift_right_logical(lax.bitcast_convert_type(dep, jnp.uint32), jnp.uint32(32))
    return lax.bitcast_convert_type(bits, F32)[0:1, :]


def _conv_chunk(c, ext_ref, sh_ref, cv_ref, dww_ref, dwb_ref, row_chunk=128, after=None):
    tm = cv_ref.shape[0]
    off = CONV_HALO - (CONV_WIDTH - 1)
    span = tm + CONV_HALO - SUBLANES
    sl = slice(c * LANES, (c + 1) * LANES)
    for r in range(1, SUBLANES):
        sh_ref[c % 2, r - 1] = ext_ref[r:r + span, sl]
    zero = None if after is None else _opaque_zero(after)
    for rc in range(tm // row_chunk):
        acc = jnp.broadcast_to(dwb_ref[:, sl] if zero is None else dwb_ref[:, sl] + zero, (row_chunk, LANES))
        for j in range(CONV_WIDTH):
            q, r = divmod(off + j, SUBLANES)
            lo = SUBLANES * q + rc * row_chunk
            win = ext_ref[lo:lo + row_chunk, sl] if r == 0 else sh_ref[c % 2, r - 1, lo:lo + row_chunk, :]
            wj = dww_ref[j:j + 1, sl] if zero is None else dww_ref[j:j + 1, sl] + zero
            acc = acc + wj * win
        cv_ref[rc * row_chunk:(rc + 1) * row_chunk, sl] = acc


def _ln_silu(cv_ref, lng_ref, lnb_ref, act_ref):
    cv = cv_ref[...]
    mu = jnp.mean(cv, axis=-1, keepdims=True)
    cen = cv - mu
    var = jnp.mean(cen * cen, axis=-1, keepdims=True)
    ln = cen * lax.rsqrt(var + EPS) * lng_ref[...] + lnb_ref[...]
    act_ref[...] = (ln * jax.nn.sigmoid(ln)).astype(act_ref.dtype)


def _in_proj_kernel(x_ref, g_ref, w_hbm, gain_ref, flag_ref, bd_ref, dww_ref, dwb_ref, lng_ref, lnb_ref,
                    act_ref, qkv_ref, gate_ref, xn_ref, ext_ref, sh_ref, cv_ref, w_ref, stage_ref, wsem,
                    *, tn, tiles_per_seq):
    tm = x_ref.shape[0]

    @pl.when(pl.program_id(0) == 0)
    def _():
        rows = stage_ref.shape[1]
        n_chunks = w_ref.shape[0] // rows

        def fetch(c):
            return pltpu.make_async_copy(w_hbm.at[pl.ds(c * rows, rows)], stage_ref.at[c % 2], wsem.at[c % 2])

        fetch(0).start()
        for c in range(n_chunks):
            if c + 1 < n_chunks:
                fetch(c + 1).start()
            fetch(c).wait()
            w_ref[c * rows:(c + 1) * rows, :] = stage_ref[c % 2].astype(BF16)

    x = x_ref[...]
    xn_ref[...] = (x * _rms_scale(x) * g_ref[...]).astype(BF16)

    def proj(lo, width):
        return jnp.dot(xn_ref[...], w_ref[:, lo:lo + width], preferred_element_type=F32)

    @pl.when(pl.program_id(0) % tiles_per_seq == 0)
    def _():
        ext_ref[0:CONV_HALO, :] = jnp.zeros((CONV_HALO, CONV_CH), F32)

    n_u = act_ref.shape[1]
    n_qkv = qkv_ref.shape[1]
    g0 = 2 * n_u + n_qkv
    n_conv = CONV_CH // LANES
    n_gate = gate_ref.shape[1] // tn
    assert n_u == 2 * tn

    def proj_after(lo, after):
        xn = xn_ref[...]
        if after is not None:
            zb = jnp.broadcast_to(_opaque_zero(after), (SUBLANES, LANES)).astype(BF16)
            zb = jnp.tile(zb, (tm // SUBLANES, MXU_K // LANES))
            xn = jnp.concatenate([xn[:, :MXU_K] + zb, xn[:, MXU_K:]], axis=1)
        return jnp.dot(xn, w_ref[:, lo:lo + tn], preferred_element_type=F32)

    ext_ref[CONV_HALO:, 0:tn] = proj(0, tn) * jax.nn.sigmoid(proj(n_u, tn))
    stages = [("a", tn), ("b", n_u + tn)] + [("gate", g0 + c * tn) for c in range(n_gate)]
    last_rows = (slice(tm - SUBLANES, tm), slice(tn - LANES, tn))
    mxu_done = None
    conv_done = None
    glu_a = None
    for c, (kind, lo) in enumerate(stages):
        if c < n_conv:
            _conv_chunk(c, ext_ref, sh_ref, cv_ref, dww_ref, dwb_ref, after=mxu_done)
        y = proj_after(lo, conv_done if c < n_conv else None)
        mxu_done = y[last_rows]
        if kind == "a":
            glu_a = y
        elif kind == "b":
            ext_ref[CONV_HALO:, tn:2 * tn] = glu_a * jax.nn.sigmoid(y)
        else:
            col = lo - g0
            gate_ref[:, col:col + tn] = y.astype(gate_ref.dtype)
        if c < n_conv:
            conv_done = cv_ref[tm - SUBLANES:tm, c * LANES:(c + 1) * LANES]
    ext_ref[0:CONV_HALO, :] = ext_ref[tm:tm + CONV_HALO, :]
    _ln_silu(cv_ref, lng_ref, lnb_ref, act_ref)
    _head_norm_store(proj(2 * n_u, n_qkv), gain_ref, flag_ref, bd_ref[...], qkv_ref, HEAD_DIM)


def _in_proj(x, g, w, gain, flag, dw_w, dw_b, ln_g, ln_b, seq, tm=256, tn=512):
    m, d = x.shape
    n_qkv = ATTN_WIDTH + 2 * KV_WIDTH
    n_gate = w.shape[1] - 2 * CONV_CH - n_qkv
    assert seq % tm == 0
    row = lambda i: (i, 0)
    vec = _const_spec((1, CONV_CH))
    return pl.pallas_call(
        functools.partial(_in_proj_kernel, tn=tn, tiles_per_seq=seq // tm),
        grid=(m // tm,),
        in_specs=[pl.BlockSpec((tm, d), row), _const_spec((1, d)), pl.BlockSpec(memory_space=pl.ANY), _const_spec((1, n_qkv)),
                  _const_spec((1, n_qkv)), _const_spec((LANES, LANES)), _const_spec((CONV_WIDTH, CONV_CH)), vec, vec, vec],
        out_specs=[pl.BlockSpec((tm, CONV_CH), row), pl.BlockSpec((tm, n_qkv), row), pl.BlockSpec((tm, n_gate), row)],
        out_shape=[jax.ShapeDtypeStruct((m, CONV_CH), BF16), jax.ShapeDtypeStruct((m, n_qkv), BF16),
                   jax.ShapeDtypeStruct((m, n_gate), BF16)],
        scratch_shapes=[pltpu.VMEM((tm, d), BF16), pltpu.VMEM((tm + CONV_HALO, CONV_CH), F32),
                        pltpu.VMEM((2, SUBLANES - 1, tm + CONV_HALO - SUBLANES, LANES), F32), pltpu.VMEM((tm, CONV_CH), F32),
                        pltpu.VMEM(w.shape, BF16), pltpu.VMEM((2, W_STAGE_ROWS) + w.shape[1:], F32),
                        pltpu.SemaphoreType.DMA((2,))],
        compiler_params=_cparams(("arbitrary",)),
        name="in_proj",
    )(x, g.reshape(1, d), w, gain.reshape(1, -1), flag.reshape(1, -1), _block_diag_ones(HEAD_DIM),
      dw_w, dw_b.reshape(1, -1), ln_g.reshape(1, -1), ln_b.reshape(1, -1))


def _mem_proj_kernel(x_ref, g_ref, w_ref, kgain_ref, o_ref):
    x = x_ref[...]
    xn = (x * _rms_scale(x) * g_ref[...]).astype(BF16)
    y = jnp.dot(xn, w_ref[...], preferred_element_type=F32)
    n_k = kgain_ref.shape[1]
    for c in range(y.shape[1] // LANES):
        sl = slice(c * LANES, (c + 1) * LANES)
        yc = y[:, sl]
        if c * LANES < n_k:
            yc = yc * _rms_scale(yc) * kgain_ref[:, sl]
        o_ref[:, sl] = yc.astype(o_ref.dtype)


def _mem_proj(x, g, w, kgain, tm=256):
    m, d = x.shape
    n = w.shape[1]
    assert X_HEAD_DIM == LANES
    return pl.pallas_call(
        _mem_proj_kernel,
        grid=(m // tm,),
        in_specs=[pl.BlockSpec((tm, d), lambda i: (i, 0)), _const_spec((1, d)), _const_spec(w.shape),
                  _const_spec((1, kgain.shape[0]))],
        out_specs=pl.BlockSpec((tm, n), lambda i: (i, 0)),
        out_shape=jax.ShapeDtypeStruct((m, n), BF16),
        compiler_params=_cparams(("parallel",)),
        name="mem_proj",
    )(x, g.reshape(1, d), w, kgain.reshape(1, -1))


def _block_diag_ones(group):
    r = np.arange(LANES)
    return jnp.asarray((r[:, None] // group) == (r[None, :] // group), dtype=BF16)


def _swa_kernel(q_ref, kp_ref, kc_ref, vp_ref, vc_ref, bias_ref, o_ref, *, n_chunks, qb):
    lane = lax.broadcasted_iota(jnp.int32, (qb, LANES), 1)
    lane2 = lax.broadcasted_iota(jnp.int32, (2 * qb, LANES), 1)
    row2 = lax.broadcasted_iota(jnp.int32, (2 * qb, LANES), 0)
    chunks_per_kv = n_chunks // N_KV_HEADS

    def both_halves(prev, cur):
        t = jnp.concatenate([prev, cur], axis=0).astype(F32)
        t = jnp.where(row2 == 0, 0.0, t)
        swapped = pltpu.roll(t, HEAD_DIM, 1)
        return (jnp.where(lane2 < HEAD_DIM, t, swapped).astype(BF16), jnp.where(lane2 < HEAD_DIM, swapped, t).astype(BF16))

    for sub in range(q_ref.shape[0] // qb):
        rows = slice(sub * qb, (sub + 1) * qb)
        before = slice((sub - 1) * qb, sub * qb)
        ks = both_halves(kp_ref[...] if sub == 0 else kc_ref[before, :], kc_ref[rows, :])
        vs = both_halves(vp_ref[...] if sub == 0 else vc_ref[before, :], vc_ref[rows, :])
        table = jnp.where(pl.program_id(1) == 0, 1, 0) if sub == 0 else 0
        for c in range(n_chunks):
            q2 = q_ref[rows, c * LANES:(c + 1) * LANES]
            zero = jnp.zeros_like(q2)
            qs = jnp.concatenate([jnp.where(lane < HEAD_DIM, q2, zero), jnp.where(lane >= HEAD_DIM, q2, zero)], axis=0)
            s = lax.dot_general(qs, ks[c // chunks_per_kv], (((1,), (1,)), ((), ())), preferred_element_type=F32)
            s = s + bias_ref[table, c]
            p = jnp.exp(s - jnp.max(s, axis=-1, keepdims=True))
            den = jnp.sum(p, axis=-1, keepdims=True)
            o2 = jnp.dot(p.astype(BF16), vs[c // chunks_per_kv], preferred_element_type=F32) / den
            o = jnp.where(lane < HEAD_DIM, o2[:qb], o2[qb:])
            o_ref[rows, c * LANES:(c + 1) * LANES] = o.astype(o_ref.dtype)


def _t5_bucket_np(dist):
    n = np.maximum(dist, 0)
    max_exact = NUM_BUCKETS // 2
    large = max_exact + (np.log(np.maximum(n, 1).astype(np.float32) / max_exact)
                         / math.log(MAX_DISTANCE / max_exact) * (NUM_BUCKETS - max_exact)).astype(np.int32)
    large = np.minimum(large, NUM_BUCKETS - 1)
    return np.where(n < max_exact, n, large)


def _swa(qkv, rel_bias, sinks, batch, seq):
    qb = WINDOW
    nb = seq // qb
    n_chunks = N_Q_HEADS // 2
    qi = np.arange(qb)[:, None]
    kj = np.arange(2 * qb)[None, :]
    dist = qi + qb - kj
    valid = (dist >= 0) & (dist < WINDOW)
    onehot = jnp.asarray(_t5_bucket_np(dist)[:, :, None] == np.arange(NUM_BUCKETS), dtype=F32)
    bias = jnp.einsum('qkb,bh->qkh', onehot, rel_bias.astype(F32), precision=lax.Precision.HIGHEST)
    bias = jnp.where(valid[:, :, None], bias, NEG).transpose(2, 0, 1)
    bias = bias.reshape(n_chunks, 2 * qb, 2 * qb)
    sink_rows = jnp.repeat(sinks.astype(F32).reshape(n_chunks, 2), qb, axis=1)
    bias = bias.at[:, :, 0].set(sink_rows)
    bias = jnp.stack([bias, jnp.where((kj < qb) & (kj > 0), NEG, bias)])
    kcol = ATTN_WIDTH // LANES
    vcol = kcol + 1
    sub = math.gcd(SWA_BLOCKS_PER_STEP, nb)
    steps = nb // sub

    def prev(b, n):
        return (b * nb + jnp.maximum(n * sub - 1, 0))

    return pl.pallas_call(
        functools.partial(_swa_kernel, n_chunks=n_chunks, qb=qb),
        grid=(batch, steps),
        in_specs=[
            pl.BlockSpec((sub * qb, ATTN_WIDTH), lambda b, n: (b * steps + n, 0)),
            pl.BlockSpec((qb, LANES), lambda b, n: (prev(b, n), kcol)),
            pl.BlockSpec((sub * qb, LANES), lambda b, n: (b * steps + n, kcol)),
            pl.BlockSpec((qb, LANES), lambda b, n: (prev(b, n), vcol)),
            pl.BlockSpec((sub * qb, LANES), lambda b, n: (b * steps + n, vcol)),
            _const_spec((2, n_chunks, 2 * qb, 2 * qb)),
        ],
        out_specs=pl.BlockSpec((sub * qb, ATTN_WIDTH), lambda b, n: (b * steps + n, 0)),
        out_shape=jax.ShapeDtypeStruct((batch * seq, ATTN_WIDTH), BF16),
        compiler_params=_cparams(("parallel", "parallel")),
        name="swa",
    )(qkv, qkv, qkv, qkv, qkv, bias)


def _mix_kernel(c_ref, a_ref, g0_ref, g1_ref, x_ref, wc_ref, wa_ref, wm_ref, o_ref):
    y_conv = jnp.dot(c_ref[...], wc_ref[...], preferred_element_type=F32)
    y_attn = jnp.dot(a_ref[...], wa_ref[...], preferred_element_type=F32)
    merged = jax.nn.sigmoid(g0_ref[...].astype(F32)) * y_conv + jax.nn.sigmoid(g1_ref[...].astype(F32)) * y_attn
    o_ref[...] = x_ref[...] + jnp.dot(merged.astype(BF16), wm_ref[...], preferred_element_type=F32)


def _mix(conv_act, attn_o, gates, x, w_conv_out, w_attn_out, w_mix_out, tm=256):
    n, d = x.shape
    return pl.pallas_call(
        _mix_kernel,
        grid=(n // tm,),
        in_specs=[
            pl.BlockSpec((tm, CONV_CH), lambda i: (i, 0)),
            pl.BlockSpec((tm, ATTN_WIDTH), lambda i: (i, 0)),
            pl.BlockSpec((tm, d), lambda i: (i, 0)),
            pl.BlockSpec((tm, d), lambda i: (i, 1)),
            pl.BlockSpec((tm, d), lambda i: (i, 0)),
            _const_spec((CONV_CH, d)),
            _const_spec((ATTN_WIDTH, d)),
            _const_spec((d, d)),
        ],
        out_specs=pl.BlockSpec((tm, d), lambda i: (i, 0)),
        out_shape=jax.ShapeDtypeStruct((n, d), F32),
        compiler_params=_cparams(("parallel",)),
        name="mix",
    )(conv_act, attn_o, gates, gates, x, w_conv_out, w_attn_out, w_mix_out)


def _xattn_route_kernel(h_ref, gx_ref, wq_ref, qgain_ref, k_ref, v_ref, wo_ref, gm_ref,
                        wr_ref, br_ref, tri_ref,
                        h2_ref, xp_ref, ri_ref, rf_ref, cnt_ref, run_ref):
    tm, d = h_ref.shape
    half = d // 2

    @pl.when(pl.program_id(0) == 0)
    def _():
        run_ref[...] = jnp.zeros_like(run_ref)

    h = h_ref[...]
    xn = (h * _rms_scale(h) * gx_ref[...]).astype(BF16)
    q = jnp.dot(xn, wq_ref[...], preferred_element_type=F32)
    heads = []
    for hd in range(X_HEADS):
        sl = slice(hd * X_HEAD_DIM, (hd + 1) * X_HEAD_DIM)
        qh = q[:, sl]
        qh = (qh * _rms_scale(qh) * qgain_ref[:, sl]).astype(BF16)
        s = lax.dot_general(qh, k_ref[:, sl], (((1,), (1,)), ((), ())), preferred_element_type=F32)
        p = jnp.exp(s - jnp.max(s, axis=-1, keepdims=True))
        den = jnp.sum(p, axis=-1, keepdims=True)
        heads.append((jnp.dot(p.astype(BF16), v_ref[:, sl], preferred_element_type=F32) / den).astype(BF16))
    o = jnp.concatenate(heads, axis=-1)
    h2 = h + jnp.dot(o, wo_ref[...], preferred_element_type=F32)
    h2_ref[...] = h2

    xn2 = h2 * _rms_scale(h2) * gm_ref[...]
    xp_ref[...] = _pack_bf16_pair(xn2[:, :half], xn2[:, half:])

    x_hi = xn2.astype(BF16)
    x_lo = (xn2 - x_hi.astype(F32)).astype(BF16)
    r = jnp.dot(jnp.concatenate([x_hi, x_lo], axis=0), wr_ref[...], preferred_element_type=F32)
    lg = (r[:tm, :LANES] + r[:tm, LANES:]) + (r[tm:, :LANES] + r[tm:, LANES:]) + br_ref[...]

    lane = lax.broadcasted_iota(jnp.int32, (tm, LANES), 1)
    lane_f = lane.astype(F32)
    big = jnp.float32(LANES)

    def first_argmax(vals, vmax):
        idx = jnp.min(jnp.where(vals == vmax, lane_f, big), axis=-1, keepdims=True)
        return idx.astype(jnp.int32)

    glog = jnp.where(lane < N_GROUPS, lg, -jnp.inf)
    gmax = jnp.max(glog, axis=-1, keepdims=True)
    grp = first_argmax(glog, gmax)
    pg_top = 1.0 / jnp.sum(jnp.exp(glog - gmax), axis=-1, keepdims=True)
    elane = lane - N_GROUPS
    emask = (elane >= 0) & (elane < N_EXPERTS) & ((elane >> GROUP_SHIFT) == grp)
    elog = jnp.where(emask, lg, -jnp.inf)
    m1 = jnp.max(elog, axis=-1, keepdims=True)
    i1 = first_argmax(elog, m1)
    elog2 = jnp.where(lane == i1, -jnp.inf, elog)
    m2 = jnp.max(elog2, axis=-1, keepdims=True)
    i2 = first_argmax(elog2, m2)
    den = jnp.sum(jnp.exp(elog - m1), axis=-1, keepdims=True)
    p1 = 1.0 / den
    p2 = jnp.exp(m2 - m1) / den
    psum = p1 + p2
    g1 = pg_top * (p1 / psum)
    g2 = pg_top * (p2 / psum)
    e1 = i1 - N_GROUPS
    e2 = i2 - N_GROUPS

    onehot = jnp.where((elane == e1) | (elane == e2), 1.0, 0.0)
    before = jnp.dot(tri_ref[...], onehot.astype(BF16), preferred_element_type=F32) + run_ref[0:1, :]
    r1 = jnp.sum(jnp.where(elane == e1, before, 0.0), axis=-1, keepdims=True).astype(jnp.int32)
    r2 = jnp.sum(jnp.where(elane == e2, before, 0.0), axis=-1, keepdims=True).astype(jnp.int32)
    run = run_ref[...] + jnp.sum(onehot, axis=0, keepdims=True)
    run_ref[...] = run
    cnt_ref[0] = run.astype(jnp.int32)

    ri = jnp.where(lane == 0, e1, jnp.where(lane == 1, e2, jnp.where(lane == 2, r1, jnp.where(lane == 3, r2, 0))))
    ri_ref[0] = jnp.transpose(ri)[0:SUBLANES, :]
    rf_ref[...] = jnp.where(lane == 0, g1, jnp.where(lane == 1, g2, 0.0))


def _xattn_route(h1, memkv, gx, w_xq, q_gain, w_xo, gm, wr, b_r, seq, mem_len, tm=512):
    n, d = h1.shape
    tiles_per_seq = seq // tm
    nt = n // tm
    assert X_HEAD_DIM == LANES
    tri = jnp.asarray(np.tril(np.ones((tm, tm), np.float32), -1), dtype=BF16)
    return pl.pallas_call(
        _xattn_route_kernel,
        grid=(nt,),
        in_specs=[
            pl.BlockSpec((tm, d), lambda i: (i, 0)),
            _const_spec((1, d)),
            _const_spec((d, X_WIDTH)),
            _const_spec((1, X_WIDTH)),
            pl.BlockSpec((mem_len, X_WIDTH), lambda i: (i // tiles_per_seq, 0)),
            pl.BlockSpec((mem_len, X_WIDTH), lambda i: (i // tiles_per_seq, 1)),
            _const_spec((X_WIDTH, d)),
            _const_spec((1, d)),
            _const_spec((d, 2 * LANES)),
            _const_spec((1, LANES)),
            _const_spec((tm, tm)),
        ],
        out_specs=[
            pl.BlockSpec((tm, d), lambda i: (i, 0)),
            pl.BlockSpec((tm, d // 2), lambda i: (i, 0)),
            pl.BlockSpec((1, SUBLANES, tm), lambda i: (i, 0, 0)),
            pl.BlockSpec((tm, LANES), lambda i: (i, 0)),
            pl.BlockSpec((1, SUBLANES, LANES), lambda i: (i, 0, 0)),
        ],
        out_shape=[
            jax.ShapeDtypeStruct((n, d), F32),
            jax.ShapeDtypeStruct((n, d // 2), jnp.uint32),
            jax.ShapeDtypeStruct((nt, SUBLANES, tm), jnp.int32),
            jax.ShapeDtypeStruct((n, LANES), F32),
            jax.ShapeDtypeStruct((nt, SUBLANES, LANES), jnp.int32),
        ],
        scratch_shapes=[pltpu.VMEM((SUBLANES, LANES), F32)],
        compiler_params=_cparams(("arbitrary",)),
        name="xattn_route",
    )(h1, gx.reshape(1, d), w_xq, q_gain, memkv, memkv, w_xo, gm.reshape(1, d), wr, b_r, tri)


def _dest_rows_kernel(padded_ref, ri_ref, d0_ref, d1_ref):
    for k, d_ref in enumerate((d0_ref, d1_ref)):
        e = ri_ref[:, k, :]
        row = ri_ref[:, 2 + k, :]
        for j in range(N_EXPERTS - 1):
            row = row + jnp.where(e > j, padded_ref[j], 0)
        d_ref[...] = row


def _dest_rows(ri, padded):
    nt, _, tm = ri.shape
    out = jax.ShapeDtypeStruct((nt, tm), jnp.int32)
    return pl.pallas_call(
        _dest_rows_kernel,
        in_specs=[pl.BlockSpec(memory_space=pltpu.SMEM), pl.BlockSpec(memory_space=pltpu.VMEM)],
        out_specs=[pl.BlockSpec(memory_space=pltpu.VMEM)] * 2,
        out_shape=[out, out],
        name="dest_rows",
    )(padded, ri)


def _dispatch_kernel(tail_ref, d0_ref, d1_ref, x_ref, xs_ref, zero_ref, sem, zsem):
    tm = x_ref.shape[0]

    @pl.when(pl.program_id(0) == 0)
    def _():
        zero_ref[...] = jnp.zeros_like(zero_ref)

        def zero_copy(e):
            start = pl.multiple_of(tail_ref[e], DISPATCH_BLOCK)
            return pltpu.make_async_copy(zero_ref, xs_ref.at[pl.ds(start, DISPATCH_BLOCK)], zsem)

        for e in range(N_EXPERTS):
            pl.when(tail_ref[e] >= 0)(lambda e=e: zero_copy(e).start())
        for e in range(N_EXPERTS):
            pl.when(tail_ref[e] >= 0)(lambda e=e: zero_copy(e).wait())

        def unused_copy(b):
            return pltpu.make_async_copy(zero_ref, xs_ref.at[pl.ds(pl.multiple_of(b * DISPATCH_BLOCK, DISPATCH_BLOCK),
                                                                   DISPATCH_BLOCK)], zsem)

        n_blocks = xs_ref.shape[0] // DISPATCH_BLOCK
        lax.fori_loop(tail_ref[N_EXPERTS], n_blocks, lambda b, c: (unused_copy(b).start(), c)[1], 0)
        lax.fori_loop(tail_ref[N_EXPERTS], n_blocks, lambda b, c: (unused_copy(b).wait(), c)[1], 0)

    def copy(t, k):
        return pltpu.make_async_copy(x_ref.at[pl.ds(t, 1)], xs_ref.at[pl.ds((d0_ref, d1_ref)[k][0, 0, t], 1)], sem)

    def issue(tb, carry):
        for j in range(DMA_UNROLL):
            copy(tb * DMA_UNROLL + j, 0).start()
            copy(tb * DMA_UNROLL + j, 1).start()
        return carry

    lax.fori_loop(0, tm // DMA_UNROLL, issue, 0)

    def drain(tb, carry):
        for j in range(DMA_UNROLL):
            copy(tb * DMA_UNROLL + j, 0).wait()
            copy(tb * DMA_UNROLL + j, 1).wait()
        return carry

    lax.fori_loop(0, tm // DMA_UNROLL, drain, 0)


def _dispatch(xp, dest, tail, rows):
    n, w = xp.shape
    nt, tm = dest[0].shape
    grid_spec = pltpu.PrefetchScalarGridSpec(
        num_scalar_prefetch=1,
        grid=(nt,),
        in_specs=[
            pl.BlockSpec((1, 1, tm), lambda i, tail: (i, 0, 0), memory_space=pltpu.SMEM),
            pl.BlockSpec((1, 1, tm), lambda i, tail: (i, 0, 0), memory_space=pltpu.SMEM),
            pl.BlockSpec((tm, w), lambda i, tail: (i, 0)),
        ],
        out_specs=pl.BlockSpec(memory_space=pl.ANY),
        scratch_shapes=[pltpu.VMEM((DISPATCH_BLOCK, w), xp.dtype), pltpu.SemaphoreType.DMA(()),
                        pltpu.SemaphoreType.DMA(())],
    )
    return pl.pallas_call(
        _dispatch_kernel,
        grid_spec=grid_spec,
        out_shape=jax.ShapeDtypeStruct((rows, w), xp.dtype),
        compiler_params=_cparams(("arbitrary",)),
        name="dispatch",
    )(tail, dest[0].reshape(nt, 1, tm), dest[1].reshape(nt, 1, tm), xp)


def _experts_kernel(blk_e_ref, blk_x_ref, nused_ref, first_ref, next_e_ref, xs_ref, wgu_hbm, wdn_hbm, y_ref,
                    wgu_f32, wdn_f32, wgu_bf, wdn_bf, sems):
    b = pl.program_id(0)
    half = xs_ref.shape[1]

    def fetch(e):
        return (pltpu.make_async_copy(wgu_hbm.at[e], wgu_f32, sems.at[0]),
                pltpu.make_async_copy(wdn_hbm.at[e], wdn_f32, sems.at[1]))

    @pl.when(b == 0)
    def _():
        for c in fetch(blk_e_ref[0]):
            c.start()

    @pl.when(first_ref[b] == 1)
    def _():
        for c in fetch(blk_e_ref[b]):
            c.wait()
        wgu_bf[...] = wgu_f32[...].astype(BF16)
        wdn_bf[...] = wdn_f32[...].astype(BF16)

        @pl.when(next_e_ref[b] >= 0)
        def _():
            for c in fetch(next_e_ref[b]):
                c.start()

    @pl.when(b < nused_ref[0])
    def _():
        x_lo, x_hi = _unpack_bf16_pair(xs_ref[...])
        gu = (jnp.dot(x_lo.astype(BF16), wgu_bf[0:half, :], preferred_element_type=F32)
              + jnp.dot(x_hi.astype(BF16), wgu_bf[half:, :], preferred_element_type=F32))
        g = gu[:, :D_EXPERT]
        u = gu[:, D_EXPERT:]
        act = (g * jax.nn.sigmoid(g) * u).astype(BF16)
        y = jnp.dot(act, wdn_bf[...], preferred_element_type=F32)
        y_ref[...] = _pack_bf16_pair(y[:, :half], y[:, half:])

    @pl.when(b >= nused_ref[0])
    def _():
        y_ref[...] = jnp.zeros_like(y_ref)


def _experts(xs, w_gu, w_dn, blk_e, blk_x, nused, first, next_e):
    rows, half = xs.shape
    d = 2 * half
    n_blocks = rows // DISPATCH_BLOCK
    grid_spec = pltpu.PrefetchScalarGridSpec(
        num_scalar_prefetch=5,
        grid=(n_blocks,),
        in_specs=[
            pl.BlockSpec((DISPATCH_BLOCK, half), lambda b, be, bx, *_: (bx[b], 0)),
            pl.BlockSpec(memory_space=pl.ANY),
            pl.BlockSpec(memory_space=pl.ANY),
        ],
        out_specs=pl.BlockSpec((DISPATCH_BLOCK, half), lambda b, *_: (b, 0)),
        scratch_shapes=[pltpu.VMEM((d, 2 * D_EXPERT), F32), pltpu.VMEM((D_EXPERT, d), F32),
                        pltpu.VMEM((d, 2 * D_EXPERT), BF16), pltpu.VMEM((D_EXPERT, d), BF16),
                        pltpu.SemaphoreType.DMA((2,))],
    )
    return pl.pallas_call(
        _experts_kernel,
        grid_spec=grid_spec,
        out_shape=jax.ShapeDtypeStruct((rows, half), jnp.uint32),
        compiler_params=_cparams(("arbitrary",)),
        name="experts",
    )(blk_e, blk_x, nused, first, next_e, xs, w_gu, w_dn)


def _combine_kernel(cur0_ref, cur1_ref, next0_ref, next1_ref, h_ref, rf_ref, y_ref, o_ref, ya_ref, yb_ref, sems):
    i = pl.program_id(0)
    nt = pl.num_programs(0)
    tm, d = h_ref.shape
    half = d // 2
    slot = i % 2

    def issue(d0_ref, d1_ref, s):
        def body(tb, carry):
            for j in range(DMA_UNROLL):
                t = tb * DMA_UNROLL + j
                pltpu.make_async_copy(y_ref.at[pl.ds(d0_ref[0, 0, t], 1)], ya_ref.at[s, pl.ds(t, 1)], sems.at[s]).start()
                pltpu.make_async_copy(y_ref.at[pl.ds(d1_ref[0, 0, t], 1)], yb_ref.at[s, pl.ds(t, 1)], sems.at[s]).start()
            return carry

        lax.fori_loop(0, tm // DMA_UNROLL, body, 0)

    pl.when(i == 0)(lambda: issue(cur0_ref, cur1_ref, 0))
    pl.when(i + 1 < nt)(lambda: issue(next0_ref, next1_ref, 1 - slot))
    pltpu.make_async_copy(y_ref.at[pl.ds(0, tm)], ya_ref.at[slot], sems.at[slot]).wait()
    pltpu.make_async_copy(y_ref.at[pl.ds(0, tm)], yb_ref.at[slot], sems.at[slot]).wait()
    g1 = rf_ref[:, 0:1]
    g2 = rf_ref[:, 1:2]
    a_lo, a_hi = _unpack_bf16_pair(ya_ref[slot])
    b_lo, b_hi = _unpack_bf16_pair(yb_ref[slot])
    o_ref[:, :half] = h_ref[:, :half] + (a_lo * g1 + b_lo * g2)
    o_ref[:, half:] = h_ref[:, half:] + (a_hi * g1 + b_hi * g2)


def _combine(h2, rf, dest, ys, tm=256):
    n, d = h2.shape
    nt = n // tm
    half = d // 2
    cur = pl.BlockSpec((1, 1, tm), lambda i: (i, 0, 0), memory_space=pltpu.SMEM)
    nxt = pl.BlockSpec((1, 1, tm), lambda i: (jnp.minimum(i + 1, nt - 1), 0, 0), memory_space=pltpu.SMEM)
    d0 = dest[0].reshape(nt, 1, tm)
    d1 = dest[1].reshape(nt, 1, tm)
    return pl.pallas_call(
        _combine_kernel,
        grid=(nt,),
        in_specs=[
            cur, cur, nxt, nxt,
            pl.BlockSpec((tm, d), lambda i: (i, 0)),
            pl.BlockSpec((tm, LANES), lambda i: (i, 0)),
            pl.BlockSpec(memory_space=pl.ANY),
        ],
        out_specs=pl.BlockSpec((tm, d), lambda i: (i, 0)),
        out_shape=jax.ShapeDtypeStruct((n, d), F32),
        scratch_shapes=[pltpu.VMEM((2, tm, half), jnp.uint32), pltpu.VMEM((2, tm, half), jnp.uint32),
                        pltpu.SemaphoreType.DMA((2,))],
        compiler_params=_cparams(("arbitrary",)),
        name="combine",
    )(d0, d1, d0, d1, h2, rf, ys)


def kernel(x, mem, rel_bias, norm_mix_g, w_in, conv_dw_w, conv_dw_b, conv_ln_g, conv_ln_b, w_conv_out, q_norm_g, k_norm_g, attn_sinks, w_attn_out, w_mix_out, norm_x_g, norm_mem_g, w_xq, w_xkv, xq_norm_g, xk_norm_g, w_xo, norm_moe_g, w_router_group, b_router_group, w_router_expert, b_router_expert, w_expert_gu, w_expert_down):
    batch, seq, d = x.shape
    mem_len = mem.shape[1]
    n = batch * seq
    h = x.reshape(n, d)
    for l in range(norm_mix_g.shape[0]):
        q_gain = jnp.tile(q_norm_g[l].astype(F32) * HEAD_DIM ** -0.5, N_Q_HEADS)
        gain = jnp.concatenate([q_gain, jnp.tile(k_norm_g[l].astype(F32), N_KV_HEADS), jnp.ones((KV_WIDTH,), F32)])
        flag = jnp.concatenate([jnp.ones((ATTN_WIDTH + KV_WIDTH,), F32), jnp.zeros((KV_WIDTH,), F32)])
        conv_act, qkv, gates = _in_proj(h, norm_mix_g[l], w_in[l], gain, flag, conv_dw_w[l].reshape(CONV_WIDTH, CONV_CH),
                                        conv_dw_b[l], conv_ln_g[l], conv_ln_b[l], seq)
        attn_o = _swa(qkv, rel_bias, attn_sinks[l], batch, seq)
        h = _mix(conv_act, attn_o, gates, h, w_conv_out[l].astype(BF16), w_attn_out[l].astype(BF16),
                 w_mix_out[l].astype(BF16))

        memkv = _mem_proj(mem.reshape(batch * mem_len, d), norm_mem_g[l], w_xkv[l].astype(BF16),
                          jnp.tile(xk_norm_g[l].astype(F32), X_HEADS))
        xq_gain = jnp.tile(xq_norm_g[l].astype(F32) * X_HEAD_DIM ** -0.5, X_HEADS).reshape(1, -1)
        w_r = jnp.concatenate([w_router_group[l], w_router_expert[l]], axis=1).astype(F32)
        w_r = jnp.pad(w_r, ((0, 0), (0, LANES - w_r.shape[1])))
        wr_hi = w_r.astype(BF16)
        wr = jnp.concatenate([wr_hi, (w_r - wr_hi.astype(F32)).astype(BF16)], axis=1)
        b_r = jnp.concatenate([b_router_group[l], b_router_expert[l]]).astype(F32)
        b_r = jnp.pad(b_r, (0, LANES - b_r.shape[0])).reshape(1, LANES)
        h2, xp, ri, rf, cnt = _xattn_route(h, memkv, norm_x_g[l], w_xq[l].astype(BF16), xq_gain, w_xo[l].astype(BF16),
                                           norm_moe_g[l], wr, b_r, seq, mem_len)

        counts = cnt[-1, 0, N_GROUPS:N_GROUPS + N_EXPERTS]
        padded = (counts + DISPATCH_BLOCK - 1) // DISPATCH_BLOCK * DISPATCH_BLOCK
        pad_end = jnp.cumsum(padded)
        n_blocks = -(-(2 * n) // DISPATCH_BLOCK) + N_EXPERTS
        rows = n_blocks * DISPATCH_BLOCK
        nused = (pad_end[-1] // DISPATCH_BLOCK).astype(jnp.int32)
        blk = jnp.minimum(jnp.arange(n_blocks, dtype=jnp.int32), nused - 1)
        blk_e = jnp.minimum(jnp.sum(pad_end[None, :] <= (blk * DISPATCH_BLOCK)[:, None], axis=1), N_EXPERTS - 1).astype(jnp.int32)
        dest = _dest_rows(ri, padded.astype(jnp.int32))

        tail = jnp.concatenate([jnp.where(counts > 0, pad_end - DISPATCH_BLOCK, -1), nused.reshape(1)]).astype(jnp.int32)
        xs = _dispatch(xp, dest, tail, rows)
        first = jnp.concatenate([jnp.ones((1,), jnp.int32), (blk_e[1:] != blk_e[:-1]).astype(jnp.int32)])
        eids = jnp.arange(N_EXPERTS, dtype=jnp.int32)
        later = (eids[None, :] > blk_e[:, None]) & (counts[None, :] > 0)
        next_e = jnp.min(jnp.where(later, eids[None, :], N_EXPERTS), axis=1)
        next_e = jnp.where(next_e < N_EXPERTS, next_e, -1).astype(jnp.int32)
        ys = _experts(xs, w_expert_gu[l], w_expert_down[l], blk_e, blk, nused.reshape(1), first, next_e)
        h = _combine(h2, rf, dest, ys)
    return h.reshape(batch, seq, d)
```

```python
import functools
import math

import jax
import jax.numpy as jnp
import numpy as np
from jax import lax
from jax.experimental import pallas as pl
from jax.experimental.pallas import tpu as pltpu

EPS = 1e-6
NEG = -1e30

CONV_CH = 1024
CONV_WIDTH = 31
N_Q_HEADS = 16
N_KV_HEADS = 2
HEAD_DIM = 64
ATTN_WIDTH = N_Q_HEADS * HEAD_DIM
KV_WIDTH = N_KV_HEADS * HEAD_DIM
WINDOW = 128
NUM_BUCKETS = 32
MAX_DISTANCE = 128
X_HEADS = 4
X_HEAD_DIM = 128
X_WIDTH = X_HEADS * X_HEAD_DIM
N_GROUPS = 4
EXPERTS_PER_GROUP = 8
N_EXPERTS = N_GROUPS * EXPERTS_PER_GROUP
GROUP_SHIFT = EXPERTS_PER_GROUP.bit_length() - 1
assert 1 << GROUP_SHIFT == EXPERTS_PER_GROUP
D_EXPERT = 512
DISPATCH_BLOCK = 256

LANES = 128
SUBLANES = 8
MXU_K = 256
MIX_STAGE_ROWS = 256
W_STAGE_ROWS = 64
SWA_BLOCKS_PER_STEP = 8
DMA_UNROLL = 8
CONV_HALO = 32
VMEM_LIMIT = 56 * 1024 * 1024

BF16 = jnp.bfloat16
F32 = jnp.float32


def _cparams(sem):
    return pltpu.CompilerParams(dimension_semantics=sem, vmem_limit_bytes=VMEM_LIMIT)


def _const_spec(shape):
    nd = len(shape)
    return pl.BlockSpec(shape, lambda *_: (0,) * nd, pipeline_mode=pl.Buffered(1))


def _rms_scale(x):
    return lax.rsqrt(jnp.mean(x * x, axis=-1, keepdims=True) + EPS)


def _group_sumsq(y, bd):
    sq = y * y
    hi = sq.astype(BF16)
    lo = (sq - hi.astype(F32)).astype(BF16)
    return (jnp.dot(hi, bd, preferred_element_type=F32) + jnp.dot(lo, bd, preferred_element_type=F32))


def _pack_bf16_pair(lo, hi):
    lo_bits = lax.bitcast_convert_type(lo.astype(BF16).astype(F32), jnp.uint32)
    hi_bits = lax.bitcast_convert_type(hi.astype(BF16).astype(F32), jnp.uint32)
    return (lo_bits >> 16) | hi_bits


def _unpack_bf16_pair(w):
    return (lax.bitcast_convert_type(w << 16, F32), lax.bitcast_convert_type(w & jnp.uint32(0xFFFF0000), F32))


def _load_as_bf16(pairs, stage_ref, sems):
    rows = stage_ref.shape[1]
    chunks = [(src, dst, r0) for src, dst in pairs for r0 in range(0, dst.shape[0], rows)]

    def fetch(i):
        src, _, r0 = chunks[i]
        return pltpu.make_async_copy(src.at[pl.ds(r0, rows)], stage_ref.at[i % 2], sems.at[i % 2])

    fetch(0).start()
    for i, (_, dst, r0) in enumerate(chunks):
        if i + 1 < len(chunks):
            fetch(i + 1).start()
        fetch(i).wait()
        dst[r0:r0 + rows, :] = stage_ref[i % 2].astype(BF16)


def _head_norm_store(y, gain_ref, flag_ref, bd, o_ref, head_dim):
    for c in range(y.shape[1] // LANES):
        sl = slice(c * LANES, (c + 1) * LANES)
        yc = y[:, sl]
        ss = _group_sumsq(yc, bd)
        normed = yc * lax.rsqrt(ss * (1.0 / head_dim) + EPS) * gain_ref[:, sl]
        o_ref[:, sl] = jnp.where(flag_ref[:, sl] > 0.0, normed, yc).astype(o_ref.dtype)


def _opaque_zero(dep):
    bits = lax.shift_right_logical(lax.bitcast_convert_type(dep, jnp.uint32), jnp.uint32(32))
    return lax.bitcast_convert_type(bits, F32)[0:1, :]


def _conv_chunk(c, ext_ref, sh_ref, cv_ref, dww_ref, dwb_ref, row_chunk=128, after=None):
    tm = cv_ref.shape[0]
    off = CONV_HALO - (CONV_WIDTH - 1)
    span = tm + CONV_HALO - SUBLANES
    sl = slice(c * LANES, (c + 1) * LANES)
    for r in range(1, SUBLANES):
        sh_ref[c % 2, r - 1] = ext_ref[r:r + span, sl]
    zero = None if after is None else _opaque_zero(after)
    for rc in range(tm // row_chunk):
        acc = jnp.broadcast_to(dwb_ref[:, sl] if zero is None else dwb_ref[:, sl] + zero, (row_chunk, LANES))
        for j in range(CONV_WIDTH):
            q, r = divmod(off + j, SUBLANES)
            lo = SUBLANES * q + rc * row_chunk
            win = ext_ref[lo:lo + row_chunk, sl] if r == 0 else sh_ref[c % 2, r - 1, lo:lo + row_chunk, :]
            wj = dww_ref[j:j + 1, sl] if zero is None else dww_ref[j:j + 1, sl] + zero
            acc = acc + wj * win
        cv_ref[rc * row_chunk:(rc + 1) * row_chunk, sl] = acc


def _ln_silu(cv_ref, lng_ref, lnb_ref, act_ref):
    cv = cv_ref[...]
    mu = jnp.mean(cv, axis=-1, keepdims=True)
    cen = cv - mu
    var = jnp.mean(cen * cen, axis=-1, keepdims=True)
    ln = cen * lax.rsqrt(var + EPS) * lng_ref[...] + lnb_ref[...]
    act_ref[...] = (ln * jax.nn.sigmoid(ln)).astype(act_ref.dtype)


def _in_proj_kernel(x_ref, g_ref, w_hbm, gain_ref, flag_ref, bd_ref, dww_ref, dwb_ref, lng_ref, lnb_ref,
                    act_ref, qkv_ref, gate_ref, xn_ref, ext_ref, sh_ref, cv_ref, w_ref, stage_ref, wsem,
                    *, tn, tiles_per_seq):
    tm = x_ref.shape[0]

    pl.when(pl.program_id(0) == 0)(lambda: _load_as_bf16([(w_hbm, w_ref)], stage_ref, wsem))

    x = x_ref[...]
    xn_ref[...] = (x * _rms_scale(x) * g_ref[...]).astype(BF16)

    def proj(lo, width):
        return jnp.dot(xn_ref[...], w_ref[:, lo:lo + width], preferred_element_type=F32)

    @pl.when(pl.program_id(0) % tiles_per_seq == 0)
    def _():
        ext_ref[0:CONV_HALO, :] = jnp.zeros((CONV_HALO, CONV_CH), F32)

    n_u = act_ref.shape[1]
    n_qkv = qkv_ref.shape[1]
    g0 = 2 * n_u + n_qkv
    n_conv = CONV_CH // LANES
    n_gate = gate_ref.shape[1] // tn
    assert n_u == 2 * tn

    def proj_after(lo, after):
        xn = xn_ref[...]
        if after is not None:
            zb = jnp.broadcast_to(_opaque_zero(after), (SUBLANES, LANES)).astype(BF16)
            zb = jnp.tile(zb, (tm // SUBLANES, MXU_K // LANES))
            xn = jnp.concatenate([xn[:, :MXU_K] + zb, xn[:, MXU_K:]], axis=1)
        return jnp.dot(xn, w_ref[:, lo:lo + tn], preferred_element_type=F32)

    ext_ref[CONV_HALO:, 0:tn] = proj(0, tn) * jax.nn.sigmoid(proj(n_u, tn))
    stages = [("a", tn), ("b", n_u + tn)] + [("gate", g0 + c * tn) for c in range(n_gate)]
    last_rows = (slice(tm - SUBLANES, tm), slice(tn - LANES, tn))
    mxu_done = None
    conv_done = None
    glu_a = None
    for c, (kind, lo) in enumerate(stages):
        if c < n_conv:
            _conv_chunk(c, ext_ref, sh_ref, cv_ref, dww_ref, dwb_ref, after=mxu_done)
        y = proj_after(lo, conv_done if c < n_conv else None)
        mxu_done = y[last_rows]
        if kind == "a":
            glu_a = y
        elif kind == "b":
            ext_ref[CONV_HALO:, tn:2 * tn] = glu_a * jax.nn.sigmoid(y)
        else:
            col = lo - g0
            gate_ref[:, col:col + tn] = y.astype(gate_ref.dtype)
        if c < n_conv:
            conv_done = cv_ref[tm - SUBLANES:tm, c * LANES:(c + 1) * LANES]
    ext_ref[0:CONV_HALO, :] = ext_ref[tm:tm + CONV_HALO, :]
    _ln_silu(cv_ref, lng_ref, lnb_ref, act_ref)
    _head_norm_store(proj(2 * n_u, n_qkv), gain_ref, flag_ref, bd_ref[...], qkv_ref, HEAD_DIM)


def _in_proj(x, g, w, gain, flag, dw_w, dw_b, ln_g, ln_b, seq, tm=256, tn=512):
    m, d = x.shape
    n_qkv = ATTN_WIDTH + 2 * KV_WIDTH
    n_gate = w.shape[1] - 2 * CONV_CH - n_qkv
    assert seq % tm == 0
    row = lambda i: (i, 0)
    vec = _const_spec((1, CONV_CH))
    return pl.pallas_call(
        functools.partial(_in_proj_kernel, tn=tn, tiles_per_seq=seq // tm),
        grid=(m // tm,),
        in_specs=[pl.BlockSpec((tm, d), row), _const_spec((1, d)), pl.BlockSpec(memory_space=pl.ANY), _const_spec((1, n_qkv)),
                  _const_spec((1, n_qkv)), _const_spec((LANES, LANES)), _const_spec((CONV_WIDTH, CONV_CH)), vec, vec, vec],
        out_specs=[pl.BlockSpec((tm, CONV_CH), row), pl.BlockSpec((tm, n_qkv), row), pl.BlockSpec((tm, n_gate), row)],
        out_shape=[jax.ShapeDtypeStruct((m, CONV_CH), BF16), jax.ShapeDtypeStruct((m, n_qkv), BF16),
                   jax.ShapeDtypeStruct((m, n_gate), BF16)],
        scratch_shapes=[pltpu.VMEM((tm, d), BF16), pltpu.VMEM((tm + CONV_HALO, CONV_CH), F32),
                        pltpu.VMEM((2, SUBLANES - 1, tm + CONV_HALO - SUBLANES, LANES), F32), pltpu.VMEM((tm, CONV_CH), F32),
                        pltpu.VMEM(w.shape, BF16), pltpu.VMEM((2, W_STAGE_ROWS) + w.shape[1:], F32),
                        pltpu.SemaphoreType.DMA((2,))],
        compiler_params=_cparams(("arbitrary",)),
        name="in_proj",
    )(x, g.reshape(1, d), w, gain.reshape(1, -1), flag.reshape(1, -1), _block_diag_ones(HEAD_DIM),
      dw_w, dw_b.reshape(1, -1), ln_g.reshape(1, -1), ln_b.reshape(1, -1))


def _mem_proj_kernel(x_ref, g_ref, w_ref, kgain_ref, o_ref):
    x = x_ref[...]
    xn = (x * _rms_scale(x) * g_ref[...]).astype(BF16)
    y = jnp.dot(xn, w_ref[...], preferred_element_type=F32)
    n_k = kgain_ref.shape[1]
    for c in range(y.shape[1] // LANES):
        sl = slice(c * LANES, (c + 1) * LANES)
        yc = y[:, sl]
        if c * LANES < n_k:
            yc = yc * _rms_scale(yc) * kgain_ref[:, sl]
        o_ref[:, sl] = yc.astype(o_ref.dtype)


def _mem_proj(x, g, w, kgain, tm=256):
    m, d = x.shape
    n = w.shape[1]
    assert X_HEAD_DIM == LANES
    return pl.pallas_call(
        _mem_proj_kernel,
        grid=(m // tm,),
        in_specs=[pl.BlockSpec((tm, d), lambda i: (i, 0)), _const_spec((1, d)), _const_spec(w.shape),
                  _const_spec((1, kgain.shape[0]))],
        out_specs=pl.BlockSpec((tm, n), lambda i: (i, 0)),
        out_shape=jax.ShapeDtypeStruct((m, n), BF16),
        compiler_params=_cparams(("parallel",)),
        name="mem_proj",
    )(x, g.reshape(1, d), w, kgain.reshape(1, -1))


def _block_diag_ones(group):
    r = np.arange(LANES)
    return jnp.asarray((r[:, None] // group) == (r[None, :] // group), dtype=BF16)


def _swa_kernel(q_ref, kp_ref, kc_ref, vp_ref, vc_ref, bias_ref, o_ref, *, n_chunks, qb):
    lane = lax.broadcasted_iota(jnp.int32, (qb, LANES), 1)
    lane2 = lax.broadcasted_iota(jnp.int32, (2 * qb, LANES), 1)
    row2 = lax.broadcasted_iota(jnp.int32, (2 * qb, LANES), 0)
    chunks_per_kv = n_chunks // N_KV_HEADS

    def both_halves(prev, cur):
        t = jnp.concatenate([prev, cur], axis=0).astype(F32)
        t = jnp.where(row2 == 0, 0.0, t)
        swapped = pltpu.roll(t, HEAD_DIM, 1)
        return (jnp.where(lane2 < HEAD_DIM, t, swapped).astype(BF16), jnp.where(lane2 < HEAD_DIM, swapped, t).astype(BF16))

    for sub in range(q_ref.shape[0] // qb):
        rows = slice(sub * qb, (sub + 1) * qb)
        before = slice((sub - 1) * qb, sub * qb)
        ks = both_halves(kp_ref[...] if sub == 0 else kc_ref[before, :], kc_ref[rows, :])
        vs = both_halves(vp_ref[...] if sub == 0 else vc_ref[before, :], vc_ref[rows, :])
        table = jnp.where(pl.program_id(1) == 0, 1, 0) if sub == 0 else 0
        for c in range(n_chunks):
            q2 = q_ref[rows, c * LANES:(c + 1) * LANES]
            zero = jnp.zeros_like(q2)
            qs = jnp.concatenate([jnp.where(lane < HEAD_DIM, q2, zero), jnp.where(lane >= HEAD_DIM, q2, zero)], axis=0)
            s = lax.dot_general(qs, ks[c // chunks_per_kv], (((1,), (1,)), ((), ())), preferred_element_type=F32)
            s = s + bias_ref[table, c]
            p = jnp.exp(s - jnp.max(s, axis=-1, keepdims=True))
            den = jnp.sum(p, axis=-1, keepdims=True)
            o2 = jnp.dot(p.astype(BF16), vs[c // chunks_per_kv], preferred_element_type=F32) / den
            o = jnp.where(lane < HEAD_DIM, o2[:qb], o2[qb:])
            o_ref[rows, c * LANES:(c + 1) * LANES] = o.astype(o_ref.dtype)


def _t5_bucket_np(dist):
    n = np.maximum(dist, 0)
    max_exact = NUM_BUCKETS // 2
    large = max_exact + (np.log(np.maximum(n, 1).astype(np.float32) / max_exact)
                         / math.log(MAX_DISTANCE / max_exact) * (NUM_BUCKETS - max_exact)).astype(np.int32)
    large = np.minimum(large, NUM_BUCKETS - 1)
    return np.where(n < max_exact, n, large)


def _swa(qkv, rel_bias, sinks, batch, seq):
    qb = WINDOW
    nb = seq // qb
    n_chunks = N_Q_HEADS // 2
    qi = np.arange(qb)[:, None]
    kj = np.arange(2 * qb)[None, :]
    dist = qi + qb - kj
    valid = (dist >= 0) & (dist < WINDOW)
    onehot = jnp.asarray(_t5_bucket_np(dist)[:, :, None] == np.arange(NUM_BUCKETS), dtype=F32)
    bias = jnp.einsum('qkb,bh->qkh', onehot, rel_bias.astype(F32), precision=lax.Precision.HIGHEST)
    bias = jnp.where(valid[:, :, None], bias, NEG).transpose(2, 0, 1)
    bias = bias.reshape(n_chunks, 2 * qb, 2 * qb)
    sink_rows = jnp.repeat(sinks.astype(F32).reshape(n_chunks, 2), qb, axis=1)
    bias = bias.at[:, :, 0].set(sink_rows)
    bias = jnp.stack([bias, jnp.where((kj < qb) & (kj > 0), NEG, bias)])
    kcol = ATTN_WIDTH // LANES
    vcol = kcol + 1
    sub = math.gcd(SWA_BLOCKS_PER_STEP, nb)
    steps = nb // sub

    def prev(b, n):
        return (b * nb + jnp.maximum(n * sub - 1, 0))

    return pl.pallas_call(
        functools.partial(_swa_kernel, n_chunks=n_chunks, qb=qb),
        grid=(batch, steps),
        in_specs=[
            pl.BlockSpec((sub * qb, ATTN_WIDTH), lambda b, n: (b * steps + n, 0)),
            pl.BlockSpec((qb, LANES), lambda b, n: (prev(b, n), kcol)),
            pl.BlockSpec((sub * qb, LANES), lambda b, n: (b * steps + n, kcol)),
            pl.BlockSpec((qb, LANES), lambda b, n: (prev(b, n), vcol)),
            pl.BlockSpec((sub * qb, LANES), lambda b, n: (b * steps + n, vcol)),
            _const_spec((2, n_chunks, 2 * qb, 2 * qb)),
        ],
        out_specs=pl.BlockSpec((sub * qb, ATTN_WIDTH), lambda b, n: (b * steps + n, 0)),
        out_shape=jax.ShapeDtypeStruct((batch * seq, ATTN_WIDTH), BF16),
        compiler_params=_cparams(("parallel", "parallel")),
        name="swa",
    )(qkv, qkv, qkv, qkv, qkv, bias)


def _mix_kernel(c_ref, a_ref, g0_ref, g1_ref, x_ref, wc_hbm, wa_hbm, wm_hbm, o_ref, wc_ref, wa_ref, wm_ref, stage_ref, wsem):
    pl.when(pl.program_id(0) == 0)(
        lambda: _load_as_bf16([(wc_hbm, wc_ref), (wa_hbm, wa_ref), (wm_hbm, wm_ref)], stage_ref, wsem))
    y_conv = jnp.dot(c_ref[...], wc_ref[...], preferred_element_type=F32)
    y_attn = jnp.dot(a_ref[...], wa_ref[...], preferred_element_type=F32)
    merged = jax.nn.sigmoid(g0_ref[...].astype(F32)) * y_conv + jax.nn.sigmoid(g1_ref[...].astype(F32)) * y_attn
    o_ref[...] = x_ref[...] + jnp.dot(merged.astype(BF16), wm_ref[...], preferred_element_type=F32)


def _mix(conv_act, attn_o, gates, x, w_conv_out, w_attn_out, w_mix_out, tm=256):
    n, d = x.shape
    return pl.pallas_call(
        _mix_kernel,
        grid=(n // tm,),
        in_specs=[
            pl.BlockSpec((tm, CONV_CH), lambda i: (i, 0)),
            pl.BlockSpec((tm, ATTN_WIDTH), lambda i: (i, 0)),
            pl.BlockSpec((tm, d), lambda i: (i, 0)),
            pl.BlockSpec((tm, d), lambda i: (i, 1)),
            pl.BlockSpec((tm, d), lambda i: (i, 0)),
            pl.BlockSpec(memory_space=pl.ANY),
            pl.BlockSpec(memory_space=pl.ANY),
            pl.BlockSpec(memory_space=pl.ANY),
        ],
        out_specs=pl.BlockSpec((tm, d), lambda i: (i, 0)),
        out_shape=jax.ShapeDtypeStruct((n, d), F32),
        scratch_shapes=[pltpu.VMEM((CONV_CH, d), BF16), pltpu.VMEM((ATTN_WIDTH, d), BF16), pltpu.VMEM((d, d), BF16),
                        pltpu.VMEM((2, MIX_STAGE_ROWS, d), F32), pltpu.SemaphoreType.DMA((2,))],
        compiler_params=_cparams(("arbitrary",)),
        name="mix",
    )(conv_act, attn_o, gates, gates, x, w_conv_out, w_attn_out, w_mix_out)


def _xattn_route_kernel(h_ref, gx_ref, wq_ref, qgain_ref, k_ref, v_ref, wo_ref, gm_ref,
                        wr_ref, br_ref, tri_ref,
                        h2_ref, xp_ref, ri_ref, rf_ref, cnt_ref, run_ref):
    tm, d = h_ref.shape
    half = d // 2

    @pl.when(pl.program_id(0) == 0)
    def _():
        run_ref[...] = jnp.zeros_like(run_ref)

    h = h_ref[...]
    xn = (h * _rms_scale(h) * gx_ref[...]).astype(BF16)
    q = jnp.dot(xn, wq_ref[...], preferred_element_type=F32)
    heads = []
    for hd in range(X_HEADS):
        sl = slice(hd * X_HEAD_DIM, (hd + 1) * X_HEAD_DIM)
        qh = q[:, sl]
        qh = (qh * _rms_scale(qh) * qgain_ref[:, sl]).astype(BF16)
        s = lax.dot_general(qh, k_ref[:, sl], (((1,), (1,)), ((), ())), preferred_element_type=F32)
        p = jnp.exp(s - jnp.max(s, axis=-1, keepdims=True))
        den = jnp.sum(p, axis=-1, keepdims=True)
        heads.append((jnp.dot(p.astype(BF16), v_ref[:, sl], preferred_element_type=F32) / den).astype(BF16))
    o = jnp.concatenate(heads, axis=-1)
    h2 = h + jnp.dot(o, wo_ref[...], preferred_element_type=F32)
    h2_ref[...] = h2

    xn2 = h2 * _rms_scale(h2) * gm_ref[...]
    xp_ref[...] = _pack_bf16_pair(xn2[:, :half], xn2[:, half:])

    x_hi = xn2.astype(BF16)
    x_lo = (xn2 - x_hi.astype(F32)).astype(BF16)
    r = jnp.dot(jnp.concatenate([x_hi, x_lo], axis=0), wr_ref[...], preferred_element_type=F32)
    lg = (r[:tm, :LANES] + r[:tm, LANES:]) + (r[tm:, :LANES] + r[tm:, LANES:]) + br_ref[...]

    lane = lax.broadcasted_iota(jnp.int32, (tm, LANES), 1)
    lane_f = lane.astype(F32)
    big = jnp.float32(LANES)

    def first_argmax(vals, vmax):
        idx = jnp.min(jnp.where(vals == vmax, lane_f, big), axis=-1, keepdims=True)
        return idx.astype(jnp.int32)

    glog = jnp.where(lane < N_GROUPS, lg, -jnp.inf)
    gmax = jnp.max(glog, axis=-1, keepdims=True)
    grp = first_argmax(glog, gmax)
    pg_top = 1.0 / jnp.sum(jnp.exp(glog - gmax), axis=-1, keepdims=True)
    elane = lane - N_GROUPS
    emask = (elane >= 0) & (elane < N_EXPERTS) & ((elane >> GROUP_SHIFT) == grp)
    elog = jnp.where(emask, lg, -jnp.inf)
    m1 = jnp.max(elog, axis=-1, keepdims=True)
    i1 = first_argmax(elog, m1)
    elog2 = jnp.where(lane == i1, -jnp.inf, elog)
    m2 = jnp.max(elog2, axis=-1, keepdims=True)
    i2 = first_argmax(elog2, m2)
    den = jnp.sum(jnp.exp(elog - m1), axis=-1, keepdims=True)
    p1 = 1.0 / den
    p2 = jnp.exp(m2 - m1) / den
    psum = p1 + p2
    g1 = pg_top * (p1 / psum)
    g2 = pg_top * (p2 / psum)
    e1 = i1 - N_GROUPS
    e2 = i2 - N_GROUPS

    onehot = jnp.where((elane == e1) | (elane == e2), 1.0, 0.0)
    before = jnp.dot(tri_ref[...], onehot.astype(BF16), preferred_element_type=F32) + run_ref[0:1, :]
    r1 = jnp.sum(jnp.where(elane == e1, before, 0.0), axis=-1, keepdims=True).astype(jnp.int32)
    r2 = jnp.sum(jnp.where(elane == e2, before, 0.0), axis=-1, keepdims=True).astype(jnp.int32)
    run = run_ref[...] + jnp.sum(onehot, axis=0, keepdims=True)
    run_ref[...] = run
    cnt_ref[0] = run.astype(jnp.int32)

    ri = jnp.where(lane == 0, e1, jnp.where(lane == 1, e2, jnp.where(lane == 2, r1, jnp.where(lane == 3, r2, 0))))
    ri_ref[0] = jnp.transpose(ri)[0:SUBLANES, :]
    rf_ref[...] = jnp.where(lane == 0, g1, jnp.where(lane == 1, g2, 0.0))


def _xattn_route(h1, memkv, gx, w_xq, q_gain, w_xo, gm, wr, b_r, seq, mem_len, tm=512):
    n, d = h1.shape
    tiles_per_seq = seq // tm
    nt = n // tm
    assert X_HEAD_DIM == LANES
    tri = jnp.asarray(np.tril(np.ones((tm, tm), np.float32), -1), dtype=BF16)
    return pl.pallas_call(
        _xattn_route_kernel,
        grid=(nt,),
        in_specs=[
            pl.BlockSpec((tm, d), lambda i: (i, 0)),
            _const_spec((1, d)),
            _const_spec((d, X_WIDTH)),
            _const_spec((1, X_WIDTH)),
            pl.BlockSpec((mem_len, X_WIDTH), lambda i: (i // tiles_per_seq, 0)),
            pl.BlockSpec((mem_len, X_WIDTH), lambda i: (i // tiles_per_seq, 1)),
            _const_spec((X_WIDTH, d)),
            _const_spec((1, d)),
            _const_spec((d, 2 * LANES)),
            _const_spec((1, LANES)),
            _const_spec((tm, tm)),
        ],
        out_specs=[
            pl.BlockSpec((tm, d), lambda i: (i, 0)),
            pl.BlockSpec((tm, d // 2), lambda i: (i, 0)),
            pl.BlockSpec((1, SUBLANES, tm), lambda i: (i, 0, 0)),
            pl.BlockSpec((tm, LANES), lambda i: (i, 0)),
            pl.BlockSpec((1, SUBLANES, LANES), lambda i: (i, 0, 0)),
        ],
        out_shape=[
            jax.ShapeDtypeStruct((n, d), F32),
            jax.ShapeDtypeStruct((n, d // 2), jnp.uint32),
            jax.ShapeDtypeStruct((nt, SUBLANES, tm), jnp.int32),
            jax.ShapeDtypeStruct((n, LANES), F32),
            jax.ShapeDtypeStruct((nt, SUBLANES, LANES), jnp.int32),
        ],
        scratch_shapes=[pltpu.VMEM((SUBLANES, LANES), F32)],
        compiler_params=_cparams(("arbitrary",)),
        name="xattn_route",
    )(h1, gx.reshape(1, d), w_xq, q_gain, memkv, memkv, w_xo, gm.reshape(1, d), wr, b_r, tri)


def _dest_rows_kernel(padded_ref, ri_ref, d0_ref, d1_ref):
    for k, d_ref in enumerate((d0_ref, d1_ref)):
        e = ri_ref[:, k, :]
        row = ri_ref[:, 2 + k, :]
        for j in range(N_EXPERTS - 1):
            row = row + jnp.where(e > j, padded_ref[j], 0)
        d_ref[...] = row


def _dest_rows(ri, padded):
    nt, _, tm = ri.shape
    out = jax.ShapeDtypeStruct((nt, tm), jnp.int32)
    return pl.pallas_call(
        _dest_rows_kernel,
        in_specs=[pl.BlockSpec(memory_space=pltpu.SMEM), pl.BlockSpec(memory_space=pltpu.VMEM)],
        out_specs=[pl.BlockSpec(memory_space=pltpu.VMEM)] * 2,
        out_shape=[out, out],
        name="dest_rows",
    )(padded, ri)


def _dispatch_kernel(tail_ref, d0_ref, d1_ref, x_ref, xs_ref, zero_ref, sem, zsem):
    tm = x_ref.shape[0]

    @pl.when(pl.program_id(0) == 0)
    def _():
        zero_ref[...] = jnp.zeros_like(zero_ref)

        def zero_copy(e):
            start = pl.multiple_of(tail_ref[e], DISPATCH_BLOCK)
            return pltpu.make_async_copy(zero_ref, xs_ref.at[pl.ds(start, DISPATCH_BLOCK)], zsem)

        for e in range(N_EXPERTS):
            pl.when(tail_ref[e] >= 0)(lambda e=e: zero_copy(e).start())
        for e in range(N_EXPERTS):
            pl.when(tail_ref[e] >= 0)(lambda e=e: zero_copy(e).wait())

        def unused_copy(b):
            return pltpu.make_async_copy(zero_ref, xs_ref.at[pl.ds(pl.multiple_of(b * DISPATCH_BLOCK, DISPATCH_BLOCK),
                                                                   DISPATCH_BLOCK)], zsem)

        n_blocks = xs_ref.shape[0] // DISPATCH_BLOCK
        lax.fori_loop(tail_ref[N_EXPERTS], n_blocks, lambda b, c: (unused_copy(b).start(), c)[1], 0)
        lax.fori_loop(tail_ref[N_EXPERTS], n_blocks, lambda b, c: (unused_copy(b).wait(), c)[1], 0)

    def copy(t, k):
        return pltpu.make_async_copy(x_ref.at[pl.ds(t, 1)], xs_ref.at[pl.ds((d0_ref, d1_ref)[k][0, 0, t], 1)], sem)

    def issue(tb, carry):
        for j in range(DMA_UNROLL):
            copy(tb * DMA_UNROLL + j, 0).start()
            copy(tb * DMA_UNROLL + j, 1).start()
        return carry

    lax.fori_loop(0, tm // DMA_UNROLL, issue, 0)

    def drain(tb, carry):
        for j in range(DMA_UNROLL):
            copy(tb * DMA_UNROLL + j, 0).wait()
            copy(tb * DMA_UNROLL + j, 1).wait()
        return carry

    lax.fori_loop(0, tm // DMA_UNROLL, drain, 0)


def _dispatch(xp, dest, tail, rows):
    n, w = xp.shape
    nt, tm = dest[0].shape
    grid_spec = pltpu.PrefetchScalarGridSpec(
        num_scalar_prefetch=1,
        grid=(nt,),
        in_specs=[
            pl.BlockSpec((1, 1, tm), lambda i, tail: (i, 0, 0), memory_space=pltpu.SMEM),
            pl.BlockSpec((1, 1, tm), lambda i, tail: (i, 0, 0), memory_space=pltpu.SMEM),
            pl.BlockSpec((tm, w), lambda i, tail: (i, 0)),
        ],
        out_specs=pl.BlockSpec(memory_space=pl.ANY),
        scratch_shapes=[pltpu.VMEM((DISPATCH_BLOCK, w), xp.dtype), pltpu.SemaphoreType.DMA(()),
                        pltpu.SemaphoreType.DMA(())],
    )
    return pl.pallas_call(
        _dispatch_kernel,
        grid_spec=grid_spec,
        out_shape=jax.ShapeDtypeStruct((rows, w), xp.dtype),
        compiler_params=_cparams(("arbitrary",)),
        name="dispatch",
    )(tail, dest[0].reshape(nt, 1, tm), dest[1].reshape(nt, 1, tm), xp)


def _experts_kernel(blk_e_ref, blk_x_ref, nused_ref, first_ref, next_e_ref, xs_ref, wgu_hbm, wdn_hbm, y_ref,
                    wgu_f32, wdn_f32, wgu_bf, wdn_bf, sems):
    b = pl.program_id(0)
    half = xs_ref.shape[1]

    def fetch(e):
        return (pltpu.make_async_copy(wgu_hbm.at[e], wgu_f32, sems.at[0]),
                pltpu.make_async_copy(wdn_hbm.at[e], wdn_f32, sems.at[1]))

    @pl.when(b == 0)
    def _():
        for c in fetch(blk_e_ref[0]):
            c.start()

    @pl.when(first_ref[b] == 1)
    def _():
        for c in fetch(blk_e_ref[b]):
            c.wait()
        wgu_bf[...] = wgu_f32[...].astype(BF16)
        wdn_bf[...] = wdn_f32[...].astype(BF16)

        @pl.when(next_e_ref[b] >= 0)
        def _():
            for c in fetch(next_e_ref[b]):
                c.start()

    @pl.when(b < nused_ref[0])
    def _():
        x_lo, x_hi = _unpack_bf16_pair(xs_ref[...])
        gu = (jnp.dot(x_lo.astype(BF16), wgu_bf[0:half, :], preferred_element_type=F32)
              + jnp.dot(x_hi.astype(BF16), wgu_bf[half:, :], preferred_element_type=F32))
        g = gu[:, :D_EXPERT]
        u = gu[:, D_EXPERT:]
        act = (g * jax.nn.sigmoid(g) * u).astype(BF16)
        y = jnp.dot(act, wdn_bf[...], preferred_element_type=F32)
        y_ref[...] = _pack_bf16_pair(y[:, :half], y[:, half:])

    @pl.when(b >= nused_ref[0])
    def _():
        y_ref[...] = jnp.zeros_like(y_ref)


def _experts(xs, w_gu, w_dn, blk_e, blk_x, nused, first, next_e):
    rows, half = xs.shape
    d = 2 * half
    n_blocks = rows // DISPATCH_BLOCK
    grid_spec = pltpu.PrefetchScalarGridSpec(
        num_scalar_prefetch=5,
        grid=(n_blocks,),
        in_specs=[
            pl.BlockSpec((DISPATCH_BLOCK, half), lambda b, be, bx, *_: (bx[b], 0)),
            pl.BlockSpec(memory_space=pl.ANY),
            pl.BlockSpec(memory_space=pl.ANY),
        ],
        out_specs=pl.BlockSpec((DISPATCH_BLOCK, half), lambda b, *_: (b, 0)),
        scratch_shapes=[pltpu.VMEM((d, 2 * D_EXPERT), F32), pltpu.VMEM((D_EXPERT, d), F32),
                        pltpu.VMEM((d, 2 * D_EXPERT), BF16), pltpu.VMEM((D_EXPERT, d), BF16),
                        pltpu.SemaphoreType.DMA((2,))],
    )
    return pl.pallas_call(
        _experts_kernel,
        grid_spec=grid_spec,
        out_shape=jax.ShapeDtypeStruct((rows, half), jnp.uint32),
        compiler_params=_cparams(("arbitrary",)),
        name="experts",
    )(blk_e, blk_x, nused, first, next_e, xs, w_gu, w_dn)


def _combine_kernel(cur0_ref, cur1_ref, next0_ref, next1_ref, h_ref, rf_ref, y_ref, o_ref, ya_ref, yb_ref, sems):
    i = pl.program_id(0)
    nt = pl.num_programs(0)
    tm, d = h_ref.shape
    half = d // 2
    slot = i % 2

    def issue(d0_ref, d1_ref, s):
        def body(tb, carry):
            for j in range(DMA_UNROLL):
                t = tb * DMA_UNROLL + j
                pltpu.make_async_copy(y_ref.at[pl.ds(d0_ref[0, 0, t], 1)], ya_ref.at[s, pl.ds(t, 1)], sems.at[s]).start()
                pltpu.make_async_copy(y_ref.at[pl.ds(d1_ref[0, 0, t], 1)], yb_ref.at[s, pl.ds(t, 1)], sems.at[s]).start()
            return carry

        lax.fori_loop(0, tm // DMA_UNROLL, body, 0)

    pl.when(i == 0)(lambda: issue(cur0_ref, cur1_ref, 0))
    pl.when(i + 1 < nt)(lambda: issue(next0_ref, next1_ref, 1 - slot))
    pltpu.make_async_copy(y_ref.at[pl.ds(0, tm)], ya_ref.at[slot], sems.at[slot]).wait()
    pltpu.make_async_copy(y_ref.at[pl.ds(0, tm)], yb_ref.at[slot], sems.at[slot]).wait()
    g1 = rf_ref[:, 0:1]
    g2 = rf_ref[:, 1:2]
    a_lo, a_hi = _unpack_bf16_pair(ya_ref[slot])
    b_lo, b_hi = _unpack_bf16_pair(yb_ref[slot])
    o_ref[:, :half] = h_ref[:, :half] + (a_lo * g1 + b_lo * g2)
    o_ref[:, half:] = h_ref[:, half:] + (a_hi * g1 + b_hi * g2)


def _combine(h2, rf, dest, ys, tm=256):
    n, d = h2.shape
    nt = n // tm
    half = d // 2
    cur = pl.BlockSpec((1, 1, tm), lambda i: (i, 0, 0), memory_space=pltpu.SMEM)
    nxt = pl.BlockSpec((1, 1, tm), lambda i: (jnp.minimum(i + 1, nt - 1), 0, 0), memory_space=pltpu.SMEM)
    d0 = dest[0].reshape(nt, 1, tm)
    d1 = dest[1].reshape(nt, 1, tm)
    return pl.pallas_call(
        _combine_kernel,
        grid=(nt,),
        in_specs=[
            cur, cur, nxt, nxt,
            pl.BlockSpec((tm, d), lambda i: (i, 0)),
            pl.BlockSpec((tm, LANES), lambda i: (i, 0)),
            pl.BlockSpec(memory_space=pl.ANY),
        ],
        out_specs=pl.BlockSpec((tm, d), lambda i: (i, 0)),
        out_shape=jax.ShapeDtypeStruct((n, d), F32),
        scratch_shapes=[pltpu.VMEM((2, tm, half), jnp.uint32), pltpu.VMEM((2, tm, half), jnp.uint32),
                        pltpu.SemaphoreType.DMA((2,))],
        compiler_params=_cparams(("arbitrary",)),
        name="combine",
    )(d0, d1, d0, d1, h2, rf, ys)


def kernel(x, mem, rel_bias, norm_mix_g, w_in, conv_dw_w, conv_dw_b, conv_ln_g, conv_ln_b, w_conv_out, q_norm_g, k_norm_g, attn_sinks, w_attn_out, w_mix_out, norm_x_g, norm_mem_g, w_xq, w_xkv, xq_norm_g, xk_norm_g, w_xo, norm_moe_g, w_router_group, b_router_group, w_router_expert, b_router_expert, w_expert_gu, w_expert_down):
    batch, seq, d = x.shape
    mem_len = mem.shape[1]
    n = batch * seq
    h = x.reshape(n, d)
    for l in range(norm_mix_g.shape[0]):
        q_gain = jnp.tile(q_norm_g[l].astype(F32) * HEAD_DIM ** -0.5, N_Q_HEADS)
        gain = jnp.concatenate([q_gain, jnp.tile(k_norm_g[l].astype(F32), N_KV_HEADS), jnp.ones((KV_WIDTH,), F32)])
        flag = jnp.concatenate([jnp.ones((ATTN_WIDTH + KV_WIDTH,), F32), jnp.zeros((KV_WIDTH,), F32)])
        conv_act, qkv, gates = _in_proj(h, norm_mix_g[l], w_in[l], gain, flag, conv_dw_w[l].reshape(CONV_WIDTH, CONV_CH),
                                        conv_dw_b[l], conv_ln_g[l], conv_ln_b[l], seq)
        attn_o = _swa(qkv, rel_bias, attn_sinks[l], batch, seq)
        h = _mix(conv_act, attn_o, gates, h, w_conv_out[l], w_attn_out[l], w_mix_out[l])

        memkv = _mem_proj(mem.reshape(batch * mem_len, d), norm_mem_g[l], w_xkv[l].astype(BF16),
                          jnp.tile(xk_norm_g[l].astype(F32), X_HEADS))
        xq_gain = jnp.tile(xq_norm_g[l].astype(F32) * X_HEAD_DIM ** -0.5, X_HEADS).reshape(1, -1)
        w_r = jnp.concatenate([w_router_group[l], w_router_expert[l]], axis=1).astype(F32)
        w_r = jnp.pad(w_r, ((0, 0), (0, LANES - w_r.shape[1])))
        wr_hi = w_r.astype(BF16)
        wr = jnp.concatenate([wr_hi, (w_r - wr_hi.astype(F32)).astype(BF16)], axis=1)
        b_r = jnp.concatenate([b_router_group[l], b_router_expert[l]]).astype(F32)
        b_r = jnp.pad(b_r, (0, LANES - b_r.shape[0])).reshape(1, LANES)
        h2, xp, ri, rf, cnt = _xattn_route(h, memkv, norm_x_g[l], w_xq[l].astype(BF16), xq_gain, w_xo[l].astype(BF16),
                                           norm_moe_g[l], wr, b_r, seq, mem_len)

        counts = cnt[-1, 0, N_GROUPS:N_GROUPS + N_EXPERTS]
        padded = (counts + DISPATCH_BLOCK - 1) // DISPATCH_BLOCK * DISPATCH_BLOCK
        pad_end = jnp.cumsum(padded)
        n_blocks = -(-(2 * n) // DISPATCH_BLOCK) + N_EXPERTS
        rows = n_blocks * DISPATCH_BLOCK
        nused = (pad_end[-1] // DISPATCH_BLOCK).astype(jnp.int32)
        blk = jnp.minimum(jnp.arange(n_blocks, dtype=jnp.int32), nused - 1)
        blk_e = jnp.minimum(jnp.sum(pad_end[None, :] <= (blk * DISPATCH_BLOCK)[:, None], axis=1), N_EXPERTS - 1).astype(jnp.int32)
        dest = _dest_rows(ri, padded.astype(jnp.int32))

        tail = jnp.concatenate([jnp.where(counts > 0, pad_end - DISPATCH_BLOCK, -1), nused.reshape(1)]).astype(jnp.int32)
        xs = _dispatch(xp, dest, tail, rows)
        first = jnp.concatenate([jnp.ones((1,), jnp.int32), (blk_e[1:] != blk_e[:-1]).astype(jnp.int32)])
        eids = jnp.arange(N_EXPERTS, dtype=jnp.int32)
        later = (eids[None, :] > blk_e[:, None]) & (counts[None, :] > 0)
        next_e = jnp.min(jnp.where(later, eids[None, :], N_EXPERTS), axis=1)
        next_e = jnp.where(next_e < N_EXPERTS, next_e, -1).astype(jnp.int32)
        ys = _experts(xs, w_expert_gu[l], w_expert_down[l], blk_e, blk, nused.reshape(1), first, next_e)
        h = _combine(h2, rf, dest, ys)
    return h.reshape(batch, seq, d)
```

```python
import functools
import math

import jax
import jax.numpy as jnp
import numpy as np
from jax import lax
from jax.experimental import pallas as pl
from jax.experimental.pallas import tpu as pltpu

EPS = 1e-6
NEG = -1e30

CONV_CH = 1024
CONV_WIDTH = 31
N_Q_HEADS = 16
N_KV_HEADS = 2
HEAD_DIM = 64
ATTN_WIDTH = N_Q_HEADS * HEAD_DIM
KV_WIDTH = N_KV_HEADS * HEAD_DIM
WINDOW = 128
NUM_BUCKETS = 32
MAX_DISTANCE = 128
X_HEADS = 4
X_HEAD_DIM = 128
X_WIDTH = X_HEADS * X_HEAD_DIM
N_GROUPS = 4
EXPERTS_PER_GROUP = 8
N_EXPERTS = N_GROUPS * EXPERTS_PER_GROUP
GROUP_SHIFT = EXPERTS_PER_GROUP.bit_length() - 1
assert 1 << GROUP_SHIFT == EXPERTS_PER_GROUP
D_EXPERT = 512
DISPATCH_BLOCK = 256

LANES = 128
SUBLANES = 8
MXU_K = 256
MERGE_COLS = 512
MIX_STAGE_ROWS = 256
W_STAGE_ROWS = 64
SWA_BLOCKS_PER_STEP = 8
DMA_UNROLL = 8
CONV_HALO = 32
VMEM_LIMIT = 56 * 1024 * 1024

BF16 = jnp.bfloat16
F32 = jnp.float32


def _cparams(sem):
    return pltpu.CompilerParams(dimension_semantics=sem, vmem_limit_bytes=VMEM_LIMIT)


def _const_spec(shape):
    nd = len(shape)
    return pl.BlockSpec(shape, lambda *_: (0,) * nd, pipeline_mode=pl.Buffered(1))


def _rms_scale(x):
    return lax.rsqrt(jnp.mean(x * x, axis=-1, keepdims=True) + EPS)


def _group_sumsq(y, bd):
    sq = y * y
    hi = sq.astype(BF16)
    lo = (sq - hi.astype(F32)).astype(BF16)
    return (jnp.dot(hi, bd, preferred_element_type=F32) + jnp.dot(lo, bd, preferred_element_type=F32))


def _pack_bf16_pair(lo, hi):
    lo_bits = lax.bitcast_convert_type(lo.astype(BF16).astype(F32), jnp.uint32)
    hi_bits = lax.bitcast_convert_type(hi.astype(BF16).astype(F32), jnp.uint32)
    return (lo_bits >> 16) | hi_bits


def _unpack_bf16_pair(w):
    return (lax.bitcast_convert_type(w << 16, F32), lax.bitcast_convert_type(w & jnp.uint32(0xFFFF0000), F32))


def _load_as_bf16(pairs, stage_ref, sems):
    rows = stage_ref.shape[1]
    chunks = [(src, dst, r0) for src, dst in pairs for r0 in range(0, dst.shape[0], rows)]

    def fetch(i):
        src, _, r0 = chunks[i]
        return pltpu.make_async_copy(src.at[pl.ds(r0, rows)], stage_ref.at[i % 2], sems.at[i % 2])

    fetch(0).start()
    for i, (_, dst, r0) in enumerate(chunks):
        if i + 1 < len(chunks):
            fetch(i + 1).start()
        fetch(i).wait()
        dst[r0:r0 + rows, :] = stage_ref[i % 2].astype(BF16)


def _head_norm_store(y, gain_ref, flag_ref, bd, o_ref, head_dim):
    for c in range(y.shape[1] // LANES):
        sl = slice(c * LANES, (c + 1) * LANES)
        yc = y[:, sl]
        ss = _group_sumsq(yc, bd)
        normed = yc * lax.rsqrt(ss * (1.0 / head_dim) + EPS) * gain_ref[:, sl]
        o_ref[:, sl] = jnp.where(flag_ref[:, sl] > 0.0, normed, yc).astype(o_ref.dtype)


def _opaque_zero(dep):
    bits = lax.shift_right_logical(lax.bitcast_convert_type(dep, jnp.uint32), jnp.uint32(32))
    return lax.bitcast_convert_type(bits, F32)[0:1, :]


def _conv_chunk(c, ext_ref, sh_ref, cv_ref, dww_ref, dwb_ref, row_chunk=128, after=None):
    tm = cv_ref.shape[0]
    off = CONV_HALO - (CONV_WIDTH - 1)
    span = tm + CONV_HALO - SUBLANES
    sl = slice(c * LANES, (c + 1) * LANES)
    for r in range(1, SUBLANES):
        sh_ref[c % 2, r - 1] = ext_ref[r:r + span, sl]
    zero = None if after is None else _opaque_zero(after)
    for rc in range(tm // row_chunk):
        acc = jnp.broadcast_to(dwb_ref[:, sl] if zero is None else dwb_ref[:, sl] + zero, (row_chunk, LANES))
        for j in range(CONV_WIDTH):
            q, r = divmod(off + j, SUBLANES)
            lo = SUBLANES * q + rc * row_chunk
            win = ext_ref[lo:lo + row_chunk, sl] if r == 0 else sh_ref[c % 2, r - 1, lo:lo + row_chunk, :]
            wj = dww_ref[j:j + 1, sl] if zero is None else dww_ref[j:j + 1, sl] + zero
            acc = acc + wj * win
        cv_ref[rc * row_chunk:(rc + 1) * row_chunk, sl] = acc


def _ln_silu(cv_ref, lng_ref, lnb_ref, act_ref):
    cv = cv_ref[...]
    mu = jnp.mean(cv, axis=-1, keepdims=True)
    cen = cv - mu
    var = jnp.mean(cen * cen, axis=-1, keepdims=True)
    ln = cen * lax.rsqrt(var + EPS) * lng_ref[...] + lnb_ref[...]
    act_ref[...] = (ln * jax.nn.sigmoid(ln)).astype(act_ref.dtype)


def _in_proj_kernel(x_ref, g_ref, w_hbm, gain_ref, flag_ref, bd_ref, dww_ref, dwb_ref, lng_ref, lnb_ref,
                    act_ref, qkv_ref, gate_ref, xn_ref, ext_ref, sh_ref, cv_ref, w_ref, stage_ref, wsem,
                    *, tn, tiles_per_seq):
    tm = x_ref.shape[0]

    pl.when(pl.program_id(0) == 0)(lambda: _load_as_bf16([(w_hbm, w_ref)], stage_ref, wsem))

    x = x_ref[...]
    xn_ref[...] = (x * _rms_scale(x) * g_ref[...]).astype(BF16)

    def proj(lo, width):
        return jnp.dot(xn_ref[...], w_ref[:, lo:lo + width], preferred_element_type=F32)

    @pl.when(pl.program_id(0) % tiles_per_seq == 0)
    def _():
        ext_ref[0:CONV_HALO, :] = jnp.zeros((CONV_HALO, CONV_CH), F32)

    n_u = act_ref.shape[1]
    n_qkv = qkv_ref.shape[1]
    g0 = 2 * n_u + n_qkv
    n_conv = CONV_CH // LANES
    n_gate = gate_ref.shape[1] // tn
    assert n_u == 2 * tn

    def proj_after(lo, after):
        xn = xn_ref[...]
        if after is not None:
            zb = jnp.broadcast_to(_opaque_zero(after), (SUBLANES, LANES)).astype(BF16)
            zb = jnp.tile(zb, (tm // SUBLANES, MXU_K // LANES))
            xn = jnp.concatenate([xn[:, :MXU_K] + zb, xn[:, MXU_K:]], axis=1)
        return jnp.dot(xn, w_ref[:, lo:lo + tn], preferred_element_type=F32)

    ext_ref[CONV_HALO:, 0:tn] = proj(0, tn) * jax.nn.sigmoid(proj(n_u, tn))
    stages = [("a", tn), ("b", n_u + tn)] + [("gate", g0 + c * tn) for c in range(n_gate)]
    last_rows = (slice(tm - SUBLANES, tm), slice(tn - LANES, tn))
    mxu_done = None
    conv_done = None
    glu_a = None
    for c, (kind, lo) in enumerate(stages):
        if c < n_conv:
            _conv_chunk(c, ext_ref, sh_ref, cv_ref, dww_ref, dwb_ref, after=mxu_done)
        y = proj_after(lo, conv_done if c < n_conv else None)
        mxu_done = y[last_rows]
        if kind == "a":
            glu_a = y
        elif kind == "b":
            ext_ref[CONV_HALO:, tn:2 * tn] = glu_a * jax.nn.sigmoid(y)
        else:
            col = lo - g0
            gate_ref[:, col:col + tn] = y.astype(gate_ref.dtype)
        if c < n_conv:
            conv_done = cv_ref[tm - SUBLANES:tm, c * LANES:(c + 1) * LANES]
    ext_ref[0:CONV_HALO, :] = ext_ref[tm:tm + CONV_HALO, :]
    _ln_silu(cv_ref, lng_ref, lnb_ref, act_ref)
    _head_norm_store(proj(2 * n_u, n_qkv), gain_ref, flag_ref, bd_ref[...], qkv_ref, HEAD_DIM)


def _in_proj(x, g, w, gain, flag, dw_w, dw_b, ln_g, ln_b, seq, tm=256, tn=512):
    m, d = x.shape
    n_qkv = ATTN_WIDTH + 2 * KV_WIDTH
    n_gate = w.shape[1] - 2 * CONV_CH - n_qkv
    assert seq % tm == 0
    row = lambda i: (i, 0)
    vec = _const_spec((1, CONV_CH))
    return pl.pallas_call(
        functools.partial(_in_proj_kernel, tn=tn, tiles_per_seq=seq // tm),
        grid=(m // tm,),
        in_specs=[pl.BlockSpec((tm, d), row), _const_spec((1, d)), pl.BlockSpec(memory_space=pl.ANY), _const_spec((1, n_qkv)),
                  _const_spec((1, n_qkv)), _const_spec((LANES, LANES)), _const_spec((CONV_WIDTH, CONV_CH)), vec, vec, vec],
        out_specs=[pl.BlockSpec((tm, CONV_CH), row), pl.BlockSpec((tm, n_qkv), row), pl.BlockSpec((tm, n_gate), row)],
        out_shape=[jax.ShapeDtypeStruct((m, CONV_CH), BF16), jax.ShapeDtypeStruct((m, n_qkv), BF16),
                   jax.ShapeDtypeStruct((m, n_gate), BF16)],
        scratch_shapes=[pltpu.VMEM((tm, d), BF16), pltpu.VMEM((tm + CONV_HALO, CONV_CH), F32),
                        pltpu.VMEM((2, SUBLANES - 1, tm + CONV_HALO - SUBLANES, LANES), F32), pltpu.VMEM((tm, CONV_CH), F32),
                        pltpu.VMEM(w.shape, BF16), pltpu.VMEM((2, W_STAGE_ROWS) + w.shape[1:], F32),
                        pltpu.SemaphoreType.DMA((2,))],
        compiler_params=_cparams(("arbitrary",)),
        name="in_proj",
    )(x, g.reshape(1, d), w, gain.reshape(1, -1), flag.reshape(1, -1), _block_diag_ones(HEAD_DIM),
      dw_w, dw_b.reshape(1, -1), ln_g.reshape(1, -1), ln_b.reshape(1, -1))


def _mem_proj_kernel(x_ref, g_ref, w_ref, kgain_ref, o_ref):
    x = x_ref[...]
    xn = (x * _rms_scale(x) * g_ref[...]).astype(BF16)
    y = jnp.dot(xn, w_ref[...], preferred_element_type=F32)
    n_k = kgain_ref.shape[1]
    for c in range(y.shape[1] // LANES):
        sl = slice(c * LANES, (c + 1) * LANES)
        yc = y[:, sl]
        if c * LANES < n_k:
            yc = yc * _rms_scale(yc) * kgain_ref[:, sl]
        o_ref[:, sl] = yc.astype(o_ref.dtype)


def _mem_proj(x, g, w, kgain, tm=256):
    m, d = x.shape
    n = w.shape[1]
    assert X_HEAD_DIM == LANES
    return pl.pallas_call(
        _mem_proj_kernel,
        grid=(m // tm,),
        in_specs=[pl.BlockSpec((tm, d), lambda i: (i, 0)), _const_spec((1, d)), _const_spec(w.shape),
                  _const_spec((1, kgain.shape[0]))],
        out_specs=pl.BlockSpec((tm, n), lambda i: (i, 0)),
        out_shape=jax.ShapeDtypeStruct((m, n), BF16),
        compiler_params=_cparams(("parallel",)),
        name="mem_proj",
    )(x, g.reshape(1, d), w, kgain.reshape(1, -1))


def _block_diag_ones(group):
    r = np.arange(LANES)
    return jnp.asarray((r[:, None] // group) == (r[None, :] // group), dtype=BF16)


def _swa_kernel(q_ref, kp_ref, kc_ref, vp_ref, vc_ref, bias_ref, o_ref, *, n_chunks, qb):
    lane = lax.broadcasted_iota(jnp.int32, (qb, LANES), 1)
    lane2 = lax.broadcasted_iota(jnp.int32, (2 * qb, LANES), 1)
    row2 = lax.broadcasted_iota(jnp.int32, (2 * qb, LANES), 0)
    chunks_per_kv = n_chunks // N_KV_HEADS

    def both_halves(prev, cur):
        t = jnp.concatenate([prev, cur], axis=0).astype(F32)
        t = jnp.where(row2 == 0, 0.0, t)
        swapped = pltpu.roll(t, HEAD_DIM, 1)
        return (jnp.where(lane2 < HEAD_DIM, t, swapped).astype(BF16), jnp.where(lane2 < HEAD_DIM, swapped, t).astype(BF16))

    for sub in range(q_ref.shape[0] // qb):
        rows = slice(sub * qb, (sub + 1) * qb)
        before = slice((sub - 1) * qb, sub * qb)
        ks = both_halves(kp_ref[...] if sub == 0 else kc_ref[before, :], kc_ref[rows, :])
        vs = both_halves(vp_ref[...] if sub == 0 else vc_ref[before, :], vc_ref[rows, :])
        table = jnp.where(pl.program_id(1) == 0, 1, 0) if sub == 0 else 0
        for c in range(n_chunks):
            q2 = q_ref[rows, c * LANES:(c + 1) * LANES]
            zero = jnp.zeros_like(q2)
            qs = jnp.concatenate([jnp.where(lane < HEAD_DIM, q2, zero), jnp.where(lane >= HEAD_DIM, q2, zero)], axis=0)
            s = lax.dot_general(qs, ks[c // chunks_per_kv], (((1,), (1,)), ((), ())), preferred_element_type=F32)
            s = s + bias_ref[table, c]
            p = jnp.exp(s - jnp.max(s, axis=-1, keepdims=True))
            den = jnp.sum(p, axis=-1, keepdims=True)
            o2 = jnp.dot(p.astype(BF16), vs[c // chunks_per_kv], preferred_element_type=F32) / den
            o = jnp.where(lane < HEAD_DIM, o2[:qb], o2[qb:])
            o_ref[rows, c * LANES:(c + 1) * LANES] = o.astype(o_ref.dtype)


def _t5_bucket_np(dist):
    n = np.maximum(dist, 0)
    max_exact = NUM_BUCKETS // 2
    large = max_exact + (np.log(np.maximum(n, 1).astype(np.float32) / max_exact)
                         / math.log(MAX_DISTANCE / max_exact) * (NUM_BUCKETS - max_exact)).astype(np.int32)
    large = np.minimum(large, NUM_BUCKETS - 1)
    return np.where(n < max_exact, n, large)


def _swa(qkv, rel_bias, sinks, batch, seq):
    qb = WINDOW
    nb = seq // qb
    n_chunks = N_Q_HEADS // 2
    qi = np.arange(qb)[:, None]
    kj = np.arange(2 * qb)[None, :]
    dist = qi + qb - kj
    valid = (dist >= 0) & (dist < WINDOW)
    onehot = jnp.asarray(_t5_bucket_np(dist)[:, :, None] == np.arange(NUM_BUCKETS), dtype=F32)
    bias = jnp.einsum('qkb,bh->qkh', onehot, rel_bias.astype(F32), precision=lax.Precision.HIGHEST)
    bias = jnp.where(valid[:, :, None], bias, NEG).transpose(2, 0, 1)
    bias = bias.reshape(n_chunks, 2 * qb, 2 * qb)
    sink_rows = jnp.repeat(sinks.astype(F32).reshape(n_chunks, 2), qb, axis=1)
    bias = bias.at[:, :, 0].set(sink_rows)
    bias = jnp.stack([bias, jnp.where((kj < qb) & (kj > 0), NEG, bias)])
    kcol = ATTN_WIDTH // LANES
    vcol = kcol + 1
    sub = math.gcd(SWA_BLOCKS_PER_STEP, nb)
    steps = nb // sub

    def prev(b, n):
        return (b * nb + jnp.maximum(n * sub - 1, 0))

    return pl.pallas_call(
        functools.partial(_swa_kernel, n_chunks=n_chunks, qb=qb),
        grid=(batch, steps),
        in_specs=[
            pl.BlockSpec((sub * qb, ATTN_WIDTH), lambda b, n: (b * steps + n, 0)),
            pl.BlockSpec((qb, LANES), lambda b, n: (prev(b, n), kcol)),
            pl.BlockSpec((sub * qb, LANES), lambda b, n: (b * steps + n, kcol)),
            pl.BlockSpec((qb, LANES), lambda b, n: (prev(b, n), vcol)),
            pl.BlockSpec((sub * qb, LANES), lambda b, n: (b * steps + n, vcol)),
            _const_spec((2, n_chunks, 2 * qb, 2 * qb)),
        ],
        out_specs=pl.BlockSpec((sub * qb, ATTN_WIDTH), lambda b, n: (b * steps + n, 0)),
        out_shape=jax.ShapeDtypeStruct((batch * seq, ATTN_WIDTH), BF16),
        compiler_params=_cparams(("parallel", "parallel")),
        name="swa",
    )(qkv, qkv, qkv, qkv, qkv, bias)


def _mix_kernel(c_ref, a_ref, g0_ref, g1_ref, x_ref, wc_hbm, wa_hbm, wm_hbm, o_ref, wc_ref, wa_ref, wm_ref, stage_ref, wsem, m_ref):
    pl.when(pl.program_id(0) == 0)(
        lambda: _load_as_bf16([(wc_hbm, wc_ref), (wa_hbm, wa_ref), (wm_hbm, wm_ref)], stage_ref, wsem))
    d = x_ref.shape[1]
    for lo in range(0, d, MERGE_COLS):
        sl = slice(lo, lo + MERGE_COLS)
        y_conv = jnp.dot(c_ref[...], wc_ref[:, sl], preferred_element_type=F32)
        y_attn = jnp.dot(a_ref[...], wa_ref[:, sl], preferred_element_type=F32)
        merged = jax.nn.sigmoid(g0_ref[:, sl].astype(F32)) * y_conv + jax.nn.sigmoid(g1_ref[:, sl].astype(F32)) * y_attn
        m_ref[:, sl] = merged.astype(BF16)
    o_ref[...] = x_ref[...] + jnp.dot(m_ref[...], wm_ref[...], preferred_element_type=F32)


def _mix(conv_act, attn_o, gates, x, w_conv_out, w_attn_out, w_mix_out, tm=256):
    n, d = x.shape
    return pl.pallas_call(
        _mix_kernel,
        grid=(n // tm,),
        in_specs=[
            pl.BlockSpec((tm, CONV_CH), lambda i: (i, 0)),
            pl.BlockSpec((tm, ATTN_WIDTH), lambda i: (i, 0)),
            pl.BlockSpec((tm, d), lambda i: (i, 0)),
            pl.BlockSpec((tm, d), lambda i: (i, 1)),
            pl.BlockSpec((tm, d), lambda i: (i, 0)),
            pl.BlockSpec(memory_space=pl.ANY),
            pl.BlockSpec(memory_space=pl.ANY),
            pl.BlockSpec(memory_space=pl.ANY),
        ],
        out_specs=pl.BlockSpec((tm, d), lambda i: (i, 0)),
        out_shape=jax.ShapeDtypeStruct((n, d), F32),
        scratch_shapes=[pltpu.VMEM((CONV_CH, d), BF16), pltpu.VMEM((ATTN_WIDTH, d), BF16), pltpu.VMEM((d, d), BF16),
                        pltpu.VMEM((2, MIX_STAGE_ROWS, d), F32), pltpu.SemaphoreType.DMA((2,)), pltpu.VMEM((tm, d), BF16)],
        compiler_params=_cparams(("arbitrary",)),
        name="mix",
    )(conv_act, attn_o, gates, gates, x, w_conv_out, w_attn_out, w_mix_out)


def _xattn_route_kernel(h_ref, gx_ref, wq_ref, qgain_ref, k_ref, v_ref, wo_ref, gm_ref,
                        wr_ref, br_ref, tri_ref,
                        h2_ref, xp_ref, ri_ref, rf_ref, cnt_ref, run_ref):
    tm, d = h_ref.shape
    half = d // 2

    @pl.when(pl.program_id(0) == 0)
    def _():
        run_ref[...] = jnp.zeros_like(run_ref)

    h = h_ref[...]
    xn = (h * _rms_scale(h) * gx_ref[...]).astype(BF16)
    q = jnp.dot(xn, wq_ref[...], preferred_element_type=F32)
    heads = []
    for hd in range(X_HEADS):
        sl = slice(hd * X_HEAD_DIM, (hd + 1) * X_HEAD_DIM)
        qh = q[:, sl]
        qh = (qh * _rms_scale(qh) * qgain_ref[:, sl]).astype(BF16)
        s = lax.dot_general(qh, k_ref[:, sl], (((1,), (1,)), ((), ())), preferred_element_type=F32)
        p = jnp.exp(s - jnp.max(s, axis=-1, keepdims=True))
        den = jnp.sum(p, axis=-1, keepdims=True)
        heads.append((jnp.dot(p.astype(BF16), v_ref[:, sl], preferred_element_type=F32) / den).astype(BF16))
    o = jnp.concatenate(heads, axis=-1)
    h2 = h + jnp.dot(o, wo_ref[...], preferred_element_type=F32)
    h2_ref[...] = h2

    xn2 = h2 * _rms_scale(h2) * gm_ref[...]
    xp_ref[...] = _pack_bf16_pair(xn2[:, :half], xn2[:, half:])

    x_hi = xn2.astype(BF16)
    x_lo = (xn2 - x_hi.astype(F32)).astype(BF16)
    r = jnp.dot(jnp.concatenate([x_hi, x_lo], axis=0), wr_ref[...], preferred_element_type=F32)
    lg = (r[:tm, :LANES] + r[:tm, LANES:]) + (r[tm:, :LANES] + r[tm:, LANES:]) + br_ref[...]

    lane = lax.broadcasted_iota(jnp.int32, (tm, LANES), 1)
    lane_f = lane.astype(F32)
    big = jnp.float32(LANES)

    def first_argmax(vals, vmax):
        idx = jnp.min(jnp.where(vals == vmax, lane_f, big), axis=-1, keepdims=True)
        return idx.astype(jnp.int32)

    glog = jnp.where(lane < N_GROUPS, lg, -jnp.inf)
    gmax = jnp.max(glog, axis=-1, keepdims=True)
    grp = first_argmax(glog, gmax)
    pg_top = 1.0 / jnp.sum(jnp.exp(glog - gmax), axis=-1, keepdims=True)
    elane = lane - N_GROUPS
    emask = (elane >= 0) & (elane < N_EXPERTS) & ((elane >> GROUP_SHIFT) == grp)
    elog = jnp.where(emask, lg, -jnp.inf)
    m1 = jnp.max(elog, axis=-1, keepdims=True)
    i1 = first_argmax(elog, m1)
    elog2 = jnp.where(lane == i1, -jnp.inf, elog)
    m2 = jnp.max(elog2, axis=-1, keepdims=True)
    i2 = first_argmax(elog2, m2)
    den = jnp.sum(jnp.exp(elog - m1), axis=-1, keepdims=True)
    p1 = 1.0 / den
    p2 = jnp.exp(m2 - m1) / den
    psum = p1 + p2
    g1 = pg_top * (p1 / psum)
    g2 = pg_top * (p2 / psum)
    e1 = i1 - N_GROUPS
    e2 = i2 - N_GROUPS

    onehot = jnp.where((elane == e1) | (elane == e2), 1.0, 0.0)
    before = jnp.dot(tri_ref[...], onehot.astype(BF16), preferred_element_type=F32) + run_ref[0:1, :]
    r1 = jnp.sum(jnp.where(elane == e1, before, 0.0), axis=-1, keepdims=True).astype(jnp.int32)
    r2 = jnp.sum(jnp.where(elane == e2, before, 0.0), axis=-1, keepdims=True).astype(jnp.int32)
    run = run_ref[...] + jnp.sum(onehot, axis=0, keepdims=True)
    run_ref[...] = run
    cnt_ref[0] = run.astype(jnp.int32)

    ri = jnp.where(lane == 0, e1, jnp.where(lane == 1, e2, jnp.where(lane == 2, r1, jnp.where(lane == 3, r2, 0))))
    ri_ref[0] = jnp.transpose(ri)[0:SUBLANES, :]
    rf_ref[...] = jnp.where(lane == 0, g1, jnp.where(lane == 1, g2, 0.0))


def _xattn_route(h1, memkv, gx, w_xq, q_gain, w_xo, gm, wr, b_r, seq, mem_len, tm=512):
    n, d = h1.shape
    tiles_per_seq = seq // tm
    nt = n // tm
    assert X_HEAD_DIM == LANES
    tri = jnp.asarray(np.tril(np.ones((tm, tm), np.float32), -1), dtype=BF16)
    return pl.pallas_call(
        _xattn_route_kernel,
        grid=(nt,),
        in_specs=[
            pl.BlockSpec((tm, d), lambda i: (i, 0)),
            _const_spec((1, d)),
            _const_spec((d, X_WIDTH)),
            _const_spec((1, X_WIDTH)),
            pl.BlockSpec((mem_len, X_WIDTH), lambda i: (i // tiles_per_seq, 0)),
            pl.BlockSpec((mem_len, X_WIDTH), lambda i: (i // tiles_per_seq, 1)),
            _const_spec((X_WIDTH, d)),
            _const_spec((1, d)),
            _const_spec((d, 2 * LANES)),
            _const_spec((1, LANES)),
            _const_spec((tm, tm)),
        ],
        out_specs=[
            pl.BlockSpec((tm, d), lambda i: (i, 0)),
            pl.BlockSpec((tm, d // 2), lambda i: (i, 0)),
            pl.BlockSpec((1, SUBLANES, tm), lambda i: (i, 0, 0)),
            pl.BlockSpec((tm, LANES), lambda i: (i, 0)),
            pl.BlockSpec((1, SUBLANES, LANES), lambda i: (i, 0, 0)),
        ],
        out_shape=[
            jax.ShapeDtypeStruct((n, d), F32),
            jax.ShapeDtypeStruct((n, d // 2), jnp.uint32),
            jax.ShapeDtypeStruct((nt, SUBLANES, tm), jnp.int32),
            jax.ShapeDtypeStruct((n, LANES), F32),
            jax.ShapeDtypeStruct((nt, SUBLANES, LANES), jnp.int32),
        ],
        scratch_shapes=[pltpu.VMEM((SUBLANES, LANES), F32)],
        compiler_params=_cparams(("arbitrary",)),
        name="xattn_route",
    )(h1, gx.reshape(1, d), w_xq, q_gain, memkv, memkv, w_xo, gm.reshape(1, d), wr, b_r, tri)


def _dest_rows_kernel(padded_ref, ri_ref, d0_ref, d1_ref):
    for k, d_ref in enumerate((d0_ref, d1_ref)):
        e = ri_ref[:, k, :]
        row = ri_ref[:, 2 + k, :]
        for j in range(N_EXPERTS - 1):
            row = row + jnp.where(e > j, padded_ref[j], 0)
        d_ref[...] = row


def _dest_rows(ri, padded):
    nt, _, tm = ri.shape
    out = jax.ShapeDtypeStruct((nt, tm), jnp.int32)
    return pl.pallas_call(
        _dest_rows_kernel,
        in_specs=[pl.BlockSpec(memory_space=pltpu.SMEM), pl.BlockSpec(memory_space=pltpu.VMEM)],
        out_specs=[pl.BlockSpec(memory_space=pltpu.VMEM)] * 2,
        out_shape=[out, out],
        name="dest_rows",
    )(padded, ri)


def _dispatch_kernel(tail_ref, d0_ref, d1_ref, x_ref, xs_ref, zero_ref, sem, zsem):
    tm = x_ref.shape[0]

    @pl.when(pl.program_id(0) == 0)
    def _():
        zero_ref[...] = jnp.zeros_like(zero_ref)

        def zero_copy(e):
            start = pl.multiple_of(tail_ref[e], DISPATCH_BLOCK)
            return pltpu.make_async_copy(zero_ref, xs_ref.at[pl.ds(start, DISPATCH_BLOCK)], zsem)

        for e in range(N_EXPERTS):
            pl.when(tail_ref[e] >= 0)(lambda e=e: zero_copy(e).start())
        for e in range(N_EXPERTS):
            pl.when(tail_ref[e] >= 0)(lambda e=e: zero_copy(e).wait())

        def unused_copy(b):
            return pltpu.make_async_copy(zero_ref, xs_ref.at[pl.ds(pl.multiple_of(b * DISPATCH_BLOCK, DISPATCH_BLOCK),
                                                                   DISPATCH_BLOCK)], zsem)

        n_blocks = xs_ref.shape[0] // DISPATCH_BLOCK
        lax.fori_loop(tail_ref[N_EXPERTS], n_blocks, lambda b, c: (unused_copy(b).start(), c)[1], 0)
        lax.fori_loop(tail_ref[N_EXPERTS], n_blocks, lambda b, c: (unused_copy(b).wait(), c)[1], 0)

    def copy(t, k):
        return pltpu.make_async_copy(x_ref.at[pl.ds(t, 1)], xs_ref.at[pl.ds((d0_ref, d1_ref)[k][0, 0, t], 1)], sem)

    def issue(tb, carry):
        for j in range(DMA_UNROLL):
            copy(tb * DMA_UNROLL + j, 0).start()
            copy(tb * DMA_UNROLL + j, 1).start()
        return carry

    lax.fori_loop(0, tm // DMA_UNROLL, issue, 0)

    def drain(tb, carry):
        for j in range(DMA_UNROLL):
            copy(tb * DMA_UNROLL + j, 0).wait()
            copy(tb * DMA_UNROLL + j, 1).wait()
        return carry

    lax.fori_loop(0, tm // DMA_UNROLL, drain, 0)


def _dispatch(xp, dest, tail, rows):
    n, w = xp.shape
    nt, tm = dest[0].shape
    grid_spec = pltpu.PrefetchScalarGridSpec(
        num_scalar_prefetch=1,
        grid=(nt,),
        in_specs=[
            pl.BlockSpec((1, 1, tm), lambda i, tail: (i, 0, 0), memory_space=pltpu.SMEM),
            pl.BlockSpec((1, 1, tm), lambda i, tail: (i, 0, 0), memory_space=pltpu.SMEM),
            pl.BlockSpec((tm, w), lambda i, tail: (i, 0)),
        ],
        out_specs=pl.BlockSpec(memory_space=pl.ANY),
        scratch_shapes=[pltpu.VMEM((DISPATCH_BLOCK, w), xp.dtype), pltpu.SemaphoreType.DMA(()),
                        pltpu.SemaphoreType.DMA(())],
    )
    return pl.pallas_call(
        _dispatch_kernel,
        grid_spec=grid_spec,
        out_shape=jax.ShapeDtypeStruct((rows, w), xp.dtype),
        compiler_params=_cparams(("arbitrary",)),
        name="dispatch",
    )(tail, dest[0].reshape(nt, 1, tm), dest[1].reshape(nt, 1, tm), xp)


def _experts_kernel(blk_e_ref, blk_x_ref, nused_ref, first_ref, next_e_ref, xs_ref, wgu_hbm, wdn_hbm, y_ref,
                    wgu_f32, wdn_f32, wgu_bf, wdn_bf, sems):
    b = pl.program_id(0)
    half = xs_ref.shape[1]

    def fetch(e):
        return (pltpu.make_async_copy(wgu_hbm.at[e], wgu_f32, sems.at[0]),
                pltpu.make_async_copy(wdn_hbm.at[e], wdn_f32, sems.at[1]))

    @pl.when(b == 0)
    def _():
        for c in fetch(blk_e_ref[0]):
            c.start()

    @pl.when(first_ref[b] == 1)
    def _():
        for c in fetch(blk_e_ref[b]):
            c.wait()
        wgu_bf[...] = wgu_f32[...].astype(BF16)
        wdn_bf[...] = wdn_f32[...].astype(BF16)

        @pl.when(next_e_ref[b] >= 0)
        def _():
            for c in fetch(next_e_ref[b]):
                c.start()

    @pl.when(b < nused_ref[0])
    def _():
        x_lo, x_hi = _unpack_bf16_pair(xs_ref[...])
        gu = (jnp.dot(x_lo.astype(BF16), wgu_bf[0:half, :], preferred_element_type=F32)
              + jnp.dot(x_hi.astype(BF16), wgu_bf[half:, :], preferred_element_type=F32))
        g = gu[:, :D_EXPERT]
        u = gu[:, D_EXPERT:]
        act = (g * jax.nn.sigmoid(g) * u).astype(BF16)
        y = jnp.dot(act, wdn_bf[...], preferred_element_type=F32)
        y_ref[...] = _pack_bf16_pair(y[:, :half], y[:, half:])

    @pl.when(b >= nused_ref[0])
    def _():
        y_ref[...] = jnp.zeros_like(y_ref)


def _experts(xs, w_gu, w_dn, blk_e, blk_x, nused, first, next_e):
    rows, half = xs.shape
    d = 2 * half
    n_blocks = rows // DISPATCH_BLOCK
    grid_spec = pltpu.PrefetchScalarGridSpec(
        num_scalar_prefetch=5,
        grid=(n_blocks,),
        in_specs=[
            pl.BlockSpec((DISPATCH_BLOCK, half), lambda b, be, bx, *_: (bx[b], 0)),
            pl.BlockSpec(memory_space=pl.ANY),
            pl.BlockSpec(memory_space=pl.ANY),
        ],
        out_specs=pl.BlockSpec((DISPATCH_BLOCK, half), lambda b, *_: (b, 0)),
        scratch_shapes=[pltpu.VMEM((d, 2 * D_EXPERT), F32), pltpu.VMEM((D_EXPERT, d), F32),
                        pltpu.VMEM((d, 2 * D_EXPERT), BF16), pltpu.VMEM((D_EXPERT, d), BF16),
                        pltpu.SemaphoreType.DMA((2,))],
    )
    return pl.pallas_call(
        _experts_kernel,
        grid_spec=grid_spec,
        out_shape=jax.ShapeDtypeStruct((rows, half), jnp.uint32),
        compiler_params=_cparams(("arbitrary",)),
        name="experts",
    )(blk_e, blk_x, nused, first, next_e, xs, w_gu, w_dn)


def _combine_kernel(cur0_ref, cur1_ref, next0_ref, next1_ref, h_ref, rf_ref, y_ref, o_ref, ya_ref, yb_ref, sems):
    i = pl.program_id(0)
    nt = pl.num_programs(0)
    tm, d = h_ref.shape
    half = d // 2
    slot = i % 2

    def issue(d0_ref, d1_ref, s):
        def body(tb, carry):
            for j in range(DMA_UNROLL):
                t = tb * DMA_UNROLL + j
                pltpu.make_async_copy(y_ref.at[pl.ds(d0_ref[0, 0, t], 1)], ya_ref.at[s, pl.ds(t, 1)], sems.at[s]).start()
                pltpu.make_async_copy(y_ref.at[pl.ds(d1_ref[0, 0, t], 1)], yb_ref.at[s, pl.ds(t, 1)], sems.at[s]).start()
            return carry

        lax.fori_loop(0, tm // DMA_UNROLL, body, 0)

    pl.when(i == 0)(lambda: issue(cur0_ref, cur1_ref, 0))
    pl.when(i + 1 < nt)(lambda: issue(next0_ref, next1_ref, 1 - slot))
    pltpu.make_async_copy(y_ref.at[pl.ds(0, tm)], ya_ref.at[slot], sems.at[slot]).wait()
    pltpu.make_async_copy(y_ref.at[pl.ds(0, tm)], yb_ref.at[slot], sems.at[slot]).wait()
    g1 = rf_ref[:, 0:1]
    g2 = rf_ref[:, 1:2]
    a_lo, a_hi = _unpack_bf16_pair(ya_ref[slot])
    b_lo, b_hi = _unpack_bf16_pair(yb_ref[slot])
    o_ref[:, :half] = h_ref[:, :half] + (a_lo * g1 + b_lo * g2)
    o_ref[:, half:] = h_ref[:, half:] + (a_hi * g1 + b_hi * g2)


def _combine(h2, rf, dest, ys, tm=256):
    n, d = h2.shape
    nt = n // tm
    half = d // 2
    cur = pl.BlockSpec((1, 1, tm), lambda i: (i, 0, 0), memory_space=pltpu.SMEM)
    nxt = pl.BlockSpec((1, 1, tm), lambda i: (jnp.minimum(i + 1, nt - 1), 0, 0), memory_space=pltpu.SMEM)
    d0 = dest[0].reshape(nt, 1, tm)
    d1 = dest[1].reshape(nt, 1, tm)
    return pl.pallas_call(
        _combine_kernel,
        grid=(nt,),
        in_specs=[
            cur, cur, nxt, nxt,
            pl.BlockSpec((tm, d), lambda i: (i, 0)),
            pl.BlockSpec((tm, LANES), lambda i: (i, 0)),
            pl.BlockSpec(memory_space=pl.ANY),
        ],
        out_specs=pl.BlockSpec((tm, d), lambda i: (i, 0)),
        out_shape=jax.ShapeDtypeStruct((n, d), F32),
        scratch_shapes=[pltpu.VMEM((2, tm, half), jnp.uint32), pltpu.VMEM((2, tm, half), jnp.uint32),
                        pltpu.SemaphoreType.DMA((2,))],
        compiler_params=_cparams(("arbitrary",)),
        name="combine",
    )(d0, d1, d0, d1, h2, rf, ys)


def kernel(x, mem, rel_bias, norm_mix_g, w_in, conv_dw_w, conv_dw_b, conv_ln_g, conv_ln_b, w_conv_out, q_norm_g, k_norm_g, attn_sinks, w_attn_out, w_mix_out, norm_x_g, norm_mem_g, w_xq, w_xkv, xq_norm_g, xk_norm_g, w_xo, norm_moe_g, w_router_group, b_router_group, w_router_expert, b_router_expert, w_expert_gu, w_expert_down):
    batch, seq, d = x.shape
    mem_len = mem.shape[1]
    n = batch * seq
    h = x.reshape(n, d)
    for l in range(norm_mix_g.shape[0]):
        q_gain = jnp.tile(q_norm_g[l].astype(F32) * HEAD_DIM ** -0.5, N_Q_HEADS)
        gain = jnp.concatenate([q_gain, jnp.tile(k_norm_g[l].astype(F32), N_KV_HEADS), jnp.ones((KV_WIDTH,), F32)])
        flag = jnp.concatenate([jnp.ones((ATTN_WIDTH + KV_WIDTH,), F32), jnp.zeros((KV_WIDTH,), F32)])
        conv_act, qkv, gates = _in_proj(h, norm_mix_g[l], w_in[l], gain, flag, conv_dw_w[l].reshape(CONV_WIDTH, CONV_CH),
                                        conv_dw_b[l], conv_ln_g[l], conv_ln_b[l], seq)
        attn_o = _swa(qkv, rel_bias, attn_sinks[l], batch, seq)
        h = _mix(conv_act, attn_o, gates, h, w_conv_out[l], w_attn_out[l], w_mix_out[l])

        memkv = _mem_proj(mem.reshape(batch * mem_len, d), norm_mem_g[l], w_xkv[l].astype(BF16),
                          jnp.tile(xk_norm_g[l].astype(F32), X_HEADS))
        xq_gain = jnp.tile(xq_norm_g[l].astype(F32) * X_HEAD_DIM ** -0.5, X_HEADS).reshape(1, -1)
        w_r = jnp.concatenate([w_router_group[l], w_router_expert[l]], axis=1).astype(F32)
        w_r = jnp.pad(w_r, ((0, 0), (0, LANES - w_r.shape[1])))
        wr_hi = w_r.astype(BF16)
        wr = jnp.concatenate([wr_hi, (w_r - wr_hi.astype(F32)).astype(BF16)], axis=1)
        b_r = jnp.concatenate([b_router_group[l], b_router_expert[l]]).astype(F32)
        b_r = jnp.pad(b_r, (0, LANES - b_r.shape[0])).reshape(1, LANES)
        h2, xp, ri, rf, cnt = _xattn_route(h, memkv, norm_x_g[l], w_xq[l].astype(BF16), xq_gain, w_xo[l].astype(BF16),
                                           norm_moe_g[l], wr, b_r, seq, mem_len)

        counts = cnt[-1, 0, N_GROUPS:N_GROUPS + N_EXPERTS]
        padded = (counts + DISPATCH_BLOCK - 1) // DISPATCH_BLOCK * DISPATCH_BLOCK
        pad_end = jnp.cumsum(padded)
        n_blocks = -(-(2 * n) // DISPATCH_BLOCK) + N_EXPERTS
        rows = n_blocks * DISPATCH_BLOCK
        nused = (pad_end[-1] // DISPATCH_BLOCK).astype(jnp.int32)
        blk = jnp.minimum(jnp.arange(n_blocks, dtype=jnp.int32), nused - 1)
        blk_e = jnp.minimum(jnp.sum(pad_end[None, :] <= (blk * DISPATCH_BLOCK)[:, None], axis=1), N_EXPERTS - 1).astype(jnp.int32)
        dest = _dest_rows(ri, padded.astype(jnp.int32))

        tail = jnp.concatenate([jnp.where(counts > 0, pad_end - DISPATCH_BLOCK, -1), nused.reshape(1)]).astype(jnp.int32)
        xs = _dispatch(xp, dest, tail, rows)
        first = jnp.concatenate([jnp.ones((1,), jnp.int32), (blk_e[1:] != blk_e[:-1]).astype(jnp.int32)])
        eids = jnp.arange(N_EXPERTS, dtype=jnp.int32)
        later = (eids[None, :] > blk_e[:, None]) & (counts[None, :] > 0)
        next_e = jnp.min(jnp.where(later, eids[None, :], N_EXPERTS), axis=1)
        next_e = jnp.where(next_e < N_EXPERTS, next_e, -1).astype(jnp.int32)
        ys = _experts(xs, w_expert_gu[l], w_expert_down[l], blk_e, blk, nused.reshape(1), first, next_e)
        h = _combine(h2, rf, dest, ys)
    return h.reshape(batch, seq, d)
```
